```python
import math
import jax, jax.numpy as jnp
from jax import lax
import numpy as np

D_MODEL = 1024
BATCH = 8
SEQ = 2048
DEPTH = 2

N_MIXERS = 2
N_GLA = (DEPTH + 1) // 2
N_DIFF = DEPTH // 2

GLA_HEADS = 4
GLA_DK = D_MODEL // 2
GLA_DV = D_MODEL
GLA_HK = GLA_DK // GLA_HEADS
GLA_HV = GLA_DV // GLA_HEADS
GLA_GATE_RANK = 16
GLA_TAU = 16.0
GLA_CHUNK = 64

DIFF_HEADS = 8
DIFF_HEAD_DIM = D_MODEL // (2 * DIFF_HEADS)
Q_BLOCK = 128

D_FF = -(-8 * D_MODEL // (3 * 256)) * 256

NORM_EPS = 1e-6
SUBLN_EPS = 1e-5

kernel_name = "hybrid_gla_diffattn_swiglu"


def rmsnorm(x, g, eps=NORM_EPS):
    xf = x.astype(jnp.float32)
    y = xf * lax.rsqrt(jnp.mean(xf * xf, axis=-1, keepdims=True) + eps)
    return (y * g.astype(jnp.float32)).astype(x.dtype)


def gla_mixer(h, w_in, w_gate_a, w_gate_b, b_gate, g_norm, w_out):
    B, S, _ = h.shape
    H, HK, HV, C = GLA_HEADS, GLA_HK, GLA_HV, GLA_CHUNK
    N = S // C
    proj = h @ w_in
    q, k, v, r = jnp.split(proj, [GLA_DK, 2 * GLA_DK, 2 * GLA_DK + GLA_DV], axis=-1)
    gk = jax.nn.log_sigmoid(((h @ w_gate_a) @ w_gate_b + b_gate).astype(jnp.float32)) / GLA_TAU

    def to_chunks(t, d):
        return t.reshape(B, N, C, H, d).transpose(1, 0, 3, 2, 4).astype(jnp.float32)

    qc = to_chunks(q, HK) * (HK ** -0.5)
    kc = to_chunks(k, HK)
    vc = to_chunks(v, HV)
    gc = to_chunks(gk, HK)
    causal = jnp.tril(jnp.ones((C, C), dtype=bool))

    def step(state, inp):
        qi, ki, vi, gi = inp
        b = jnp.cumsum(gi, axis=-2)
        o_inter = jnp.einsum('bhck,bhkv->bhcv', qi * jnp.exp(b), state)
        rel = b[:, :, :, None, :] - b[:, :, None, :, :]
        rel = jnp.where(causal[:, :, None], rel, -jnp.inf)
        attn = jnp.einsum('bhik,bhjk,bhijk->bhij', qi, ki, jnp.exp(rel))
        o = o_inter + jnp.einsum('bhij,bhjv->bhiv', attn, vi)
        b_last = b[:, :, -1:, :]
        new_state = (jnp.exp(b_last[:, :, 0, :, None]) * state
                     + jnp.einsum('bhck,bhcv->bhkv', ki * jnp.exp(b_last - b), vi))
        return new_state, o

    state0 = jnp.zeros((B, H, HK, HV), jnp.float32)
    _, o = lax.scan(step, state0, (qc, kc, vc, gc))
    o = o.transpose(1, 0, 3, 2, 4).reshape(B, S, H, HV)
    o = rmsnorm(o, g_norm).astype(h.dtype)
    o = o.reshape(B, S, GLA_DV) * jax.nn.silu(r)
    return o @ w_out


def diff_mixer(h, w_in, lam_q1, lam_k1, lam_q2, lam_k2, g_norm, w_out, lambda_init):
    B, S, _ = h.shape
    H, d = DIFF_HEADS, DIFF_HEAD_DIM
    proj = h @ w_in
    q, k, v = jnp.split(proj, 3, axis=-1)
    q = q.reshape(B, S, H, 2, d).transpose(0, 2, 3, 1, 4)
    k = k.reshape(B, S, H, 2, d).transpose(0, 2, 3, 1, 4)
    v = v.reshape(B, S, H, 2 * d).transpose(0, 2, 1, 3)
    lam = (jnp.exp(jnp.sum(lam_q1.astype(jnp.float32) * lam_k1.astype(jnp.float32)))
           - jnp.exp(jnp.sum(lam_q2.astype(jnp.float32) * lam_k2.astype(jnp.float32)))
           + lambda_init)
    scale = d ** -0.5
    outs = []
    for blk in range(S // Q_BLOCK):
        q0, kend = blk * Q_BLOCK, (blk + 1) * Q_BLOCK
        qb = q[:, :, :, q0:kend]
        kb = k[:, :, :, :kend]
        vb = v[:, :, :kend]
        s = jnp.einsum('bhcqd,bhckd->bhcqk', qb, kb).astype(jnp.float32) * scale
        qpos = q0 + jnp.arange(Q_BLOCK)
        kpos = jnp.arange(kend)
        s = jnp.where(kpos[None, :] <= qpos[:, None], s, -jnp.inf)
        p = jax.nn.softmax(s, axis=-1)
        a = p[:, :, 0] - lam * p[:, :, 1]
        outs.append(jnp.einsum('bhqk,bhkv->bhqv', a.astype(vb.dtype), vb))
    o = jnp.concatenate(outs, axis=2)
    o = rmsnorm(o, g_norm, SUBLN_EPS) * (1.0 - lambda_init)
    o = o.astype(h.dtype).transpose(0, 2, 1, 3).reshape(B, S, H * 2 * d)
    return o @ w_out


def swiglu(h, w_in, w_out):
    gate, up = jnp.split(h @ w_in, 2, axis=-1)
    return (jax.nn.silu(gate) * up) @ w_out


def setup_inputs(seed: int = 0) -> dict:
    key = jax.random.key(seed)
    ks = jax.random.split(key, 20)
    D = D_MODEL

    def w(k, shape, fan_in):
        return jax.random.normal(k, shape, jnp.float32) * fan_in ** -0.5

    def gain(k, shape):
        return 1.0 + 0.02 * jax.random.normal(k, shape, jnp.float32)

    gla_in_w = 2 * GLA_DK + 2 * GLA_DV
    return {
        "x": jax.random.normal(ks[0], (BATCH, SEQ, D), jnp.float32),
        "gla_w_in": w(ks[1], (N_GLA, D, gla_in_w), D),
        "gla_w_gate_a": w(ks[2], (N_GLA, D, GLA_GATE_RANK), D),
        "gla_w_gate_b": w(ks[3], (N_GLA, GLA_GATE_RANK, GLA_DK), GLA_GATE_RANK),
        "gla_b_gate": 0.1 * jax.random.normal(ks[4], (N_GLA, GLA_DK), jnp.float32),
        "gla_norm": gain(ks[5], (N_GLA, GLA_HV)),
        "gla_w_out": w(ks[6], (N_GLA, GLA_DV, D), GLA_DV),
        "diff_w_in": w(ks[7], (N_DIFF, D, 3 * D), D),
        "diff_lam_q1": 0.1 * jax.random.normal(ks[8], (N_DIFF, DIFF_HEAD_DIM), jnp.float32),
        "diff_lam_k1": 0.1 * jax.random.normal(ks[9], (N_DIFF, DIFF_HEAD_DIM), jnp.float32),
        "diff_lam_q2": 0.1 * jax.random.normal(ks[10], (N_DIFF, DIFF_HEAD_DIM), jnp.float32),
        "diff_lam_k2": 0.1 * jax.random.normal(ks[11], (N_DIFF, DIFF_HEAD_DIM), jnp.float32),
        "diff_norm": gain(ks[12], (N_DIFF, 2 * DIFF_HEAD_DIM)),
        "diff_w_out": w(ks[13], (N_DIFF, D, D), D),
        "norm_mixer": gain(ks[14], (DEPTH, D)),
        "norm_ffn": gain(ks[15], (DEPTH, D)),
        "ffn_w_in": w(ks[16], (DEPTH, D, 2 * D_FF), D),
        "ffn_w_out": w(ks[17], (DEPTH, D_FF, D), D_FF),
        "norm_final": gain(ks[18], (D,)),
    }


def reference(x, gla_w_in, gla_w_gate_a, gla_w_gate_b, gla_b_gate, gla_norm, gla_w_out,
              diff_w_in, diff_lam_q1, diff_lam_k1, diff_lam_q2, diff_lam_k2, diff_norm,
              diff_w_out, norm_mixer, norm_ffn, ffn_w_in, ffn_w_out, norm_final):
    for i in range(DEPTH):
        h = rmsnorm(x, norm_mixer[i])
        j = i // N_MIXERS
        if i % N_MIXERS == 0:
            y = gla_mixer(h, gla_w_in[j], gla_w_gate_a[j], gla_w_gate_b[j], gla_b_gate[j],
                          gla_norm[j], gla_w_out[j])
        else:
            lambda_init = 0.8 - 0.6 * math.exp(-0.3 * i)
            y = diff_mixer(h, diff_w_in[j], diff_lam_q1[j], diff_lam_k1[j], diff_lam_q2[j],
                           diff_lam_k2[j], diff_norm[j], diff_w_out[j], lambda_init)
        x = x + y
        x = x + swiglu(rmsnorm(x, norm_ffn[i]), ffn_w_in[i], ffn_w_out[i])
    return rmsnorm(x, norm_final)
```

```python
import functools
import math

import jax
import jax.numpy as jnp
from jax import lax
from jax.experimental import pallas as pl
from jax.experimental.pallas import tpu as pltpu

F32 = jnp.float32
BF16 = jnp.bfloat16

NORM_EPS = 1e-6
SUBLN_EPS = 1e-5

GLA_HEADS = 4
GLA_GATE_RANK = 16
GLA_TAU = 16.0
GLA_CHUNK = 64

DIFF_HEADS = 8

V7X_VMEM_BYTES = 64 * 1024 * 1024
VMEM_LIMIT_BYTES = V7X_VMEM_BYTES - 8 * 1024 * 1024

ROW_TILE = 512
PROJ_COL_TILE = 1024
FFN_COL_TILE = 256
GLA_ROW_TILE = 256
ATTN_Q_TILE = 256
ATTN_KV_TILE = 256


def _resident(shape):
    zeros = (0,) * len(shape)
    return pl.BlockSpec(shape, lambda *_: zeros, pipeline_mode=pl.Buffered(1))


def _rmsnorm_rows(x, gain, eps):
    return x * lax.rsqrt(jnp.mean(x * x, axis=-1, keepdims=True) + eps) * gain


def _dot(a, b):
    return jnp.dot(a, b, preferred_element_type=F32)


def _dot_nt(a, b):
    return lax.dot_general(a, b, (((1,), (1,)), ((), ())), preferred_element_type=F32)


def _dot_tn(a, b):
    return lax.dot_general(a, b, (((0,), (0,)), ((), ())), preferred_element_type=F32)


def _norm_proj_body(x_ref, g_ref, w_ref, o_ref):
    h = _rmsnorm_rows(x_ref[...], g_ref[...], NORM_EPS).astype(BF16)
    n_out = o_ref.shape[1]
    for c0 in range(0, n_out, PROJ_COL_TILE):
        o_ref[:, c0:c0 + PROJ_COL_TILE] = _dot(h, w_ref[:, c0:c0 + PROJ_COL_TILE]).astype(o_ref.dtype)
    return h


def _norm_proj_kernel(x_ref, g_ref, w_ref, o_ref):
    _norm_proj_body(x_ref, g_ref, w_ref, o_ref)


def _norm_proj_gate_kernel(x_ref, g_ref, w_ref, wa_ref, wb_ref, bg_ref, o_ref, gk_ref):
    h = _norm_proj_body(x_ref, g_ref, w_ref, o_ref)
    low = _dot(h, wa_ref[...])
    logits = _dot(low.astype(BF16), wb_ref[...]) + bg_ref[...]
    log_sig = jnp.minimum(logits, 0.0) - jnp.log1p(jnp.exp(-jnp.abs(logits)))
    gk_ref[...] = log_sig / GLA_TAU


def _norm_proj(x2d, gain, w, gate=None):
    t, d = x2d.shape
    n_out = w.shape[1]
    assert t % ROW_TILE == 0 and n_out % PROJ_COL_TILE == 0
    in_specs = [pl.BlockSpec((ROW_TILE, d), lambda i: (i, 0)), _resident((1, d)), _resident((d, n_out))]
    args = [x2d, gain.reshape(1, d), w.astype(BF16)]
    out_shape = [jax.ShapeDtypeStruct((t, n_out), BF16)]
    out_specs = [pl.BlockSpec((ROW_TILE, n_out), lambda i: (i, 0))]
    kernel = _norm_proj_kernel
    if gate is not None:
        w_a, w_b, b_g = gate
        rank, dk = w_b.shape
        in_specs += [_resident((d, rank)), _resident((rank, dk)), _resident((1, dk))]
        args += [w_a.astype(BF16), w_b.astype(BF16), b_g.reshape(1, dk)]
        out_shape.append(jax.ShapeDtypeStruct((t, dk), F32))
        out_specs.append(pl.BlockSpec((ROW_TILE, dk), lambda i: (i, 0)))
        kernel = _norm_proj_gate_kernel
    outs = pl.pallas_call(
        kernel,
        grid=(t // ROW_TILE,),
        in_specs=in_specs,
        out_specs=out_specs,
        out_shape=out_shape,
        compiler_params=pltpu.CompilerParams(
            dimension_semantics=("parallel",), vmem_limit_bytes=VMEM_LIMIT_BYTES),
        name="norm_proj_gate" if gate is not None else "norm_proj",
    )(*args)
    return outs if gate is not None else outs[0]


def _gla_kernel(q_ref, k_ref, v_ref, r_ref, gk_ref, gn_ref, o_ref, state_ref):
    @pl.when(pl.program_id(2) == 0)
    def _():
        state_ref[...] = jnp.zeros_like(state_ref)

    c = GLA_CHUNK
    hk = q_ref.shape[2]
    row = lax.broadcasted_iota(jnp.int32, (c, c), 0)
    col = lax.broadcasted_iota(jnp.int32, (c, c), 1)
    causal = col <= row
    tri = causal.astype(F32)
    gain = gn_ref[...]
    q_scale = hk ** -0.5

    for c0 in range(0, q_ref.shape[1], c):
        rows = slice(c0, c0 + c)
        b = jnp.dot(tri, gk_ref[0, rows, :], preferred_element_type=F32,
                    precision=lax.Precision.HIGHEST)
        b_last = b[c - 1:c, :]
        b_mid = b[c // 2:c // 2 + 1, :]
        q = q_ref[0, rows, :].astype(F32) * q_scale
        k = k_ref[0, rows, :].astype(F32)
        v = v_ref[0, rows, :]
        state_t = state_ref[...]
        q_mid = (q * jnp.exp(b - b_mid)).astype(BF16)
        k_mid = (k * jnp.exp(b_mid - b)).astype(BF16)
        attn = jnp.where(causal, _dot_nt(q_mid, k_mid), 0.0)
        o = _dot_nt((q * jnp.exp(b)).astype(BF16), state_t.astype(BF16))
        o = o + _dot(attn.astype(BF16), v)
        k_end = (k * jnp.exp(b_last - b)).astype(BF16)
        state_ref[...] = state_t * jnp.exp(b_last) + _dot_tn(v, k_end)
        o = _rmsnorm_rows(o, gain, NORM_EPS)
        r = r_ref[0, rows, :].astype(F32)
        o_ref[0, rows, :] = (o * (r * jax.nn.sigmoid(r))).astype(o_ref.dtype)


def _gla_core(proj, gk, g_norm, batch, seq):
    dk = gk.shape[1]
    hk = dk // GLA_HEADS
    hv = g_norm.shape[0]
    dv = hv * GLA_HEADS
    assert proj.shape[1] == 2 * dk + 2 * dv and seq % GLA_ROW_TILE == 0 and GLA_ROW_TILE % GLA_CHUNK == 0
    proj3 = proj.reshape(batch, seq, proj.shape[1])
    gk3 = gk.reshape(batch, seq, dk)
    ts = GLA_ROW_TILE
    k_off, v_off, r_off = dk // hk, (2 * dk) // hv, (2 * dk + dv) // hv
    return pl.pallas_call(
        _gla_kernel,
        grid=(batch, GLA_HEADS, seq // ts),
        in_specs=[
            pl.BlockSpec((1, ts, hk), lambda b, h, s: (b, s, h)),
            pl.BlockSpec((1, ts, hk), lambda b, h, s: (b, s, k_off + h)),
            pl.BlockSpec((1, ts, hv), lambda b, h, s: (b, s, v_off + h)),
            pl.BlockSpec((1, ts, hv), lambda b, h, s: (b, s, r_off + h)),
            pl.BlockSpec((1, ts, hk), lambda b, h, s: (b, s, h)),
            _resident((1, hv)),
        ],
        out_specs=pl.BlockSpec((1, ts, hv), lambda b, h, s: (b, s, h)),
        out_shape=jax.ShapeDtypeStruct((batch, seq, dv), BF16),
        scratch_shapes=[pltpu.VMEM((hv, hk), F32)],
        compiler_params=pltpu.CompilerParams(
            dimension_semantics=("parallel", "parallel", "arbitrary"), vmem_limit_bytes=VMEM_LIMIT_BYTES),
        name="gla_core",
    )(proj3, proj3, proj3, proj3, gk3, g_norm.reshape(1, hv))


def _diff_attn_kernel(lq1_ref, lk1_ref, lq2_ref, lk2_ref, gn_ref, q_ref, k_ref, v_ref, o_ref,
                      qq_ref, m_ref, l_ref, acc_ref, *, lambda_init):
    tq = q_ref.shape[1]
    d2 = q_ref.shape[2]
    d = d2 // 2
    tk = ATTN_KV_TILE
    qi = pl.program_id(2)

    q = q_ref[0]
    lane = lax.broadcasted_iota(jnp.int32, (tq, d2), 1)
    scale = jnp.asarray(d ** -0.5, q.dtype)
    zero = jnp.zeros_like(q)
    qq_ref[0:tq, :] = jnp.where(lane < d, q, zero) * scale
    qq_ref[tq:2 * tq, :] = jnp.where(lane >= d, q, zero) * scale
    m_ref[...] = jnp.full_like(m_ref, -jnp.inf)
    l_ref[...] = jnp.zeros_like(l_ref)
    acc_ref[...] = jnp.zeros_like(acc_ref)

    def step(j, masked):
        k = k_ref[0, pl.ds(pl.multiple_of(j * tk, tk), tk), :]
        v = v_ref[0, pl.ds(pl.multiple_of(j * tk, tk), tk), :]
        s = _dot_nt(qq_ref[...], k)
        if masked:
            r_pos = lax.broadcasted_iota(jnp.int32, (2 * tq, tk), 0) & (tq - 1)
            c_pos = lax.broadcasted_iota(jnp.int32, (2 * tq, tk), 1)
            s = jnp.where(c_pos <= r_pos, s, -jnp.inf)
        m_prev = m_ref[...]
        m_new = jnp.maximum(m_prev, jnp.max(s, axis=-1, keepdims=True))
        p = jnp.exp(s - m_new)
        alpha = jnp.exp(m_prev - m_new)
        l_ref[...] = alpha * l_ref[...] + jnp.sum(p, axis=-1, keepdims=True)
        acc_ref[...] = alpha * acc_ref[...] + _dot(p.astype(v.dtype), v)
        m_ref[...] = m_new

    def full_step(j, carry):
        step(j, masked=False)
        return carry

    n_full = qi * (tq // tk)
    lax.fori_loop(0, n_full, full_step, 0)
    assert tq == tk
    step(qi, masked=True)

    lam = (jnp.exp(jnp.sum(lq1_ref[...] * lk1_ref[...], keepdims=True))
           - jnp.exp(jnp.sum(lq2_ref[...] * lk2_ref[...], keepdims=True)) + lambda_init)
    o_all = acc_ref[...] / l_ref[...]
    o = o_all[0:tq, :] - lam * o_all[tq:2 * tq, :]
    o = _rmsnorm_rows(o, gn_ref[...], SUBLN_EPS) * (1.0 - lambda_init)
    o_ref[0] = o.astype(o_ref.dtype)


def _diff_attn(proj, lam_q1, lam_k1, lam_q2, lam_k2, g_norm, batch, seq, lambda_init):
    d_model = proj.shape[1] // 3
    d2 = d_model // DIFF_HEADS
    d = d2 // 2
    tq = ATTN_Q_TILE
    assert seq % tq == 0 and g_norm.shape[0] == d2
    proj3 = proj.reshape(batch, seq, 3 * d_model)
    lam_specs = [_resident((1, d))] * 4
    return pl.pallas_call(
        functools.partial(_diff_attn_kernel, lambda_init=lambda_init),
        grid=(batch, DIFF_HEADS, seq // tq),
        in_specs=lam_specs + [
            _resident((1, d2)),
            pl.BlockSpec((1, tq, d2), lambda b, h, i: (b, i, h)),
            pl.BlockSpec((1, seq, d2), lambda b, h, i: (b, 0, DIFF_HEADS + h)),
            pl.BlockSpec((1, seq, d2), lambda b, h, i: (b, 0, 2 * DIFF_HEADS + h)),
        ],
        out_specs=pl.BlockSpec((1, tq, d2), lambda b, h, i: (b, i, h)),
        out_shape=jax.ShapeDtypeStruct((batch, seq, d_model), BF16),
        scratch_shapes=[
            pltpu.VMEM((2 * tq, d2), BF16),
            pltpu.VMEM((2 * tq, 1), F32),
            pltpu.VMEM((2 * tq, 1), F32),
            pltpu.VMEM((2 * tq, d2), F32),
        ],
        compiler_params=pltpu.CompilerParams(
            dimension_semantics=("parallel", "parallel", "arbitrary"), vmem_limit_bytes=VMEM_LIMIT_BYTES),
        name="diff_attn",
    )(lam_q1.reshape(1, d), lam_k1.reshape(1, d), lam_q2.reshape(1, d), lam_k2.reshape(1, d),
      g_norm.reshape(1, d2), proj3, proj3, proj3)


def _mix_out_ffn_kernel(x_ref, a_ref, wo_ref, gf_ref, wi_ref, w2_ref, gl_ref, o_ref, acc_ref, *, final_norm):
    d_ff = w2_ref.shape[0]
    x1 = x_ref[...] + _dot(a_ref[...], wo_ref[...])
    h = _rmsnorm_rows(x1, gf_ref[...], NORM_EPS).astype(BF16)
    acc_ref[...] = x1
    for f0 in range(0, d_ff, FFN_COL_TILE):
        gate = _dot(h, wi_ref[:, f0:f0 + FFN_COL_TILE])
        up = _dot(h, wi_ref[:, d_ff + f0:d_ff + f0 + FFN_COL_TILE])
        act = (gate * jax.nn.sigmoid(gate) * up).astype(BF16)
        acc_ref[...] += _dot(act, w2_ref[f0:f0 + FFN_COL_TILE, :])
    x2 = acc_ref[...]
    if final_norm:
        x2 = _rmsnorm_rows(x2, gl_ref[...], NORM_EPS)
    o_ref[...] = x2


def _mix_out_ffn(x2d, mix, w_out, g_ffn, w_in, w2, g_last, final_norm):
    t, d = x2d.shape
    d_ff = w2.shape[0]
    assert t % ROW_TILE == 0 and d_ff % FFN_COL_TILE == 0
    row_spec = pl.BlockSpec((ROW_TILE, d), lambda i: (i, 0))
    return pl.pallas_call(
        functools.partial(_mix_out_ffn_kernel, final_norm=final_norm),
        grid=(t // ROW_TILE,),
        in_specs=[row_spec, row_spec, _resident((d, d)), _resident((1, d)),
                  _resident((d, 2 * d_ff)), _resident((d_ff, d)), _resident((1, d))],
        out_specs=row_spec,
        out_shape=jax.ShapeDtypeStruct((t, d), F32),
        scratch_shapes=[pltpu.VMEM((ROW_TILE, d), F32)],
        compiler_params=pltpu.CompilerParams(
            dimension_semantics=("parallel",), vmem_limit_bytes=VMEM_LIMIT_BYTES),
        name="mix_out_ffn",
    )(x2d, mix, w_out.astype(BF16), g_ffn.reshape(1, d), w_in.astype(BF16), w2.astype(BF16),
      g_last.reshape(1, d))


def kernel(x, gla_w_in, gla_w_gate_a, gla_w_gate_b, gla_b_gate, gla_norm, gla_w_out, diff_w_in, diff_lam_q1, diff_lam_k1, diff_lam_q2, diff_lam_k2, diff_norm, diff_w_out, norm_mixer, norm_ffn, ffn_w_in, ffn_w_out, norm_final):
    batch, seq, d = x.shape
    x2d = x.reshape(batch * seq, d)

    proj, gk = _norm_proj(x2d, norm_mixer[0], gla_w_in[0],
                          gate=(gla_w_gate_a[0], gla_w_gate_b[0], gla_b_gate[0]))
    mix = _gla_core(proj, gk, gla_norm[0], batch, seq).reshape(batch * seq, -1)
    x2d = _mix_out_ffn(x2d, mix, gla_w_out[0], norm_ffn[0], ffn_w_in[0], ffn_w_out[0], norm_final,
                       final_norm=False)

    lambda_init = 0.8 - 0.6 * math.exp(-0.3 * 1)
    proj = _norm_proj(x2d, norm_mixer[1], diff_w_in[0])
    mix = _diff_attn(proj, diff_lam_q1[0], diff_lam_k1[0], diff_lam_q2[0], diff_lam_k2[0],
                     diff_norm[0], batch, seq, lambda_init).reshape(batch * seq, -1)
    x2d = _mix_out_ffn(x2d, mix, diff_w_out[0], norm_ffn[1], ffn_w_in[1], ffn_w_out[1], norm_final,
                       final_norm=True)
    return x2d.reshape(batch, seq, d)
```

```python
import functools
import math

import jax
import jax.numpy as jnp
from jax import lax
from jax.experimental import pallas as pl
from jax.experimental.pallas import tpu as pltpu

F32 = jnp.float32
BF16 = jnp.bfloat16

NORM_EPS = 1e-6
SUBLN_EPS = 1e-5

GLA_HEADS = 4
GLA_GATE_RANK = 16
GLA_TAU = 16.0
GLA_CHUNK = 64

DIFF_HEADS = 8

V7X_VMEM_BYTES = 64 * 1024 * 1024
VMEM_LIMIT_BYTES = V7X_VMEM_BYTES - 8 * 1024 * 1024

ROW_TILE = 512
PROJ_COL_TILE = 1024
FFN_COL_TILE = 256
GLA_GROUP = 256
ATTN_TILE = ROW_TILE


def _resident(shape):
    zeros = (0,) * len(shape)
    return pl.BlockSpec(shape, lambda *_: zeros, pipeline_mode=pl.Buffered(1))


def _rmsnorm_rows(x, gain, eps):
    return x * lax.rsqrt(jnp.mean(x * x, axis=-1, keepdims=True) + eps) * gain


def _dot(a, b):
    return jnp.dot(a, b, preferred_element_type=F32)


def _dot_nt(a, b):
    return lax.dot_general(a, b, (((1,), (1,)), ((), ())), preferred_element_type=F32)


def _dot_tn(a, b):
    return lax.dot_general(a, b, (((0,), (0,)), ((), ())), preferred_element_type=F32)


def _norm_proj_body(x_ref, g_ref, w_ref, o_ref):
    h = _rmsnorm_rows(x_ref[...], g_ref[...], NORM_EPS).astype(BF16)
    n_out = o_ref.shape[1]
    for c0 in range(0, n_out, PROJ_COL_TILE):
        o_ref[:, c0:c0 + PROJ_COL_TILE] = _dot(h, w_ref[:, c0:c0 + PROJ_COL_TILE]).astype(o_ref.dtype)
    return h


def _norm_proj_gate_kernel(x_ref, g_ref, w_ref, wa_ref, wb_ref, bg_ref, o_ref, gk_ref):
    h = _norm_proj_body(x_ref, g_ref, w_ref, o_ref)
    low = _dot(h, wa_ref[...])
    logits = _dot(low.astype(BF16), wb_ref[...]) + bg_ref[...]
    log_sig = jnp.minimum(logits, 0.0) - jnp.log1p(jnp.exp(-jnp.abs(logits)))
    gk_ref[...] = log_sig / GLA_TAU


def _norm_proj_vt_kernel(x_ref, g_ref, w_ref, wvt_ref, o_ref, vt_ref):
    h = _norm_proj_body(x_ref, g_ref, w_ref, o_ref)
    vt_ref[0, 0] = _dot_nt(wvt_ref[...], h).astype(vt_ref.dtype)


def _norm_proj_call(kernel, name, x2d, gain, w, extra_in, extra_specs, extra_out_shape, extra_out_spec):
    t, d = x2d.shape
    n_out = w.shape[1]
    assert t % ROW_TILE == 0 and n_out % PROJ_COL_TILE == 0
    return pl.pallas_call(
        kernel,
        grid=(t // ROW_TILE,),
        in_specs=[pl.BlockSpec((ROW_TILE, d), lambda i: (i, 0)), _resident((1, d)),
                  _resident((d, n_out))] + extra_specs,
        out_specs=[pl.BlockSpec((ROW_TILE, n_out), lambda i: (i, 0)), extra_out_spec],
        out_shape=[jax.ShapeDtypeStruct((t, n_out), BF16), extra_out_shape],
        compiler_params=pltpu.CompilerParams(
            dimension_semantics=("parallel",), vmem_limit_bytes=VMEM_LIMIT_BYTES),
        name=name,
    )(x2d, gain.reshape(1, d), w.astype(BF16), *extra_in)


def _norm_proj_gate(x2d, gain, w, w_a, w_b, b_g):
    t, d = x2d.shape
    rank, dk = w_b.shape
    return _norm_proj_call(
        _norm_proj_gate_kernel, "norm_proj_gate", x2d, gain, w,
        [w_a.astype(BF16), w_b.astype(BF16), b_g.reshape(1, dk)],
        [_resident((d, rank)), _resident((rank, dk)), _resident((1, dk))],
        jax.ShapeDtypeStruct((t, dk), F32), pl.BlockSpec((ROW_TILE, dk), lambda i: (i, 0)))


def _norm_proj_vt(x2d, gain, w_qk, w_v, batch, seq):
    t, d = x2d.shape
    d_v = w_v.shape[1]
    tiles = seq // ROW_TILE
    return _norm_proj_call(
        _norm_proj_vt_kernel, "norm_proj_vt", x2d, gain, w_qk,
        [w_v.T.astype(BF16)], [_resident((d_v, d))],
        jax.ShapeDtypeStruct((batch, tiles, d_v, ROW_TILE), BF16),
        pl.BlockSpec((1, 1, d_v, ROW_TILE), lambda i: (i // tiles, i % tiles, 0, 0)))


def _gla_kernel(q_ref, k_ref, v_ref, r_ref, gk_ref, gn_ref, o_ref):
    c = GLA_CHUNK
    grp = GLA_GROUP
    n = grp // c
    seq, hk = q_ref.shape[1], q_ref.shape[2]
    hv = v_ref.shape[2]
    shift = c.bit_length() - 1
    row = lax.broadcasted_iota(jnp.int32, (grp, grp), 0)
    col = lax.broadcasted_iota(jnp.int32, (grp, grp), 1)
    mask = (col <= row) & ((row >> shift) == (col >> shift))
    tri = jnp.where(mask, 1.0, 0.0).astype(BF16)
    gain = gn_ref[...]
    q_scale = hk ** -0.5

    def group(gi, state_t):
        rows = pl.ds(pl.multiple_of(gi * grp, grp), grp)
        g = gk_ref[0, rows, :]
        g_hi = g.astype(BF16)
        g_lo = (g - g_hi.astype(F32)).astype(BF16)
        b = (_dot(tri, g_hi) + _dot(tri, g_lo)).reshape(n, c, hk)
        b_mid = b[:, c // 2:c // 2 + 1, :]
        b_last = b[:, c - 1:c, :]
        q = (q_ref[0, rows, :].astype(F32) * q_scale).reshape(n, c, hk)
        k = k_ref[0, rows, :].astype(F32).reshape(n, c, hk)
        v = v_ref[0, rows, :]
        q_mid = (q * jnp.exp(b - b_mid)).reshape(grp, hk).astype(BF16)
        k_mid = (k * jnp.exp(b_mid - b)).reshape(grp, hk).astype(BF16)
        attn = jnp.where(mask, _dot_nt(q_mid, k_mid), 0.0)
        o_intra = _dot(attn.astype(BF16), v)
        q_dec = (q * jnp.exp(b)).astype(BF16)
        k_end = (k * jnp.exp(b_last - b)).astype(BF16)
        decay = jnp.exp(b_last)
        outs = []
        for ci in range(n):
            rc = slice(ci * c, (ci + 1) * c)
            outs.append(o_intra[rc] + _dot_nt(q_dec[ci], state_t.astype(BF16)))
            state_t = state_t * decay[ci] + _dot_tn(v[rc], k_end[ci])
        o = _rmsnorm_rows(jnp.concatenate(outs, axis=0), gain, NORM_EPS)
        r = r_ref[0, rows, :].astype(F32)
        o_ref[0, rows, :] = (o * (r * jax.nn.sigmoid(r))).astype(o_ref.dtype)
        return state_t

    lax.fori_loop(0, seq // grp, group, jnp.zeros((hv, hk), F32))


def _gla_core(proj, gk, g_norm, batch, seq):
    dk = gk.shape[1]
    hk = dk // GLA_HEADS
    hv = g_norm.shape[0]
    dv = hv * GLA_HEADS
    assert proj.shape[1] == 2 * dk + 2 * dv and seq % GLA_GROUP == 0 and GLA_GROUP % GLA_CHUNK == 0
    proj3 = proj.reshape(batch, seq, proj.shape[1])
    gk3 = gk.reshape(batch, seq, dk)
    k_off, v_off, r_off = dk // hk, (2 * dk) // hv, (2 * dk + dv) // hv
    return pl.pallas_call(
        _gla_kernel,
        grid=(batch, GLA_HEADS),
        in_specs=[
            pl.BlockSpec((1, seq, hk), lambda b, h: (b, 0, h)),
            pl.BlockSpec((1, seq, hk), lambda b, h: (b, 0, k_off + h)),
            pl.BlockSpec((1, seq, hv), lambda b, h: (b, 0, v_off + h)),
            pl.BlockSpec((1, seq, hv), lambda b, h: (b, 0, r_off + h)),
            pl.BlockSpec((1, seq, hk), lambda b, h: (b, 0, h)),
            _resident((1, hv)),
        ],
        out_specs=pl.BlockSpec((1, seq, hv), lambda b, h: (b, 0, h)),
        out_shape=jax.ShapeDtypeStruct((batch, seq, dv), BF16),
        compiler_params=pltpu.CompilerParams(
            dimension_semantics=("parallel", "parallel"), vmem_limit_bytes=VMEM_LIMIT_BYTES),
        name="gla_core",
    )(proj3, proj3, proj3, proj3, gk3, g_norm.reshape(1, hv))


def _diff_attn_kernel(lq1_ref, lk1_ref, lq2_ref, lk2_ref, gn_ref, q_ref, k_ref, vt_ref, o_ref,
                      qq_ref, m_ref, l_ref, acc_ref, *, lambda_init):
    tq = q_ref.shape[1]
    d2 = q_ref.shape[2]
    d = d2 // 2
    tk = vt_ref.shape[3]
    assert tq == tk
    qi = pl.program_id(2)

    q = q_ref[0]
    lane = lax.broadcasted_iota(jnp.int32, (tq, d2), 1)
    scale = jnp.asarray(d ** -0.5, q.dtype)
    zero = jnp.zeros_like(q)
    qq_ref[0:tq, :] = jnp.where(lane < d, q, zero) * scale
    qq_ref[tq:2 * tq, :] = jnp.where(lane >= d, q, zero) * scale
    m_ref[...] = jnp.full_like(m_ref, -jnp.inf)
    l_ref[...] = jnp.zeros_like(l_ref)
    acc_ref[...] = jnp.zeros_like(acc_ref)

    def step(j, masked):
        k = k_ref[0, pl.ds(pl.multiple_of(j * tk, tk), tk), :]
        st = _dot_nt(k, qq_ref[...])
        if masked:
            k_pos = lax.broadcasted_iota(jnp.int32, (tk, 2 * tq), 0)
            q_pos = lax.broadcasted_iota(jnp.int32, (tk, 2 * tq), 1) & (tq - 1)
            st = jnp.where(k_pos <= q_pos, st, -jnp.inf)
        m_prev = m_ref[...]
        m_new = jnp.maximum(m_prev, jnp.max(st, axis=0, keepdims=True))
        p = jnp.exp(st - m_new)
        alpha = jnp.exp(m_prev - m_new)
        l_ref[...] = alpha * l_ref[...] + jnp.sum(p, axis=0, keepdims=True)
        acc_ref[...] = alpha * acc_ref[...] + _dot(vt_ref[0, j], p.astype(BF16))
        m_ref[...] = m_new

    def full_step(j, carry):
        step(j, masked=False)
        return carry

    lax.fori_loop(0, qi, full_step, 0)
    step(qi, masked=True)

    lam = (jnp.exp(jnp.sum(lq1_ref[...] * lk1_ref[...], keepdims=True))
           - jnp.exp(jnp.sum(lq2_ref[...] * lk2_ref[...], keepdims=True)) + lambda_init)
    o_all = acc_ref[...] / l_ref[...]
    o_t = o_all[:, 0:tq] - lam * o_all[:, tq:2 * tq]
    inv_rms = lax.rsqrt(jnp.mean(o_t * o_t, axis=0, keepdims=True) + SUBLN_EPS)
    o_t = o_t * inv_rms * (gn_ref[...] * (1.0 - lambda_init))
    o_ref[0] = o_t.T.astype(o_ref.dtype)


def _diff_attn(qk, v_t, lam_q1, lam_k1, lam_q2, lam_k2, g_norm, batch, seq, lambda_init):
    d_model = qk.shape[1] // 2
    d2 = d_model // DIFF_HEADS
    d = d2 // 2
    tq = ATTN_TILE
    tiles = seq // tq
    assert seq % tq == 0 and g_norm.shape[0] == d2 and v_t.shape == (batch, tiles, d_model, tq)
    qk3 = qk.reshape(batch, seq, 2 * d_model)
    return pl.pallas_call(
        functools.partial(_diff_attn_kernel, lambda_init=lambda_init),
        grid=(batch, DIFF_HEADS, tiles),
        in_specs=[_resident((1, d))] * 4 + [
            _resident((d2, 1)),
            pl.BlockSpec((1, tq, d2), lambda b, h, i: (b, i, h)),
            pl.BlockSpec((1, seq, d2), lambda b, h, i: (b, 0, DIFF_HEADS + h)),
            pl.BlockSpec((1, tiles, d2, tq), lambda b, h, i: (b, 0, h, 0)),
        ],
        out_specs=pl.BlockSpec((1, tq, d2), lambda b, h, i: (b, i, h)),
        out_shape=jax.ShapeDtypeStruct((batch, seq, d_model), BF16),
        scratch_shapes=[
            pltpu.VMEM((2 * tq, d2), BF16),
            pltpu.VMEM((1, 2 * tq), F32),
            pltpu.VMEM((1, 2 * tq), F32),
            pltpu.VMEM((d2, 2 * tq), F32),
        ],
        compiler_params=pltpu.CompilerParams(
            dimension_semantics=("parallel", "parallel", "arbitrary"), vmem_limit_bytes=VMEM_LIMIT_BYTES),
        name="diff_attn",
    )(lam_q1.reshape(1, d), lam_k1.reshape(1, d), lam_q2.reshape(1, d), lam_k2.reshape(1, d),
      g_norm.reshape(d2, 1), qk3, qk3, v_t)


def _mix_out_ffn_kernel(x_ref, a_ref, wo_ref, gf_ref, wi_ref, w2_ref, gl_ref, o_ref, acc_ref, *, final_norm):
    d_ff = w2_ref.shape[0]
    x1 = x_ref[...] + _dot(a_ref[...], wo_ref[...])
    h = _rmsnorm_rows(x1, gf_ref[...], NORM_EPS).astype(BF16)
    acc_ref[...] = x1
    for f0 in range(0, d_ff, FFN_COL_TILE):
        gate = _dot(h, wi_ref[:, f0:f0 + FFN_COL_TILE])
        up = _dot(h, wi_ref[:, d_ff + f0:d_ff + f0 + FFN_COL_TILE])
        act = (gate * jax.nn.sigmoid(gate) * up).astype(BF16)
        acc_ref[...] += _dot(act, w2_ref[f0:f0 + FFN_COL_TILE, :])
    x2 = acc_ref[...]
    if final_norm:
        x2 = _rmsnorm_rows(x2, gl_ref[...], NORM_EPS)
    o_ref[...] = x2


def _mix_out_ffn(x2d, mix, w_out, g_ffn, w_in, w2, g_last, final_norm):
    t, d = x2d.shape
    d_ff = w2.shape[0]
    assert t % ROW_TILE == 0 and d_ff % FFN_COL_TILE == 0
    row_spec = pl.BlockSpec((ROW_TILE, d), lambda i: (i, 0))
    return pl.pallas_call(
        functools.partial(_mix_out_ffn_kernel, final_norm=final_norm),
        grid=(t // ROW_TILE,),
        in_specs=[row_spec, row_spec, _resident((d, d)), _resident((1, d)),
                  _resident((d, 2 * d_ff)), _resident((d_ff, d)), _resident((1, d))],
        out_specs=row_spec,
        out_shape=jax.ShapeDtypeStruct((t, d), F32),
        scratch_shapes=[pltpu.VMEM((ROW_TILE, d), F32)],
        compiler_params=pltpu.CompilerParams(
            dimension_semantics=("parallel",), vmem_limit_bytes=VMEM_LIMIT_BYTES),
        name="mix_out_ffn",
    )(x2d, mix, w_out.astype(BF16), g_ffn.reshape(1, d), w_in.astype(BF16), w2.astype(BF16),
      g_last.reshape(1, d))


def kernel(x, gla_w_in, gla_w_gate_a, gla_w_gate_b, gla_b_gate, gla_norm, gla_w_out, diff_w_in, diff_lam_q1, diff_lam_k1, diff_lam_q2, diff_lam_k2, diff_norm, diff_w_out, norm_mixer, norm_ffn, ffn_w_in, ffn_w_out, norm_final):
    batch, seq, d = x.shape
    x2d = x.reshape(batch * seq, d)

    proj, gk = _norm_proj_gate(x2d, norm_mixer[0], gla_w_in[0],
                               gla_w_gate_a[0], gla_w_gate_b[0], gla_b_gate[0])
    mix = _gla_core(proj, gk, gla_norm[0], batch, seq).reshape(batch * seq, -1)
    x2d = _mix_out_ffn(x2d, mix, gla_w_out[0], norm_ffn[0], ffn_w_in[0], ffn_w_out[0], norm_final,
                       final_norm=False)

    lambda_init = 0.8 - 0.6 * math.exp(-0.3 * 1)
    qk, v_t = _norm_proj_vt(x2d, norm_mixer[1], diff_w_in[0][:, :2 * d], diff_w_in[0][:, 2 * d:],
                            batch, seq)
    mix = _diff_attn(qk, v_t, diff_lam_q1[0], diff_lam_k1[0], diff_lam_q2[0], diff_lam_k2[0],
                     diff_norm[0], batch, seq, lambda_init).reshape(batch * seq, -1)
    x2d = _mix_out_ffn(x2d, mix, diff_w_out[0], norm_ffn[1], ffn_w_in[1], ffn_w_out[1], norm_final,
                       final_norm=True)
    return x2d.reshape(batch, seq, d)
```

```python
import functools
import math

import jax
import jax.numpy as jnp
from jax import lax
from jax.experimental import pallas as pl
from jax.experimental.pallas import tpu as pltpu

F32 = jnp.float32
BF16 = jnp.bfloat16

NORM_EPS = 1e-6
SUBLN_EPS = 1e-5

GLA_HEADS = 4
GLA_GATE_RANK = 16
GLA_TAU = 16.0
GLA_CHUNK = 64

DIFF_HEADS = 8

V7X_VMEM_BYTES = 64 * 1024 * 1024
VMEM_LIMIT_BYTES = V7X_VMEM_BYTES - 8 * 1024 * 1024

ROW_TILE = 512
PROJ_COL_TILE = 1024
FFN_COL_TILE = 256
GLA_GROUP = 256
ATTN_TILE = ROW_TILE
ATTN_SUB_TILE = 256
ATTN_SUM_ROWS = 16
ATTN_LOOKAHEAD = 2


def _resident(shape):
    zeros = (0,) * len(shape)
    return pl.BlockSpec(shape, lambda *_: zeros, pipeline_mode=pl.Buffered(1))


def _rmsnorm_rows(x, gain, eps):
    return x * lax.rsqrt(jnp.mean(x * x, axis=-1, keepdims=True) + eps) * gain


def _dot(a, b):
    return jnp.dot(a, b, preferred_element_type=F32)


def _dot_nt(a, b):
    return lax.dot_general(a, b, (((1,), (1,)), ((), ())), preferred_element_type=F32)


def _dot_tn(a, b):
    return lax.dot_general(a, b, (((0,), (0,)), ((), ())), preferred_element_type=F32)


def _norm_proj_body(x_ref, g_ref, w_ref, o_ref):
    h = _rmsnorm_rows(x_ref[...], g_ref[...], NORM_EPS).astype(BF16)
    n_out = o_ref.shape[1]
    for c0 in range(0, n_out, PROJ_COL_TILE):
        o_ref[:, c0:c0 + PROJ_COL_TILE] = _dot(h, w_ref[:, c0:c0 + PROJ_COL_TILE]).astype(o_ref.dtype)
    return h


def _norm_proj_gate_kernel(x_ref, g_ref, w_ref, wa_ref, wb_ref, bg_ref, o_ref, gk_ref):
    h = _norm_proj_body(x_ref, g_ref, w_ref, o_ref)
    low = _dot(h, wa_ref[...])
    logits = _dot(low.astype(BF16), wb_ref[...]) + bg_ref[...]
    log_sig = jnp.minimum(logits, 0.0) - jnp.log1p(jnp.exp(-jnp.abs(logits)))
    gk_ref[...] = log_sig / GLA_TAU


def _norm_proj_vt_kernel(x_ref, g_ref, w_ref, wvt_ref, o_ref, vt_ref):
    h = _norm_proj_body(x_ref, g_ref, w_ref, o_ref)
    vt_ref[0, 0] = _dot_nt(wvt_ref[...], h).astype(vt_ref.dtype)


def _norm_proj_call(kernel, name, x2d, gain, w, extra_in, extra_specs, extra_out_shape, extra_out_spec):
    t, d = x2d.shape
    n_out = w.shape[1]
    assert t % ROW_TILE == 0 and n_out % PROJ_COL_TILE == 0
    return pl.pallas_call(
        kernel,
        grid=(t // ROW_TILE,),
        in_specs=[pl.BlockSpec((ROW_TILE, d), lambda i: (i, 0)), _resident((1, d)),
                  _resident((d, n_out))] + extra_specs,
        out_specs=[pl.BlockSpec((ROW_TILE, n_out), lambda i: (i, 0)), extra_out_spec],
        out_shape=[jax.ShapeDtypeStruct((t, n_out), BF16), extra_out_shape],
        compiler_params=pltpu.CompilerParams(
            dimension_semantics=("parallel",), vmem_limit_bytes=VMEM_LIMIT_BYTES),
        name=name,
    )(x2d, gain.reshape(1, d), w.astype(BF16), *extra_in)


def _norm_proj_gate(x2d, gain, w, w_a, w_b, b_g):
    t, d = x2d.shape
    rank, dk = w_b.shape
    return _norm_proj_call(
        _norm_proj_gate_kernel, "norm_proj_gate", x2d, gain, w,
        [w_a.astype(BF16), w_b.astype(BF16), b_g.reshape(1, dk)],
        [_resident((d, rank)), _resident((rank, dk)), _resident((1, dk))],
        jax.ShapeDtypeStruct((t, dk), F32), pl.BlockSpec((ROW_TILE, dk), lambda i: (i, 0)))


def _norm_proj_vt(x2d, gain, w_qk, w_v, batch, seq):
    t, d = x2d.shape
    d_v = w_v.shape[1]
    tiles = seq // ROW_TILE
    return _norm_proj_call(
        _norm_proj_vt_kernel, "norm_proj_vt", x2d, gain, w_qk,
        [w_v.T.astype(BF16)], [_resident((d_v, d))],
        jax.ShapeDtypeStruct((batch, tiles, d_v, ROW_TILE), BF16),
        pl.BlockSpec((1, 1, d_v, ROW_TILE), lambda i: (i // tiles, i % tiles, 0, 0)))


def _gla_kernel(q_ref, k_ref, v_ref, r_ref, gk_ref, gn_ref, o_ref):
    c = GLA_CHUNK
    grp = GLA_GROUP
    n = grp // c
    seq, hk = q_ref.shape[1], q_ref.shape[2]
    hv = v_ref.shape[2]
    shift = c.bit_length() - 1
    row = lax.broadcasted_iota(jnp.int32, (grp, grp), 0)
    col = lax.broadcasted_iota(jnp.int32, (grp, grp), 1)
    mask = (col <= row) & ((row >> shift) == (col >> shift))
    tri = jnp.where(mask, 1.0, 0.0).astype(BF16)
    gain = gn_ref[...]
    q_scale = hk ** -0.5

    def group(gi, state_t):
        rows = pl.ds(pl.multiple_of(gi * grp, grp), grp)
        g = gk_ref[0, rows, :]
        g_hi = g.astype(BF16)
        g_lo = (g - g_hi.astype(F32)).astype(BF16)
        b = (_dot(tri, g_hi) + _dot(tri, g_lo)).reshape(n, c, hk)
        b_mid = b[:, c // 2:c // 2 + 1, :]
        b_last = b[:, c - 1:c, :]
        q = (q_ref[0, rows, :].astype(F32) * q_scale).reshape(n, c, hk)
        k = k_ref[0, rows, :].astype(F32).reshape(n, c, hk)
        v = v_ref[0, rows, :]
        q_mid = (q * jnp.exp(b - b_mid)).reshape(grp, hk).astype(BF16)
        k_mid = (k * jnp.exp(b_mid - b)).reshape(grp, hk).astype(BF16)
        attn = jnp.where(mask, _dot_nt(q_mid, k_mid), 0.0)
        o_intra = _dot(attn.astype(BF16), v)
        q_dec = (q * jnp.exp(b)).astype(BF16)
        k_end = (k * jnp.exp(b_last - b)).astype(BF16)
        decay = jnp.exp(b_last)
        outs = []
        for ci in range(n):
            rc = slice(ci * c, (ci + 1) * c)
            outs.append(o_intra[rc] + _dot_nt(q_dec[ci], state_t.astype(BF16)))
            state_t = state_t * decay[ci] + _dot_tn(v[rc], k_end[ci])
        o = _rmsnorm_rows(jnp.concatenate(outs, axis=0), gain, NORM_EPS)
        r = r_ref[0, rows, :].astype(F32)
        o_ref[0, rows, :] = (o * (r * jax.nn.sigmoid(r))).astype(o_ref.dtype)
        return state_t

    lax.fori_loop(0, seq // grp, group, jnp.zeros((hv, hk), F32))


def _gla_core(proj, gk, g_norm, batch, seq):
    dk = gk.shape[1]
    hk = dk // GLA_HEADS
    hv = g_norm.shape[0]
    dv = hv * GLA_HEADS
    assert proj.shape[1] == 2 * dk + 2 * dv and seq % GLA_GROUP == 0 and GLA_GROUP % GLA_CHUNK == 0
    proj3 = proj.reshape(batch, seq, proj.shape[1])
    gk3 = gk.reshape(batch, seq, dk)
    k_off, v_off, r_off = dk // hk, (2 * dk) // hv, (2 * dk + dv) // hv
    return pl.pallas_call(
        _gla_kernel,
        grid=(batch, GLA_HEADS),
        in_specs=[
            pl.BlockSpec((1, seq, hk), lambda b, h: (b, 0, h)),
            pl.BlockSpec((1, seq, hk), lambda b, h: (b, 0, k_off + h)),
            pl.BlockSpec((1, seq, hv), lambda b, h: (b, 0, v_off + h)),
            pl.BlockSpec((1, seq, hv), lambda b, h: (b, 0, r_off + h)),
            pl.BlockSpec((1, seq, hk), lambda b, h: (b, 0, h)),
            _resident((1, hv)),
        ],
        out_specs=pl.BlockSpec((1, seq, hv), lambda b, h: (b, 0, h)),
        out_shape=jax.ShapeDtypeStruct((batch, seq, dv), BF16),
        compiler_params=pltpu.CompilerParams(
            dimension_semantics=("parallel", "parallel"), vmem_limit_bytes=VMEM_LIMIT_BYTES),
        name="gla_core",
    )(proj3, proj3, proj3, proj3, gk3, g_norm.reshape(1, hv))


def _diff_attn_kernel(lq1_ref, lk1_ref, lq2_ref, lk2_ref, gn_ref, q_ref, k_ref, vt_ref, o_ref,
                      qq_ref, m_ref, acc_ref, *, lambda_init):
    tq = q_ref.shape[1]
    d2 = q_ref.shape[2]
    d = d2 // 2
    tk = vt_ref.shape[3]
    ts = ATTN_SUB_TILE
    assert tq == tk and tk % ts == 0
    qi = pl.program_id(2)

    lane = lax.broadcasted_iota(jnp.int32, (tq, d2), 1)
    q = q_ref[0].astype(F32) * (d ** -0.5 * math.log2(math.e))
    qq_ref[0:tq, :] = jnp.where(lane < d, q, 0.0).astype(BF16)
    qq_ref[tq:2 * tq, :] = jnp.where(lane >= d, q, 0.0).astype(BF16)
    m_ref[...] = jnp.full_like(m_ref, -jnp.inf)
    acc_ref[...] = jnp.zeros_like(acc_ref)
    ones = jnp.ones((ATTN_SUM_ROWS, ts), BF16)

    def scores(t0, off, masked):
        k = k_ref[0, pl.ds(pl.multiple_of(t0 * tk + off, ts), ts), :]
        st = _dot_nt(k, qq_ref[...])
        if masked:
            k_pos = off % tk + lax.broadcasted_iota(jnp.int32, (ts, 2 * tq), 0)
            q_pos = lax.broadcasted_iota(jnp.int32, (ts, 2 * tq), 1) & (tq - 1)
            st = jnp.where(k_pos <= q_pos, st, -jnp.inf)
        return st

    def update(st, t0, off):
        m_prev = m_ref[...]
        m_new = jnp.maximum(m_prev, jnp.max(st, axis=0, keepdims=True))
        p = jnp.exp2(st - m_new).astype(BF16)
        alpha = jnp.exp2(m_prev - m_new)
        vt = jnp.concatenate([vt_ref[0, t0 + off // tk, :, off % tk:off % tk + ts], ones], axis=0)
        acc_ref[...] = alpha * acc_ref[...] + _dot(vt, p)
        m_ref[...] = m_new

    def run(t0, subs):
        pending = {}
        for i in range(len(subs) + ATTN_LOOKAHEAD):
            if i < len(subs):
                pending[i] = scores(t0, *subs[i])
            if i >= ATTN_LOOKAHEAD:
                update(pending.pop(i - ATTN_LOOKAHEAD), t0, subs[i - ATTN_LOOKAHEAD][0])

    per_tile = tk // ts
    two_tiles = [(i * ts, False) for i in range(2 * per_tile)]
    diag_tile = [(i * ts, True) for i in range(per_tile)]

    def full_pair(jb, carry):
        run(2 * jb, two_tiles)
        return carry

    lax.fori_loop(0, qi // 2, full_pair, 0)

    @pl.when(qi % 2 == 0)
    def _():
        run(qi, diag_tile)

    @pl.when(qi % 2 == 1)
    def _():
        run(qi - 1, two_tiles[:per_tile] + [(tk + off, True) for off, _ in diag_tile])


    lam = (jnp.exp(jnp.sum(lq1_ref[...] * lk1_ref[...], keepdims=True))
           - jnp.exp(jnp.sum(lq2_ref[...] * lk2_ref[...], keepdims=True)) + lambda_init)
    o_all = acc_ref[0:d2, :] / acc_ref[d2:d2 + 1, :]
    o_t = o_all[:, 0:tq] - lam * o_all[:, tq:2 * tq]
    inv_rms = lax.rsqrt(jnp.mean(o_t * o_t, axis=0, keepdims=True) + SUBLN_EPS)
    o_t = o_t * inv_rms * (gn_ref[...] * (1.0 - lambda_init))
    o_ref[0] = o_t.T.astype(o_ref.dtype)


def _diff_attn(qk, v_t, lam_q1, lam_k1, lam_q2, lam_k2, g_norm, batch, seq, lambda_init):
    d_model = qk.shape[1] // 2
    d2 = d_model // DIFF_HEADS
    d = d2 // 2
    tq = ATTN_TILE
    tiles = seq // tq
    assert seq % tq == 0 and g_norm.shape[0] == d2 and v_t.shape == (batch, tiles, d_model, tq)
    qk3 = qk.reshape(batch, seq, 2 * d_model)
    return pl.pallas_call(
        functools.partial(_diff_attn_kernel, lambda_init=lambda_init),
        grid=(batch, DIFF_HEADS, tiles),
        in_specs=[_resident((1, d))] * 4 + [
            _resident((d2, 1)),
            pl.BlockSpec((1, tq, d2), lambda b, h, i: (b, i, h)),
            pl.BlockSpec((1, seq, d2), lambda b, h, i: (b, 0, DIFF_HEADS + h)),
            pl.BlockSpec((1, tiles, d2, tq), lambda b, h, i: (b, 0, h, 0)),
        ],
        out_specs=pl.BlockSpec((1, tq, d2), lambda b, h, i: (b, i, h)),
        out_shape=jax.ShapeDtypeStruct((batch, seq, d_model), BF16),
        scratch_shapes=[
            pltpu.VMEM((2 * tq, d2), BF16),
            pltpu.VMEM((1, 2 * tq), F32),
            pltpu.VMEM((d2 + ATTN_SUM_ROWS, 2 * tq), F32),
        ],
        compiler_params=pltpu.CompilerParams(
            dimension_semantics=("parallel", "parallel", "arbitrary"), vmem_limit_bytes=VMEM_LIMIT_BYTES),
        name="diff_attn",
    )(lam_q1.reshape(1, d), lam_k1.reshape(1, d), lam_q2.reshape(1, d), lam_k2.reshape(1, d),
      g_norm.reshape(d2, 1), qk3, qk3, v_t)


def _mix_out_ffn_kernel(x_ref, a_ref, wo_ref, gf_ref, wi_ref, w2_ref, gl_ref, o_ref, acc_ref, *, final_norm):
    d_ff = w2_ref.shape[0]
    x1 = x_ref[...] + _dot(a_ref[...], wo_ref[...])
    h = _rmsnorm_rows(x1, gf_ref[...], NORM_EPS).astype(BF16)
    acc_ref[...] = x1
    for f0 in range(0, d_ff, FFN_COL_TILE):
        gate = _dot(h, wi_ref[:, f0:f0 + FFN_COL_TILE])
        up = _dot(h, wi_ref[:, d_ff + f0:d_ff + f0 + FFN_COL_TILE])
        act = (gate * jax.nn.sigmoid(gate) * up).astype(BF16)
        acc_ref[...] += _dot(act, w2_ref[f0:f0 + FFN_COL_TILE, :])
    x2 = acc_ref[...]
    if final_norm:
        x2 = _rmsnorm_rows(x2, gl_ref[...], NORM_EPS)
    o_ref[...] = x2


def _mix_out_ffn(x2d, mix, w_out, g_ffn, w_in, w2, g_last, final_norm):
    t, d = x2d.shape
    d_ff = w2.shape[0]
    assert t % ROW_TILE == 0 and d_ff % FFN_COL_TILE == 0
    row_spec = pl.BlockSpec((ROW_TILE, d), lambda i: (i, 0))
    return pl.pallas_call(
        functools.partial(_mix_out_ffn_kernel, final_norm=final_norm),
        grid=(t // ROW_TILE,),
        in_specs=[row_spec, row_spec, _resident((d, d)), _resident((1, d)),
                  _resident((d, 2 * d_ff)), _resident((d_ff, d)), _resident((1, d))],
        out_specs=row_spec,
        out_shape=jax.ShapeDtypeStruct((t, d), F32),
        scratch_shapes=[pltpu.VMEM((ROW_TILE, d), F32)],
        compiler_params=pltpu.CompilerParams(
            dimension_semantics=("parallel",), vmem_limit_bytes=VMEM_LIMIT_BYTES),
        name="mix_out_ffn",
    )(x2d, mix, w_out.astype(BF16), g_ffn.reshape(1, d), w_in.astype(BF16), w2.astype(BF16),
      g_last.reshape(1, d))


def kernel(x, gla_w_in, gla_w_gate_a, gla_w_gate_b, gla_b_gate, gla_norm, gla_w_out, diff_w_in, diff_lam_q1, diff_lam_k1, diff_lam_q2, diff_lam_k2, diff_norm, diff_w_out, norm_mixer, norm_ffn, ffn_w_in, ffn_w_out, norm_final):
    batch, seq, d = x.shape
    x2d = x.reshape(batch * seq, d)

    proj, gk = _norm_proj_gate(x2d, norm_mixer[0], gla_w_in[0],
                               gla_w_gate_a[0], gla_w_gate_b[0], gla_b_gate[0])
    mix = _gla_core(proj, gk, gla_norm[0], batch, seq).reshape(batch * seq, -1)
    x2d = _mix_out_ffn(x2d, mix, gla_w_out[0], norm_ffn[0], ffn_w_in[0], ffn_w_out[0], norm_final,
                       final_norm=False)

    lambda_init = 0.8 - 0.6 * math.exp(-0.3 * 1)
    qk, v_t = _norm_proj_vt(x2d, norm_mixer[1], diff_w_in[0][:, :2 * d], diff_w_in[0][:, 2 * d:],
                            batch, seq)
    mix = _diff_attn(qk, v_t, diff_lam_q1[0], diff_lam_k1[0], diff_lam_q2[0], diff_lam_k2[0],
                     diff_norm[0], batch, seq, lambda_init).reshape(batch * seq, -1)
    x2d = _mix_out_ffn(x2d, mix, diff_w_out[0], norm_ffn[1], ffn_w_in[1], ffn_w_out[1], norm_final,
                       final_norm=True)
    return x2d.reshape(batch, seq, d)
```

```python
import functools
import math

import jax
import jax.numpy as jnp
from jax import lax
from jax.experimental import pallas as pl
from jax.experimental.pallas import tpu as pltpu

F32 = jnp.float32
BF16 = jnp.bfloat16

NORM_EPS = 1e-6
SUBLN_EPS = 1e-5

GLA_HEADS = 4
GLA_GATE_RANK = 16
GLA_TAU = 16.0
GLA_CHUNK = 64

DIFF_HEADS = 8

V7X_VMEM_BYTES = 64 * 1024 * 1024
VMEM_LIMIT_BYTES = V7X_VMEM_BYTES - 8 * 1024 * 1024

ROW_TILE = 512
PROJ_COL_TILE = 1024
FFN_COL_TILE = 256
GLA_GROUP = 256
ATTN_TILE = ROW_TILE
ATTN_SUB_TILE = 256
ATTN_SUM_ROWS = 16
ATTN_LOOKAHEAD = 2


def _resident(shape):
    zeros = (0,) * len(shape)
    return pl.BlockSpec(shape, lambda *_: zeros, pipeline_mode=pl.Buffered(1))


def _rmsnorm_rows(x, gain, eps):
    return x * lax.rsqrt(jnp.mean(x * x, axis=-1, keepdims=True) + eps) * gain


def _dot(a, b):
    return jnp.dot(a, b, preferred_element_type=F32)


def _dot_nt(a, b):
    return lax.dot_general(a, b, (((1,), (1,)), ((), ())), preferred_element_type=F32)


def _dot_tn(a, b):
    return lax.dot_general(a, b, (((0,), (0,)), ((), ())), preferred_element_type=F32)


def _norm_proj_body(x_ref, g_ref, w_ref, o_ref):
    h = _rmsnorm_rows(x_ref[...], g_ref[...], NORM_EPS).astype(BF16)
    n_out = o_ref.shape[1]
    for c0 in range(0, n_out, PROJ_COL_TILE):
        o_ref[:, c0:c0 + PROJ_COL_TILE] = _dot(h, w_ref[:, c0:c0 + PROJ_COL_TILE]).astype(o_ref.dtype)
    return h


def _norm_proj_gate_kernel(x_ref, g_ref, w_ref, wa_ref, wb_ref, bg_ref, o_ref, gk_ref):
    h = _norm_proj_body(x_ref, g_ref, w_ref, o_ref)
    low = _dot(h, wa_ref[...])
    logits = _dot(low.astype(BF16), wb_ref[...]) + bg_ref[...]
    log_sig = jnp.minimum(logits, 0.0) - jnp.log1p(jnp.exp(-jnp.abs(logits)))
    gk_ref[...] = log_sig / GLA_TAU


def _norm_proj_vt_kernel(x_ref, g_ref, w_ref, wvt_ref, o_ref, vt_ref):
    h = _norm_proj_body(x_ref, g_ref, w_ref, o_ref)
    vt_ref[0, 0] = _dot_nt(wvt_ref[...], h).astype(vt_ref.dtype)


def _norm_proj_call(kernel, name, x2d, gain, w, extra_in, extra_specs, extra_out_shape, extra_out_spec):
    t, d = x2d.shape
    n_out = w.shape[1]
    assert t % ROW_TILE == 0 and n_out % PROJ_COL_TILE == 0
    return pl.pallas_call(
        kernel,
        grid=(t // ROW_TILE,),
        in_specs=[pl.BlockSpec((ROW_TILE, d), lambda i: (i, 0)), _resident((1, d)),
                  _resident((d, n_out))] + extra_specs,
        out_specs=[pl.BlockSpec((ROW_TILE, n_out), lambda i: (i, 0)), extra_out_spec],
        out_shape=[jax.ShapeDtypeStruct((t, n_out), BF16), extra_out_shape],
        compiler_params=pltpu.CompilerParams(
            dimension_semantics=("parallel",), vmem_limit_bytes=VMEM_LIMIT_BYTES),
        name=name,
    )(x2d, gain.reshape(1, d), w.astype(BF16), *extra_in)


def _norm_proj_gate(x2d, gain, w, w_a, w_b, b_g):
    t, d = x2d.shape
    rank, dk = w_b.shape
    return _norm_proj_call(
        _norm_proj_gate_kernel, "norm_proj_gate", x2d, gain, w,
        [w_a.astype(BF16), w_b.astype(BF16), b_g.reshape(1, dk)],
        [_resident((d, rank)), _resident((rank, dk)), _resident((1, dk))],
        jax.ShapeDtypeStruct((t, dk), F32), pl.BlockSpec((ROW_TILE, dk), lambda i: (i, 0)))


def _norm_proj_vt(x2d, gain, w_qk, w_v, batch, seq):
    t, d = x2d.shape
    d_v = w_v.shape[1]
    tiles = seq // ROW_TILE
    return _norm_proj_call(
        _norm_proj_vt_kernel, "norm_proj_vt", x2d, gain, w_qk,
        [w_v.T.astype(BF16)], [_resident((d_v, d))],
        jax.ShapeDtypeStruct((batch, tiles, d_v, ROW_TILE), BF16),
        pl.BlockSpec((1, 1, d_v, ROW_TILE), lambda i: (i // tiles, i % tiles, 0, 0)))


def _gla_kernel(q_ref, k_ref, v_ref, r_ref, gk_ref, gn_ref, o_ref):
    c = GLA_CHUNK
    grp = GLA_GROUP
    n = grp // c
    seq, hk = q_ref.shape[1], q_ref.shape[2]
    hv = v_ref.shape[2]
    shift = c.bit_length() - 1
    row = lax.broadcasted_iota(jnp.int32, (grp, grp), 0)
    col = lax.broadcasted_iota(jnp.int32, (grp, grp), 1)
    mask = (col <= row) & ((row >> shift) == (col >> shift))
    tri = jnp.where(mask, 1.0, 0.0).astype(BF16)
    gain = gn_ref[...]
    q_scale = hk ** -0.5

    def group(gi, state_t):
        rows = pl.ds(pl.multiple_of(gi * grp, grp), grp)
        g = gk_ref[0, rows, :]
        g_hi = g.astype(BF16)
        g_lo = (g - g_hi.astype(F32)).astype(BF16)
        b = (_dot(tri, g_hi) + _dot(tri, g_lo)).reshape(n, c, hk)
        b_mid = b[:, c // 2:c // 2 + 1, :]
        b_last = b[:, c - 1:c, :]
        q = (q_ref[0, rows, :].astype(F32) * q_scale).reshape(n, c, hk)
        k = k_ref[0, rows, :].astype(F32).reshape(n, c, hk)
        v = v_ref[0, rows, :]
        q_mid = (q * jnp.exp(b - b_mid)).reshape(grp, hk).astype(BF16)
        k_mid = (k * jnp.exp(b_mid - b)).reshape(grp, hk).astype(BF16)
        attn = jnp.where(mask, _dot_nt(q_mid, k_mid), 0.0)
        o_intra = _dot(attn.astype(BF16), v)
        q_dec = (q * jnp.exp(b)).astype(BF16)
        k_end = (k * jnp.exp(b_last - b)).astype(BF16)
        decay = jnp.exp(b_last)
        outs = []
        for ci in range(n):
            rc = slice(ci * c, (ci + 1) * c)
            outs.append(o_intra[rc] + _dot_nt(q_dec[ci], state_t.astype(BF16)))
            state_t = state_t * decay[ci] + _dot_tn(v[rc], k_end[ci])
        o = _rmsnorm_rows(jnp.concatenate(outs, axis=0), gain, NORM_EPS)
        r = r_ref[0, rows, :].astype(F32)
        o_ref[0, rows, :] = (o * (r * jax.nn.sigmoid(r))).astype(o_ref.dtype)
        return state_t

    lax.fori_loop(0, seq // grp, group, jnp.zeros((hv, hk), F32))


def _gla_core(proj, gk, g_norm, batch, seq):
    dk = gk.shape[1]
    hk = dk // GLA_HEADS
    hv = g_norm.shape[0]
    dv = hv * GLA_HEADS
    assert proj.shape[1] == 2 * dk + 2 * dv and seq % GLA_GROUP == 0 and GLA_GROUP % GLA_CHUNK == 0
    proj3 = proj.reshape(batch, seq, proj.shape[1])
    gk3 = gk.reshape(batch, seq, dk)
    k_off, v_off, r_off = dk // hk, (2 * dk) // hv, (2 * dk + dv) // hv
    return pl.pallas_call(
        _gla_kernel,
        grid=(batch, GLA_HEADS),
        in_specs=[
            pl.BlockSpec((1, seq, hk), lambda b, h: (b, 0, h)),
            pl.BlockSpec((1, seq, hk), lambda b, h: (b, 0, k_off + h)),
            pl.BlockSpec((1, seq, hv), lambda b, h: (b, 0, v_off + h)),
            pl.BlockSpec((1, seq, hv), lambda b, h: (b, 0, r_off + h)),
            pl.BlockSpec((1, seq, hk), lambda b, h: (b, 0, h)),
            _resident((1, hv)),
        ],
        out_specs=pl.BlockSpec((1, seq, hv), lambda b, h: (b, 0, h)),
        out_shape=jax.ShapeDtypeStruct((batch, seq, dv), BF16),
        compiler_params=pltpu.CompilerParams(
            dimension_semantics=("parallel", "parallel"), vmem_limit_bytes=VMEM_LIMIT_BYTES),
        name="gla_core",
    )(proj3, proj3, proj3, proj3, gk3, g_norm.reshape(1, hv))


def _diff_attn_kernel(lq1_ref, lk1_ref, lq2_ref, lk2_ref, gn_ref, q_ref, k_ref, vt_ref, o_ref,
                      qq_ref, m_ref, acc_ref, *, lambda_init):
    tiles, d2, tq = vt_ref.shape[1], vt_ref.shape[2], vt_ref.shape[3]
    d = d2 // 2
    ts = ATTN_SUB_TILE
    assert tq % ts == 0

    lane = lax.broadcasted_iota(jnp.int32, (tq, d2), 1)
    for t in range(tiles):
        q = q_ref[0, t * tq:(t + 1) * tq, :].astype(F32) * (d ** -0.5 * math.log2(math.e))
        qq_ref[t, 0:tq, :] = jnp.where(lane < d, q, 0.0).astype(BF16)
        qq_ref[t, tq:2 * tq, :] = jnp.where(lane >= d, q, 0.0).astype(BF16)
    ones = jnp.ones((ATTN_SUM_ROWS, ts), BF16)
    lam = (jnp.exp(jnp.sum(lq1_ref[...] * lk1_ref[...], keepdims=True))
           - jnp.exp(jnp.sum(lq2_ref[...] * lk2_ref[...], keepdims=True)) + lambda_init)
    out_gain = gn_ref[...] * (1.0 - lambda_init)

    def scores(t, kt, off):
        k = k_ref[0, kt * tq + off:kt * tq + off + ts, :]
        st = _dot_nt(k, qq_ref[t])
        if kt == t:
            k_pos = off + lax.broadcasted_iota(jnp.int32, (ts, 2 * tq), 0)
            q_pos = lax.broadcasted_iota(jnp.int32, (ts, 2 * tq), 1) & (tq - 1)
            st = jnp.where(k_pos <= q_pos, st, -jnp.inf)
        return st

    def update(st, kt, off):
        m_prev = m_ref[...]
        m_new = jnp.maximum(m_prev, jnp.max(st, axis=0, keepdims=True))
        p = jnp.exp2(st - m_new).astype(BF16)
        alpha = jnp.exp2(m_prev - m_new)
        vt = jnp.concatenate([vt_ref[0, kt, :, off:off + ts], ones], axis=0)
        acc_ref[...] = alpha * acc_ref[...] + _dot(vt, p)
        m_ref[...] = m_new

    def finalize(t):
        o_all = acc_ref[0:d2, :] / acc_ref[d2:d2 + 1, :]
        o_t = o_all[:, 0:tq] - lam * o_all[:, tq:2 * tq]
        inv_rms = lax.rsqrt(jnp.mean(o_t * o_t, axis=0, keepdims=True) + SUBLN_EPS)
        o_ref[0, t * tq:(t + 1) * tq, :] = (o_t * inv_rms * out_gain).T.astype(o_ref.dtype)

    subs = [(t, kt, off) for t in range(tiles) for kt in range(t + 1) for off in range(0, tq, ts)]
    pending = {}
    for i in range(len(subs) + ATTN_LOOKAHEAD):
        if i < len(subs):
            pending[i] = scores(*subs[i])
        if i >= ATTN_LOOKAHEAD:
            t, kt, off = subs[i - ATTN_LOOKAHEAD]
            if kt == 0 and off == 0:
                m_ref[...] = jnp.full_like(m_ref, -jnp.inf)
                acc_ref[...] = jnp.zeros_like(acc_ref)
            update(pending.pop(i - ATTN_LOOKAHEAD), kt, off)
            if kt == t and off == tq - ts:
                finalize(t)


def _diff_attn(qk, v_t, lam_q1, lam_k1, lam_q2, lam_k2, g_norm, batch, seq, lambda_init):
    d_model = qk.shape[1] // 2
    d2 = d_model // DIFF_HEADS
    d = d2 // 2
    tq = ATTN_TILE
    tiles = seq // tq
    assert seq % tq == 0 and g_norm.shape[0] == d2 and v_t.shape == (batch, tiles, d_model, tq)
    qk3 = qk.reshape(batch, seq, 2 * d_model)
    return pl.pallas_call(
        functools.partial(_diff_attn_kernel, lambda_init=lambda_init),
        grid=(batch, DIFF_HEADS),
        in_specs=[_resident((1, d))] * 4 + [
            _resident((d2, 1)),
            pl.BlockSpec((1, seq, d2), lambda b, h: (b, 0, h)),
            pl.BlockSpec((1, seq, d2), lambda b, h: (b, 0, DIFF_HEADS + h)),
            pl.BlockSpec((1, tiles, d2, tq), lambda b, h: (b, 0, h, 0)),
        ],
        out_specs=pl.BlockSpec((1, seq, d2), lambda b, h: (b, 0, h)),
        out_shape=jax.ShapeDtypeStruct((batch, seq, d_model), BF16),
        scratch_shapes=[
            pltpu.VMEM((tiles, 2 * tq, d2), BF16),
            pltpu.VMEM((1, 2 * tq), F32),
            pltpu.VMEM((d2 + ATTN_SUM_ROWS, 2 * tq), F32),
        ],
        compiler_params=pltpu.CompilerParams(
            dimension_semantics=("parallel", "parallel"), vmem_limit_bytes=VMEM_LIMIT_BYTES),
        name="diff_attn",
    )(lam_q1.reshape(1, d), lam_k1.reshape(1, d), lam_q2.reshape(1, d), lam_k2.reshape(1, d),
      g_norm.reshape(d2, 1), qk3, qk3, v_t)


def _mix_out_ffn_kernel(x_ref, a_ref, wo_ref, gf_ref, wi_ref, w2_ref, gl_ref, o_ref, acc_ref, *, final_norm):
    d_ff = w2_ref.shape[0]
    x1 = x_ref[...] + _dot(a_ref[...], wo_ref[...])
    h = _rmsnorm_rows(x1, gf_ref[...], NORM_EPS).astype(BF16)
    acc_ref[...] = x1
    for f0 in range(0, d_ff, FFN_COL_TILE):
        gate = _dot(h, wi_ref[:, f0:f0 + FFN_COL_TILE])
        up = _dot(h, wi_ref[:, d_ff + f0:d_ff + f0 + FFN_COL_TILE])
        act = (gate * jax.nn.sigmoid(gate) * up).astype(BF16)
        acc_ref[...] += _dot(act, w2_ref[f0:f0 + FFN_COL_TILE, :])
    x2 = acc_ref[...]
    if final_norm:
        x2 = _rmsnorm_rows(x2, gl_ref[...], NORM_EPS)
    o_ref[...] = x2


def _mix_out_ffn(x2d, mix, w_out, g_ffn, w_in, w2, g_last, final_norm):
    t, d = x2d.shape
    d_ff = w2.shape[0]
    assert t % ROW_TILE == 0 and d_ff % FFN_COL_TILE == 0
    row_spec = pl.BlockSpec((ROW_TILE, d), lambda i: (i, 0))
    return pl.pallas_call(
        functools.partial(_mix_out_ffn_kernel, final_norm=final_norm),
        grid=(t // ROW_TILE,),
        in_specs=[row_spec, row_spec, _resident((d, d)), _resident((1, d)),
                  _resident((d, 2 * d_ff)), _resident((d_ff, d)), _resident((1, d))],
        out_specs=row_spec,
        out_shape=jax.ShapeDtypeStruct((t, d), F32),
        scratch_shapes=[pltpu.VMEM((ROW_TILE, d), F32)],
        compiler_params=pltpu.CompilerParams(
            dimension_semantics=("parallel",), vmem_limit_bytes=VMEM_LIMIT_BYTES),
        name="mix_out_ffn",
    )(x2d, mix, w_out.astype(BF16), g_ffn.reshape(1, d), w_in.astype(BF16), w2.astype(BF16),
      g_last.reshape(1, d))


def kernel(x, gla_w_in, gla_w_gate_a, gla_w_gate_b, gla_b_gate, gla_norm, gla_w_out, diff_w_in, diff_lam_q1, diff_lam_k1, diff_lam_q2, diff_lam_k2, diff_norm, diff_w_out, norm_mixer, norm_ffn, ffn_w_in, ffn_w_out, norm_final):
    batch, seq, d = x.shape
    x2d = x.reshape(batch * seq, d)

    proj, gk = _norm_proj_gate(x2d, norm_mixer[0], gla_w_in[0],
                               gla_w_gate_a[0], gla_w_gate_b[0], gla_b_gate[0])
    mix = _gla_core(proj, gk, gla_norm[0], batch, seq).reshape(batch * seq, -1)
    x2d = _mix_out_ffn(x2d, mix, gla_w_out[0], norm_ffn[0], ffn_w_in[0], ffn_w_out[0], norm_final,
                       final_norm=False)

    lambda_init = 0.8 - 0.6 * math.exp(-0.3 * 1)
    qk, v_t = _norm_proj_vt(x2d, norm_mixer[1], diff_w_in[0][:, :2 * d], diff_w_in[0][:, 2 * d:],
                            batch, seq)
    mix = _diff_attn(qk, v_t, diff_lam_q1[0], diff_lam_k1[0], diff_lam_q2[0], diff_lam_k2[0],
                     diff_norm[0], batch, seq, lambda_init).reshape(batch * seq, -1)
    x2d = _mix_out_ffn(x2d, mix, diff_w_out[0], norm_ffn[1], ffn_w_in[1], ffn_w_out[1], norm_final,
                       final_norm=True)
    return x2d.reshape(batch, seq, d)
```

```python
import functools
import math

import jax
import jax.numpy as jnp
from jax import lax
from jax.experimental import pallas as pl
from jax.experimental.pallas import tpu as pltpu

F32 = jnp.float32
BF16 = jnp.bfloat16

NORM_EPS = 1e-6
SUBLN_EPS = 1e-5

GLA_HEADS = 4
GLA_GATE_RANK = 16
GLA_TAU = 16.0
GLA_CHUNK = 64

DIFF_HEADS = 8

V7X_VMEM_BYTES = 64 * 1024 * 1024
VMEM_LIMIT_BYTES = V7X_VMEM_BYTES - 8 * 1024 * 1024

ROW_TILE = 512
PROJ_COL_TILE = 1024
FFN_COL_TILE = 256
GLA_GROUP = 256
ATTN_TILE = ROW_TILE
ATTN_SUB_TILE = 256
ATTN_SUM_ROWS = 16
ATTN_LOOKAHEAD = 2


def _resident(shape):
    zeros = (0,) * len(shape)
    return pl.BlockSpec(shape, lambda *_: zeros, pipeline_mode=pl.Buffered(1))


def _rmsnorm_rows(x, gain, eps):
    return x * lax.rsqrt(jnp.mean(x * x, axis=-1, keepdims=True) + eps) * gain


def _dot(a, b):
    return jnp.dot(a, b, preferred_element_type=F32)


def _dot_nt(a, b):
    return lax.dot_general(a, b, (((1,), (1,)), ((), ())), preferred_element_type=F32)


def _dot_tn(a, b):
    return lax.dot_general(a, b, (((0,), (0,)), ((), ())), preferred_element_type=F32)


def _norm_proj_body(x_ref, g_ref, w_ref, o_ref):
    h = _rmsnorm_rows(x_ref[...], g_ref[...], NORM_EPS).astype(BF16)
    n_out = o_ref.shape[1]
    for c0 in range(0, n_out, PROJ_COL_TILE):
        o_ref[:, c0:c0 + PROJ_COL_TILE] = _dot(h, w_ref[:, c0:c0 + PROJ_COL_TILE]).astype(o_ref.dtype)
    return h


def _norm_proj_gate_kernel(x_ref, g_ref, w_ref, wa_ref, wb_ref, bg_ref, o_ref, gk_ref):
    h = _norm_proj_body(x_ref, g_ref, w_ref, o_ref)
    low = _dot(h, wa_ref[...])
    logits = _dot(low.astype(BF16), wb_ref[...]) + bg_ref[...]
    log_sig = jnp.minimum(logits, 0.0) - jnp.log1p(jnp.exp(-jnp.abs(logits)))
    gk_ref[...] = log_sig / GLA_TAU


def _norm_proj_vt_kernel(x_ref, g_ref, w_ref, wvt_ref, o_ref, vt_ref):
    h = _norm_proj_body(x_ref, g_ref, w_ref, o_ref)
    vt_ref[0, 0] = _dot_nt(wvt_ref[...], h).astype(vt_ref.dtype)


def _norm_proj_call(kernel, name, x2d, gain, w, extra_in, extra_specs, extra_out_shape, extra_out_spec):
    t, d = x2d.shape
    n_out = w.shape[1]
    assert t % ROW_TILE == 0 and n_out % PROJ_COL_TILE == 0
    return pl.pallas_call(
        kernel,
        grid=(t // ROW_TILE,),
        in_specs=[pl.BlockSpec((ROW_TILE, d), lambda i: (i, 0)), _resident((1, d)),
                  _resident((d, n_out))] + extra_specs,
        out_specs=[pl.BlockSpec((ROW_TILE, n_out), lambda i: (i, 0)), extra_out_spec],
        out_shape=[jax.ShapeDtypeStruct((t, n_out), BF16), extra_out_shape],
        compiler_params=pltpu.CompilerParams(
            dimension_semantics=("parallel",), vmem_limit_bytes=VMEM_LIMIT_BYTES),
        name=name,
    )(x2d, gain.reshape(1, d), w.astype(BF16), *extra_in)


def _norm_proj_gate(x2d, gain, w, w_a, w_b, b_g):
    t, d = x2d.shape
    rank, dk = w_b.shape
    return _norm_proj_call(
        _norm_proj_gate_kernel, "norm_proj_gate", x2d, gain, w,
        [w_a.astype(BF16), w_b.astype(BF16), b_g.reshape(1, dk)],
        [_resident((d, rank)), _resident((rank, dk)), _resident((1, dk))],
        jax.ShapeDtypeStruct((t, dk), F32), pl.BlockSpec((ROW_TILE, dk), lambda i: (i, 0)))


def _norm_proj_vt(x2d, gain, w_qk, w_v, batch, seq):
    t, d = x2d.shape
    d_v = w_v.shape[1]
    tiles = seq // ROW_TILE
    return _norm_proj_call(
        _norm_proj_vt_kernel, "norm_proj_vt", x2d, gain, w_qk,
        [w_v.T.astype(BF16)], [_resident((d_v, d))],
        jax.ShapeDtypeStruct((batch, tiles, d_v, ROW_TILE), BF16),
        pl.BlockSpec((1, 1, d_v, ROW_TILE), lambda i: (i // tiles, i % tiles, 0, 0)))


def _gla_kernel(q_ref, k_ref, v_ref, r_ref, gk_ref, gn_ref, o_ref):
    c = GLA_CHUNK
    grp = GLA_GROUP
    n = grp // c
    seq, hk = q_ref.shape[1], q_ref.shape[2]
    hv = v_ref.shape[2]
    shift = c.bit_length() - 1
    row = lax.broadcasted_iota(jnp.int32, (grp, grp), 0)
    col = lax.broadcasted_iota(jnp.int32, (grp, grp), 1)
    mask = (col <= row) & ((row >> shift) == (col >> shift))
    tri = jnp.where(mask, 1.0, 0.0).astype(BF16)
    gain = gn_ref[...]
    q_scale = hk ** -0.5

    def rows_of(gi):
        return slice(gi * grp, (gi + 1) * grp)

    def cum_decay(gi):
        g = gk_ref[0, rows_of(gi), :]
        g_hi = g.astype(BF16)
        g_lo = (g - g_hi.astype(F32)).astype(BF16)
        return _dot(tri, g_hi) + _dot(tri, g_lo)

    def scaled_operands(gi, b):
        b = b.reshape(n, c, hk)
        b_mid = b[:, c // 2:c // 2 + 1, :]
        b_last = b[:, c - 1:c, :]
        q = (q_ref[0, rows_of(gi), :].astype(F32) * q_scale).reshape(n, c, hk)
        k = k_ref[0, rows_of(gi), :].astype(F32).reshape(n, c, hk)
        q_mid = (q * jnp.exp(b - b_mid)).reshape(grp, hk).astype(BF16)
        k_mid = (k * jnp.exp(b_mid - b)).reshape(grp, hk).astype(BF16)
        q_dec = (q * jnp.exp(b)).astype(BF16)
        k_end = (k * jnp.exp(b_last - b)).astype(BF16)
        return q_mid, k_mid, q_dec, k_end, jnp.exp(b_last)

    def state_part(gi, o_intra, q_dec, decay, incs, state_t):
        outs = []
        for ci in range(n):
            outs.append(o_intra[ci * c:(ci + 1) * c] + _dot_nt(q_dec[ci], state_t.astype(BF16)))
            state_t = state_t * decay[ci] + incs[ci]
        o = _rmsnorm_rows(jnp.concatenate(outs, axis=0), gain, NORM_EPS)
        r = r_ref[0, rows_of(gi), :].astype(F32)
        o_ref[0, rows_of(gi), :] = (o * (r * jax.nn.sigmoid(r))).astype(o_ref.dtype)
        return state_t

    groups = seq // grp
    state_t = jnp.zeros((hv, hk), F32)
    cum, ops = {}, {}
    for i in range(-2, groups):
        if i >= 0:
            q_mid, k_mid, q_dec, k_end, decay = ops.pop(i)
            v = v_ref[0, rows_of(i), :]
            attn = jnp.where(mask, _dot_nt(q_mid, k_mid), 0.0).astype(BF16)
            incs = [_dot_tn(v[ci * c:(ci + 1) * c], k_end[ci]) for ci in range(n)]
        if i + 2 < groups:
            cum[i + 2] = cum_decay(i + 2)
        if i >= 0:
            o_intra = _dot(attn, v)
        if 0 <= i + 1 < groups:
            ops[i + 1] = scaled_operands(i + 1, cum.pop(i + 1))
        if i >= 0:
            state_t = state_part(i, o_intra, q_dec, decay, incs, state_t)


def _gla_core(proj, gk, g_norm, batch, seq):
    dk = gk.shape[1]
    hk = dk // GLA_HEADS
    hv = g_norm.shape[0]
    dv = hv * GLA_HEADS
    assert proj.shape[1] == 2 * dk + 2 * dv and seq % GLA_GROUP == 0 and GLA_GROUP % GLA_CHUNK == 0
    proj3 = proj.reshape(batch, seq, proj.shape[1])
    gk3 = gk.reshape(batch, seq, dk)
    k_off, v_off, r_off = dk // hk, (2 * dk) // hv, (2 * dk + dv) // hv
    return pl.pallas_call(
        _gla_kernel,
        grid=(batch, GLA_HEADS),
        in_specs=[
            pl.BlockSpec((1, seq, hk), lambda b, h: (b, 0, h)),
            pl.BlockSpec((1, seq, hk), lambda b, h: (b, 0, k_off + h)),
            pl.BlockSpec((1, seq, hv), lambda b, h: (b, 0, v_off + h)),
            pl.BlockSpec((1, seq, hv), lambda b, h: (b, 0, r_off + h)),
            pl.BlockSpec((1, seq, hk), lambda b, h: (b, 0, h)),
            _resident((1, hv)),
        ],
        out_specs=pl.BlockSpec((1, seq, hv), lambda b, h: (b, 0, h)),
        out_shape=jax.ShapeDtypeStruct((batch, seq, dv), BF16),
        compiler_params=pltpu.CompilerParams(
            dimension_semantics=("parallel", "parallel"), vmem_limit_bytes=VMEM_LIMIT_BYTES),
        name="gla_core",
    )(proj3, proj3, proj3, proj3, gk3, g_norm.reshape(1, hv))


def _diff_attn_kernel(lq1_ref, lk1_ref, lq2_ref, lk2_ref, gn_ref, q_ref, k_ref, vt_ref, o_ref,
                      qq_ref, m_ref, acc_ref, *, lambda_init):
    tiles, d2, tq = vt_ref.shape[1], vt_ref.shape[2], vt_ref.shape[3]
    d = d2 // 2
    ts = ATTN_SUB_TILE
    assert tq % ts == 0

    lane = lax.broadcasted_iota(jnp.int32, (tq, d2), 1)
    for t in range(tiles):
        q = q_ref[0, t * tq:(t + 1) * tq, :].astype(F32) * (d ** -0.5 * math.log2(math.e))
        qq_ref[t, 0:tq, :] = jnp.where(lane < d, q, 0.0).astype(BF16)
        qq_ref[t, tq:2 * tq, :] = jnp.where(lane >= d, q, 0.0).astype(BF16)
    ones = jnp.ones((ATTN_SUM_ROWS, ts), BF16)
    lam = (jnp.exp(jnp.sum(lq1_ref[...] * lk1_ref[...], keepdims=True))
           - jnp.exp(jnp.sum(lq2_ref[...] * lk2_ref[...], keepdims=True)) + lambda_init)
    out_gain = gn_ref[...] * (1.0 - lambda_init)

    def scores(t, kt, off):
        k = k_ref[0, kt * tq + off:kt * tq + off + ts, :]
        st = _dot_nt(k, qq_ref[t])
        if kt == t:
            k_pos = off + lax.broadcasted_iota(jnp.int32, (ts, 2 * tq), 0)
            q_pos = lax.broadcasted_iota(jnp.int32, (ts, 2 * tq), 1) & (tq - 1)
            st = jnp.where(k_pos <= q_pos, st, -jnp.inf)
        return st

    def update(st, kt, off):
        m_prev = m_ref[...]
        m_new = jnp.maximum(m_prev, jnp.max(st, axis=0, keepdims=True))
        p = jnp.exp2(st - m_new).astype(BF16)
        alpha = jnp.exp2(m_prev - m_new)
        vt = jnp.concatenate([vt_ref[0, kt, :, off:off + ts], ones], axis=0)
        acc_ref[...] = alpha * acc_ref[...] + _dot(vt, p)
        m_ref[...] = m_new

    def finalize(t):
        o_all = acc_ref[0:d2, :] / acc_ref[d2:d2 + 1, :]
        o_t = o_all[:, 0:tq] - lam * o_all[:, tq:2 * tq]
        inv_rms = lax.rsqrt(jnp.mean(o_t * o_t, axis=0, keepdims=True) + SUBLN_EPS)
        o_ref[0, t * tq:(t + 1) * tq, :] = (o_t * inv_rms * out_gain).T.astype(o_ref.dtype)

    subs = [(t, kt, off) for t in range(tiles) for kt in range(t + 1) for off in range(0, tq, ts)]
    pending = {}
    for i in range(len(subs) + ATTN_LOOKAHEAD):
        if i < len(subs):
            pending[i] = scores(*subs[i])
        if i >= ATTN_LOOKAHEAD:
            t, kt, off = subs[i - ATTN_LOOKAHEAD]
            if kt == 0 and off == 0:
                m_ref[...] = jnp.full_like(m_ref, -jnp.inf)
                acc_ref[...] = jnp.zeros_like(acc_ref)
            update(pending.pop(i - ATTN_LOOKAHEAD), kt, off)
            if kt == t and off == tq - ts:
                finalize(t)


def _diff_attn(qk, v_t, lam_q1, lam_k1, lam_q2, lam_k2, g_norm, batch, seq, lambda_init):
    d_model = qk.shape[1] // 2
    d2 = d_model // DIFF_HEADS
    d = d2 // 2
    tq = ATTN_TILE
    tiles = seq // tq
    assert seq % tq == 0 and g_norm.shape[0] == d2 and v_t.shape == (batch, tiles, d_model, tq)
    qk3 = qk.reshape(batch, seq, 2 * d_model)
    return pl.pallas_call(
        functools.partial(_diff_attn_kernel, lambda_init=lambda_init),
        grid=(batch, DIFF_HEADS),
        in_specs=[_resident((1, d))] * 4 + [
            _resident((d2, 1)),
            pl.BlockSpec((1, seq, d2), lambda b, h: (b, 0, h)),
            pl.BlockSpec((1, seq, d2), lambda b, h: (b, 0, DIFF_HEADS + h)),
            pl.BlockSpec((1, tiles, d2, tq), lambda b, h: (b, 0, h, 0)),
        ],
        out_specs=pl.BlockSpec((1, seq, d2), lambda b, h: (b, 0, h)),
        out_shape=jax.ShapeDtypeStruct((batch, seq, d_model), BF16),
        scratch_shapes=[
            pltpu.VMEM((tiles, 2 * tq, d2), BF16),
            pltpu.VMEM((1, 2 * tq), F32),
            pltpu.VMEM((d2 + ATTN_SUM_ROWS, 2 * tq), F32),
        ],
        compiler_params=pltpu.CompilerParams(
            dimension_semantics=("parallel", "parallel"), vmem_limit_bytes=VMEM_LIMIT_BYTES),
        name="diff_attn",
    )(lam_q1.reshape(1, d), lam_k1.reshape(1, d), lam_q2.reshape(1, d), lam_k2.reshape(1, d),
      g_norm.reshape(d2, 1), qk3, qk3, v_t)


def _mix_out_ffn_kernel(x_ref, a_ref, wo_ref, gf_ref, wi_ref, w2_ref, gl_ref, o_ref, acc_ref, *, final_norm):
    d_ff = w2_ref.shape[0]
    x1 = x_ref[...] + _dot(a_ref[...], wo_ref[...])
    h = _rmsnorm_rows(x1, gf_ref[...], NORM_EPS).astype(BF16)
    acc_ref[...] = x1
    for f0 in range(0, d_ff, FFN_COL_TILE):
        gate = _dot(h, wi_ref[:, f0:f0 + FFN_COL_TILE])
        up = _dot(h, wi_ref[:, d_ff + f0:d_ff + f0 + FFN_COL_TILE])
        act = (gate * jax.nn.sigmoid(gate) * up).astype(BF16)
        acc_ref[...] += _dot(act, w2_ref[f0:f0 + FFN_COL_TILE, :])
    x2 = acc_ref[...]
    if final_norm:
        x2 = _rmsnorm_rows(x2, gl_ref[...], NORM_EPS)
    o_ref[...] = x2


def _mix_out_ffn(x2d, mix, w_out, g_ffn, w_in, w2, g_last, final_norm):
    t, d = x2d.shape
    d_ff = w2.shape[0]
    assert t % ROW_TILE == 0 and d_ff % FFN_COL_TILE == 0
    row_spec = pl.BlockSpec((ROW_TILE, d), lambda i: (i, 0))
    return pl.pallas_call(
        functools.partial(_mix_out_ffn_kernel, final_norm=final_norm),
        grid=(t // ROW_TILE,),
        in_specs=[row_spec, row_spec, _resident((d, d)), _resident((1, d)),
                  _resident((d, 2 * d_ff)), _resident((d_ff, d)), _resident((1, d))],
        out_specs=row_spec,
        out_shape=jax.ShapeDtypeStruct((t, d), F32),
        scratch_shapes=[pltpu.VMEM((ROW_TILE, d), F32)],
        compiler_params=pltpu.CompilerParams(
            dimension_semantics=("parallel",), vmem_limit_bytes=VMEM_LIMIT_BYTES),
        name="mix_out_ffn",
    )(x2d, mix, w_out.astype(BF16), g_ffn.reshape(1, d), w_in.astype(BF16), w2.astype(BF16),
      g_last.reshape(1, d))


def kernel(x, gla_w_in, gla_w_gate_a, gla_w_gate_b, gla_b_gate, gla_norm, gla_w_out, diff_w_in, diff_lam_q1, diff_lam_k1, diff_lam_q2, diff_lam_k2, diff_norm, diff_w_out, norm_mixer, norm_ffn, ffn_w_in, ffn_w_out, norm_final):
    batch, seq, d = x.shape
    x2d = x.reshape(batch * seq, d)

    proj, gk = _norm_proj_gate(x2d, norm_mixer[0], gla_w_in[0],
                               gla_w_gate_a[0], gla_w_gate_b[0], gla_b_gate[0])
    mix = _gla_core(proj, gk, gla_norm[0], batch, seq).reshape(batch * seq, -1)
    x2d = _mix_out_ffn(x2d, mix, gla_w_out[0], norm_ffn[0], ffn_w_in[0], ffn_w_out[0], norm_final,
                       final_norm=False)

    lambda_init = 0.8 - 0.6 * math.exp(-0.3 * 1)
    qk, v_t = _norm_proj_vt(x2d, norm_mixer[1], diff_w_in[0][:, :2 * d], diff_w_in[0][:, 2 * d:],
                            batch, seq)
    mix = _diff_attn(qk, v_t, diff_lam_q1[0], diff_lam_k1[0], diff_lam_q2[0], diff_lam_k2[0],
                     diff_norm[0], batch, seq, lambda_init).reshape(batch * seq, -1)
    x2d = _mix_out_ffn(x2d, mix, diff_w_out[0], norm_ffn[1], ffn_w_in[1], ffn_w_out[1], norm_final,
                       final_norm=True)
    return x2d.reshape(batch, seq, d)
```

```python
import functools
import math

import jax
import jax.numpy as jnp
from jax import lax
from jax.experimental import pallas as pl
from jax.experimental.pallas import tpu as pltpu

F32 = jnp.float32
BF16 = jnp.bfloat16

NORM_EPS = 1e-6
SUBLN_EPS = 1e-5

GLA_HEADS = 4
GLA_GATE_RANK = 16
GLA_TAU = 16.0
GLA_CHUNK = 64

DIFF_HEADS = 8

V7X_VMEM_BYTES = 64 * 1024 * 1024
VMEM_LIMIT_BYTES = V7X_VMEM_BYTES - 8 * 1024 * 1024

ROW_TILE = 512
PROJ_COL_TILE = 1024
FFN_COL_TILE = 256
GLA_GROUP = 256
ATTN_TILE = ROW_TILE
ATTN_SUB_TILE = 256
ATTN_SUM_ROWS = 16
ATTN_LOOKAHEAD = 4


def _resident(shape, layer=None):
    if layer is None:
        index, block = (0,) * len(shape), tuple(shape)
    else:
        index, block = (layer,) + (0,) * len(shape), (None,) + tuple(shape)
    return pl.BlockSpec(block, lambda *_: index, pipeline_mode=pl.Buffered(1))


def _rmsnorm_rows(x, gain, eps):
    return x * lax.rsqrt(jnp.mean(x * x, axis=-1, keepdims=True) + eps) * gain


def _dot(a, b):
    return jnp.dot(a, b, preferred_element_type=F32)


def _dot_nt(a, b):
    return lax.dot_general(a, b, (((1,), (1,)), ((), ())), preferred_element_type=F32)


def _dot_tn(a, b):
    return lax.dot_general(a, b, (((0,), (0,)), ((), ())), preferred_element_type=F32)


def _norm_proj_body(x_ref, g_ref, w_ref, o_ref):
    h = _rmsnorm_rows(x_ref[...], g_ref[...], NORM_EPS).astype(BF16)
    n_out = o_ref.shape[1]
    for c0 in range(0, n_out, PROJ_COL_TILE):
        o_ref[:, c0:c0 + PROJ_COL_TILE] = _dot(h, w_ref[:, c0:c0 + PROJ_COL_TILE]).astype(o_ref.dtype)
    return h


def _norm_proj_gate_kernel(x_ref, g_ref, w_ref, wa_ref, wb_ref, bg_ref, o_ref, gk_ref):
    h = _norm_proj_body(x_ref, g_ref, w_ref, o_ref)
    low = _dot(h, wa_ref[...])
    logits = _dot(low.astype(BF16), wb_ref[...]) + bg_ref[...]
    log_sig = jnp.minimum(logits, 0.0) - jnp.log1p(jnp.exp(-jnp.abs(logits)))
    gk_ref[...] = log_sig / GLA_TAU


def _norm_proj_vt_kernel(x_ref, g_ref, w_ref, wvt_ref, o_ref, vt_ref):
    h = _norm_proj_body(x_ref, g_ref, w_ref, o_ref)
    vt_ref[0, 0] = _dot_nt(wvt_ref[...], h).astype(vt_ref.dtype)


def _norm_proj_call(kernel, name, x2d, gain, w, n_out, extra_in, extra_specs, extra_out_shape, extra_out_spec):
    t, d = x2d.shape
    assert t % ROW_TILE == 0 and n_out % PROJ_COL_TILE == 0 and w.shape[1] >= n_out
    return pl.pallas_call(
        kernel,
        grid=(t // ROW_TILE,),
        in_specs=[pl.BlockSpec((ROW_TILE, d), lambda i: (i, 0)), _resident((1, d)),
                  _resident((d, n_out))] + extra_specs,
        out_specs=[pl.BlockSpec((ROW_TILE, n_out), lambda i: (i, 0)), extra_out_spec],
        out_shape=[jax.ShapeDtypeStruct((t, n_out), BF16), extra_out_shape],
        compiler_params=pltpu.CompilerParams(
            dimension_semantics=("parallel",), vmem_limit_bytes=VMEM_LIMIT_BYTES),
        name=name,
    )(x2d, gain.reshape(1, d), w.astype(BF16), *extra_in)


def _norm_proj_gate(x2d, gain, w, w_a, w_b, b_g):
    t, d = x2d.shape
    rank, dk = w_b.shape
    return _norm_proj_call(
        _norm_proj_gate_kernel, "norm_proj_gate", x2d, gain, w, w.shape[1],
        [w_a.astype(BF16), w_b.astype(BF16), b_g.reshape(1, dk)],
        [_resident((d, rank)), _resident((rank, dk)), _resident((1, dk))],
        jax.ShapeDtypeStruct((t, dk), F32), pl.BlockSpec((ROW_TILE, dk), lambda i: (i, 0)))


def _norm_proj_vt(x2d, gain, w, n_qk, batch, seq):
    t, d = x2d.shape
    d_v = w.shape[1] - n_qk
    tiles = seq // ROW_TILE
    return _norm_proj_call(
        _norm_proj_vt_kernel, "norm_proj_vt", x2d, gain, w, n_qk,
        [w[:, n_qk:].T.astype(BF16)], [_resident((d_v, d))],
        jax.ShapeDtypeStruct((batch, tiles, d_v, ROW_TILE), BF16),
        pl.BlockSpec((1, 1, d_v, ROW_TILE), lambda i: (i // tiles, i % tiles, 0, 0)))


def _gla_kernel(q_ref, k_ref, v_ref, r_ref, gk_ref, gn_ref, o_ref):
    c = GLA_CHUNK
    grp = GLA_GROUP
    n = grp // c
    seq, hk = q_ref.shape[1], q_ref.shape[2]
    hv = v_ref.shape[2]
    shift = c.bit_length() - 1
    row = lax.broadcasted_iota(jnp.int32, (grp, grp), 0)
    col = lax.broadcasted_iota(jnp.int32, (grp, grp), 1)
    mask = (col <= row) & ((row >> shift) == (col >> shift))
    tri = jnp.where(mask, 1.0, 0.0).astype(BF16)
    gain = gn_ref[...]
    q_scale = hk ** -0.5

    def rows_of(gi):
        return slice(gi * grp, (gi + 1) * grp)

    def cum_decay(gi):
        g = gk_ref[0, rows_of(gi), :]
        g_hi = g.astype(BF16)
        g_lo = (g - g_hi.astype(F32)).astype(BF16)
        return _dot(tri, g_hi) + _dot(tri, g_lo)

    def scaled_operands(gi, b):
        b = b.reshape(n, c, hk)
        b_mid = b[:, c // 2:c // 2 + 1, :]
        b_last = b[:, c - 1:c, :]
        q = (q_ref[0, rows_of(gi), :].astype(F32) * q_scale).reshape(n, c, hk)
        k = k_ref[0, rows_of(gi), :].astype(F32).reshape(n, c, hk)
        q_mid = (q * jnp.exp(b - b_mid)).reshape(grp, hk).astype(BF16)
        k_mid = (k * jnp.exp(b_mid - b)).reshape(grp, hk).astype(BF16)
        q_dec = (q * jnp.exp(b)).astype(BF16)
        k_end = (k * jnp.exp(b_last - b)).astype(BF16)
        return q_mid, k_mid, q_dec, k_end, jnp.exp(b_last)

    def state_part(gi, o_intra, q_dec, decay, incs, state_t):
        outs = []
        for ci in range(n):
            outs.append(o_intra[ci * c:(ci + 1) * c] + _dot_nt(q_dec[ci], state_t.astype(BF16)))
            state_t = state_t * decay[ci] + incs[ci]
        o = _rmsnorm_rows(jnp.concatenate(outs, axis=0), gain, NORM_EPS)
        r = r_ref[0, rows_of(gi), :].astype(F32)
        o_ref[0, rows_of(gi), :] = (o * (r * jax.nn.sigmoid(r))).astype(o_ref.dtype)
        return state_t

    groups = seq // grp
    state_t = jnp.zeros((hv, hk), F32)
    cum, ops = {}, {}
    for i in range(-2, groups):
        if i >= 0:
            q_mid, k_mid, q_dec, k_end, decay = ops.pop(i)
            v = v_ref[0, rows_of(i), :]
            attn = jnp.where(mask, _dot_nt(q_mid, k_mid), 0.0).astype(BF16)
            incs = [_dot_tn(v[ci * c:(ci + 1) * c], k_end[ci]) for ci in range(n)]
        if i + 2 < groups:
            cum[i + 2] = cum_decay(i + 2)
        if i >= 0:
            o_intra = _dot(attn, v)
        if 0 <= i + 1 < groups:
            ops[i + 1] = scaled_operands(i + 1, cum.pop(i + 1))
        if i >= 0:
            state_t = state_part(i, o_intra, q_dec, decay, incs, state_t)


def _gla_core(proj, gk, g_norm, batch, seq):
    dk = gk.shape[1]
    hk = dk // GLA_HEADS
    hv = g_norm.shape[0]
    dv = hv * GLA_HEADS
    assert proj.shape[1] == 2 * dk + 2 * dv and seq % GLA_GROUP == 0 and GLA_GROUP % GLA_CHUNK == 0
    proj3 = proj.reshape(batch, seq, proj.shape[1])
    gk3 = gk.reshape(batch, seq, dk)
    k_off, v_off, r_off = dk // hk, (2 * dk) // hv, (2 * dk + dv) // hv
    return pl.pallas_call(
        _gla_kernel,
        grid=(batch, GLA_HEADS),
        in_specs=[
            pl.BlockSpec((1, seq, hk), lambda b, h: (b, 0, h)),
            pl.BlockSpec((1, seq, hk), lambda b, h: (b, 0, k_off + h)),
            pl.BlockSpec((1, seq, hv), lambda b, h: (b, 0, v_off + h)),
            pl.BlockSpec((1, seq, hv), lambda b, h: (b, 0, r_off + h)),
            pl.BlockSpec((1, seq, hk), lambda b, h: (b, 0, h)),
            _resident((1, hv)),
        ],
        out_specs=pl.BlockSpec((1, seq, hv), lambda b, h: (b, 0, h)),
        out_shape=jax.ShapeDtypeStruct((batch, seq, dv), BF16),
        compiler_params=pltpu.CompilerParams(
            dimension_semantics=("parallel", "parallel"), vmem_limit_bytes=VMEM_LIMIT_BYTES),
        name="gla_core",
    )(proj3, proj3, proj3, proj3, gk3, g_norm.reshape(1, hv))


def _diff_attn_kernel(lq1_ref, lk1_ref, lq2_ref, lk2_ref, gn_ref, q_ref, k_ref, vt_ref, o_ref,
                      qq_ref, m_ref, acc_ref, *, lambda_init):
    tiles, d2, tq = vt_ref.shape[1], vt_ref.shape[2], vt_ref.shape[3]
    d = d2 // 2
    ts = ATTN_SUB_TILE
    assert tq % ts == 0

    lane = lax.broadcasted_iota(jnp.int32, (tq, d2), 1)
    for t in range(tiles):
        q = q_ref[0, t * tq:(t + 1) * tq, :].astype(F32) * (d ** -0.5 * math.log2(math.e))
        qq_ref[t, 0:tq, :] = jnp.where(lane < d, q, 0.0).astype(BF16)
        qq_ref[t, tq:2 * tq, :] = jnp.where(lane >= d, q, 0.0).astype(BF16)
    ones = jnp.ones((ATTN_SUM_ROWS, ts), BF16)
    lam = (jnp.exp(jnp.sum(lq1_ref[...] * lk1_ref[...], keepdims=True))
           - jnp.exp(jnp.sum(lq2_ref[...] * lk2_ref[...], keepdims=True)) + lambda_init)
    out_gain = gn_ref[...] * (1.0 - lambda_init)

    def scores(t, kt, off, l0, l1):
        k = k_ref[0, kt * tq + off:kt * tq + off + ts, :]
        st = _dot_nt(k, qq_ref[t, l0:l1, :])
        if kt == t:
            k_pos = off + lax.broadcasted_iota(jnp.int32, st.shape, 0)
            q_pos = (l0 + lax.broadcasted_iota(jnp.int32, st.shape, 1)) & (tq - 1)
            st = jnp.where(k_pos <= q_pos, st, -jnp.inf)
        return st

    def update(st, kt, off, l0, l1):
        m_prev = m_ref[:, l0:l1]
        m_new = jnp.maximum(m_prev, jnp.max(st, axis=0, keepdims=True))
        p = jnp.exp2(st - m_new).astype(BF16)
        alpha = jnp.exp2(m_prev - m_new)
        vt = jnp.concatenate([vt_ref[0, kt, :, off:off + ts], ones], axis=0)
        acc_ref[:, l0:l1] = alpha * acc_ref[:, l0:l1] + _dot(vt, p)
        m_ref[:, l0:l1] = m_new

    def finalize(t):
        o_all = acc_ref[0:d2, :] / acc_ref[d2:d2 + 1, :]
        o_t = o_all[:, 0:tq] - lam * o_all[:, tq:2 * tq]
        inv_rms = lax.rsqrt(jnp.mean(o_t * o_t, axis=0, keepdims=True) + SUBLN_EPS)
        o_ref[0, t * tq:(t + 1) * tq, :] = (o_t * inv_rms * out_gain).T.astype(o_ref.dtype)

    def query_ranges(t, kt, off):
        if kt < t or off == 0:
            return [(0, 2 * tq)]
        return [(off, tq), (tq + off, 2 * tq)]

    subs = [(t, kt, off, l0, l1) for t in range(tiles) for kt in range(t + 1)
            for off in range(0, tq, ts) for l0, l1 in query_ranges(t, kt, off)]
    first = {t: min(i for i, s in enumerate(subs) if s[0] == t) for t in range(tiles)}
    last = {t: max(i for i, s in enumerate(subs) if s[0] == t) for t in range(tiles)}
    pending = {}
    for i in range(len(subs) + ATTN_LOOKAHEAD):
        if i < len(subs):
            pending[i] = scores(*subs[i])
        j = i - ATTN_LOOKAHEAD
        if j >= 0:
            t = subs[j][0]
            if j == first[t]:
                m_ref[...] = jnp.full_like(m_ref, -jnp.inf)
                acc_ref[...] = jnp.zeros_like(acc_ref)
            update(pending.pop(j), *subs[j][1:])
            if j == last[t]:
                finalize(t)


def _diff_attn(qk, v_t, lam_q1, lam_k1, lam_q2, lam_k2, g_norm, batch, seq, lambda_init):
    d_model = qk.shape[1] // 2
    d2 = d_model // DIFF_HEADS
    d = d2 // 2
    tq = ATTN_TILE
    tiles = seq // tq
    assert seq % tq == 0 and g_norm.shape[0] == d2 and v_t.shape == (batch, tiles, d_model, tq)
    qk3 = qk.reshape(batch, seq, 2 * d_model)
    return pl.pallas_call(
        functools.partial(_diff_attn_kernel, lambda_init=lambda_init),
        grid=(batch, DIFF_HEADS),
        in_specs=[_resident((1, d))] * 4 + [
            _resident((d2, 1)),
            pl.BlockSpec((1, seq, d2), lambda b, h: (b, 0, h)),
            pl.BlockSpec((1, seq, d2), lambda b, h: (b, 0, DIFF_HEADS + h)),
            pl.BlockSpec((1, tiles, d2, tq), lambda b, h: (b, 0, h, 0)),
        ],
        out_specs=pl.BlockSpec((1, seq, d2), lambda b, h: (b, 0, h)),
        out_shape=jax.ShapeDtypeStruct((batch, seq, d_model), BF16),
        scratch_shapes=[
            pltpu.VMEM((tiles, 2 * tq, d2), BF16),
            pltpu.VMEM((1, 2 * tq), F32),
            pltpu.VMEM((d2 + ATTN_SUM_ROWS, 2 * tq), F32),
        ],
        compiler_params=pltpu.CompilerParams(
            dimension_semantics=("parallel", "parallel"), vmem_limit_bytes=VMEM_LIMIT_BYTES),
        name="diff_attn",
    )(lam_q1.reshape(1, d), lam_k1.reshape(1, d), lam_q2.reshape(1, d), lam_k2.reshape(1, d),
      g_norm.reshape(d2, 1), qk3, qk3, v_t)


def _mix_out_ffn_kernel(x_ref, a_ref, wo_ref, gf_ref, wi_ref, w2_ref, gl_ref, o_ref, acc_ref, *, final_norm):
    d_ff = w2_ref.shape[0]
    x1 = x_ref[...] + _dot(a_ref[...], wo_ref[...])
    h = _rmsnorm_rows(x1, gf_ref[...], NORM_EPS).astype(BF16)
    acc_ref[...] = x1
    for f0 in range(0, d_ff, FFN_COL_TILE):
        gate = _dot(h, wi_ref[:, f0:f0 + FFN_COL_TILE])
        up = _dot(h, wi_ref[:, d_ff + f0:d_ff + f0 + FFN_COL_TILE])
        act = (gate * jax.nn.sigmoid(gate) * up).astype(BF16)
        acc_ref[...] += _dot(act, w2_ref[f0:f0 + FFN_COL_TILE, :])
    x2 = acc_ref[...]
    if final_norm:
        x2 = _rmsnorm_rows(x2, gl_ref[...], NORM_EPS)
    o_ref[...] = x2


def _mix_out_ffn(x2d, mix, w_out, layer, g_ffn_all, w_in_all, w2_all, g_last, final_norm):
    t, d = x2d.shape
    d_ff = w2_all.shape[1]
    assert t % ROW_TILE == 0 and d_ff % FFN_COL_TILE == 0
    row_spec = pl.BlockSpec((ROW_TILE, d), lambda i: (i, 0))
    return pl.pallas_call(
        functools.partial(_mix_out_ffn_kernel, final_norm=final_norm),
        grid=(t // ROW_TILE,),
        in_specs=[row_spec, row_spec, _resident((d, d)), _resident((1, d), layer),
                  _resident((d, 2 * d_ff), layer), _resident((d_ff, d), layer), _resident((1, d))],
        out_specs=row_spec,
        out_shape=jax.ShapeDtypeStruct((t, d), F32),
        scratch_shapes=[pltpu.VMEM((ROW_TILE, d), F32)],
        compiler_params=pltpu.CompilerParams(
            dimension_semantics=("parallel",), vmem_limit_bytes=VMEM_LIMIT_BYTES),
        name="mix_out_ffn",
    )(x2d, mix, w_out.astype(BF16), g_ffn_all.reshape(-1, 1, d), w_in_all, w2_all, g_last.reshape(1, d))


def kernel(x, gla_w_in, gla_w_gate_a, gla_w_gate_b, gla_b_gate, gla_norm, gla_w_out, diff_w_in, diff_lam_q1, diff_lam_k1, diff_lam_q2, diff_lam_k2, diff_norm, diff_w_out, norm_mixer, norm_ffn, ffn_w_in, ffn_w_out, norm_final):
    batch, seq, d = x.shape
    x2d = x.reshape(batch * seq, d)
    ffn_w_in_bf, ffn_w_out_bf = ffn_w_in.astype(BF16), ffn_w_out.astype(BF16)

    proj, gk = _norm_proj_gate(x2d, norm_mixer[0], gla_w_in[0],
                               gla_w_gate_a[0], gla_w_gate_b[0], gla_b_gate[0])
    mix = _gla_core(proj, gk, gla_norm[0], batch, seq).reshape(batch * seq, -1)
    x2d = _mix_out_ffn(x2d, mix, gla_w_out[0], 0, norm_ffn, ffn_w_in_bf, ffn_w_out_bf, norm_final,
                       final_norm=False)

    lambda_init = 0.8 - 0.6 * math.exp(-0.3 * 1)
    qk, v_t = _norm_proj_vt(x2d, norm_mixer[1], diff_w_in[0], 2 * d, batch, seq)
    mix = _diff_attn(qk, v_t, diff_lam_q1[0], diff_lam_k1[0], diff_lam_q2[0], diff_lam_k2[0],
                     diff_norm[0], batch, seq, lambda_init).reshape(batch * seq, -1)
    x2d = _mix_out_ffn(x2d, mix, diff_w_out[0], 1, norm_ffn, ffn_w_in_bf, ffn_w_out_bf, norm_final,
                       final_norm=True)
    return x2d.reshape(batch, seq, d)
```

```python
import functools
import math

import jax
import jax.numpy as jnp
from jax import lax
from jax.experimental import pallas as pl
from jax.experimental.pallas import tpu as pltpu

F32 = jnp.float32
BF16 = jnp.bfloat16

NORM_EPS = 1e-6
SUBLN_EPS = 1e-5

GLA_HEADS = 4
GLA_GATE_RANK = 16
GLA_TAU = 16.0
GLA_CHUNK = 64

DIFF_HEADS = 8

V7X_VMEM_BYTES = 64 * 1024 * 1024
VMEM_LIMIT_BYTES = V7X_VMEM_BYTES - 8 * 1024 * 1024

ROW_TILE = 512
PROJ_COL_TILE = 1024
PROJ_ROW_SUB = 256
FFN_COL_TILE = 256
GLA_GROUP = 256
ATTN_TILE = ROW_TILE
ATTN_SUB_TILE = 256
ATTN_SUM_ROWS = 16
ATTN_LOOKAHEAD = 2


def _resident(shape, layer=None):
    if layer is None:
        index, block = (0,) * len(shape), tuple(shape)
    else:
        index, block = (layer,) + (0,) * len(shape), (None,) + tuple(shape)
    return pl.BlockSpec(block, lambda *_: index, pipeline_mode=pl.Buffered(1))


def _rmsnorm_rows(x, gain, eps):
    return x * lax.rsqrt(jnp.mean(x * x, axis=-1, keepdims=True) + eps) * gain


def _dot(a, b):
    return jnp.dot(a, b, preferred_element_type=F32)


def _dot_nt(a, b):
    return lax.dot_general(a, b, (((1,), (1,)), ((), ())), preferred_element_type=F32)


def _dot_tn(a, b):
    return lax.dot_general(a, b, (((0,), (0,)), ((), ())), preferred_element_type=F32)


def _norm_proj_body(x_ref, g_ref, w_ref, o_ref, extra):
    blocks = [slice(r0, r0 + PROJ_ROW_SUB) for r0 in range(0, x_ref.shape[0], PROJ_ROW_SUB)]
    hs = [_rmsnorm_rows(x_ref[rows, :], g_ref[...], NORM_EPS).astype(BF16) for rows in blocks]
    n_out = o_ref.shape[1]
    for rows, h in zip(blocks, hs):
        extra(rows, h)
        for c0 in range(0, n_out, PROJ_COL_TILE):
            cols = slice(c0, c0 + PROJ_COL_TILE)
            o_ref[rows, cols] = _dot(h, w_ref[:, cols]).astype(o_ref.dtype)


def _norm_proj_gate_kernel(x_ref, g_ref, w_ref, wa_ref, wb_ref, bg_ref, o_ref, gk_ref):
    def gate(rows, h):
        low = _dot(h, wa_ref[...])
        logits = _dot(low.astype(BF16), wb_ref[...]) + bg_ref[...]
        log_sig = jnp.minimum(logits, 0.0) - jnp.log(1.0 + jnp.exp(-jnp.abs(logits)))
        gk_ref[rows, :] = log_sig * (math.log2(math.e) / GLA_TAU)

    _norm_proj_body(x_ref, g_ref, w_ref, o_ref, gate)


def _norm_proj_vt_kernel(x_ref, g_ref, w_ref, wvt_ref, o_ref, vt_ref):
    def values_t(rows, h):
        vt_ref[0, 0, :, rows] = _dot_nt(wvt_ref[...], h).astype(vt_ref.dtype)

    _norm_proj_body(x_ref, g_ref, w_ref, o_ref, values_t)


def _norm_proj_call(kernel, name, x2d, gain, w, n_out, extra_in, extra_specs, extra_out_shape, extra_out_spec):
    t, d = x2d.shape
    assert t % ROW_TILE == 0 and n_out % PROJ_COL_TILE == 0 and w.shape[1] >= n_out
    return pl.pallas_call(
        kernel,
        grid=(t // ROW_TILE,),
        in_specs=[pl.BlockSpec((ROW_TILE, d), lambda i: (i, 0)), _resident((1, d)),
                  _resident((d, n_out))] + extra_specs,
        out_specs=[pl.BlockSpec((ROW_TILE, n_out), lambda i: (i, 0)), extra_out_spec],
        out_shape=[jax.ShapeDtypeStruct((t, n_out), BF16), extra_out_shape],
        compiler_params=pltpu.CompilerParams(
            dimension_semantics=("parallel",), vmem_limit_bytes=VMEM_LIMIT_BYTES),
        name=name,
    )(x2d, gain.reshape(1, d), w.astype(BF16), *extra_in)


def _norm_proj_gate(x2d, gain, w, w_a, w_b, b_g):
    t, d = x2d.shape
    rank, dk = w_b.shape
    return _norm_proj_call(
        _norm_proj_gate_kernel, "norm_proj_gate", x2d, gain, w, w.shape[1],
        [w_a.astype(BF16), w_b.astype(BF16), b_g.reshape(1, dk)],
        [_resident((d, rank)), _resident((rank, dk)), _resident((1, dk))],
        jax.ShapeDtypeStruct((t, dk), F32), pl.BlockSpec((ROW_TILE, dk), lambda i: (i, 0)))


def _norm_proj_vt(x2d, gain, w, n_qk, batch, seq):
    t, d = x2d.shape
    d_v = w.shape[1] - n_qk
    tiles = seq // ROW_TILE
    return _norm_proj_call(
        _norm_proj_vt_kernel, "norm_proj_vt", x2d, gain, w, n_qk,
        [w[:, n_qk:].T.astype(BF16)], [_resident((d_v, d))],
        jax.ShapeDtypeStruct((batch, tiles, d_v, ROW_TILE), BF16),
        pl.BlockSpec((1, 1, d_v, ROW_TILE), lambda i: (i // tiles, i % tiles, 0, 0)))


def _gla_kernel(q_ref, k_ref, v_ref, r_ref, gk_ref, gn_ref, o_ref):
    c = GLA_CHUNK
    grp = GLA_GROUP
    n = grp // c
    seq, hk = q_ref.shape[1], q_ref.shape[2]
    hv = v_ref.shape[2]
    shift = c.bit_length() - 1
    row = lax.broadcasted_iota(jnp.int32, (grp, grp), 0)
    col = lax.broadcasted_iota(jnp.int32, (grp, grp), 1)
    mask = (col <= row) & ((row >> shift) == (col >> shift))
    tri = jnp.where(mask, 1.0, 0.0).astype(BF16)
    gain = gn_ref[...]
    q_scale = hk ** -0.5

    def rows_of(gi):
        return slice(gi * grp, (gi + 1) * grp)

    def cum_decay(gi):
        g = gk_ref[0, rows_of(gi), :]
        g_hi = g.astype(BF16)
        g_lo = (g - g_hi.astype(F32)).astype(BF16)
        return _dot(tri, g_hi) + _dot(tri, g_lo)

    def scaled_operands(gi, b):
        b = b.reshape(n, c, hk)
        b_mid = b[:, c // 2:c // 2 + 1, :]
        b_last = b[:, c - 1:c, :]
        q = (q_ref[0, rows_of(gi), :].astype(F32) * q_scale).reshape(n, c, hk)
        k = k_ref[0, rows_of(gi), :].astype(F32).reshape(n, c, hk)
        q_mid = (q * jnp.exp2(b - b_mid)).reshape(grp, hk).astype(BF16)
        k_mid = (k * jnp.exp2(b_mid - b)).reshape(grp, hk).astype(BF16)
        q_dec = (q * jnp.exp2(b)).astype(BF16)
        k_end = (k * jnp.exp2(b_last - b)).astype(BF16)
        return q_mid, k_mid, q_dec, k_end, jnp.exp2(b_last)

    def state_part(gi, o_intra, q_dec, decay, incs, state_t):
        outs = []
        for ci in range(n):
            outs.append(o_intra[ci * c:(ci + 1) * c] + _dot_nt(q_dec[ci], state_t.astype(BF16)))
            state_t = state_t * decay[ci] + incs[ci]
        o = _rmsnorm_rows(jnp.concatenate(outs, axis=0), gain, NORM_EPS)
        r = r_ref[0, rows_of(gi), :].astype(F32)
        o_ref[0, rows_of(gi), :] = (o * (r * jax.nn.sigmoid(r))).astype(o_ref.dtype)
        return state_t

    groups = seq // grp
    state_t = jnp.zeros((hv, hk), F32)
    cum, ops = {}, {}
    for i in range(-2, groups):
        if i >= 0:
            q_mid, k_mid, q_dec, k_end, decay = ops.pop(i)
            v = v_ref[0, rows_of(i), :]
            attn = jnp.where(mask, _dot_nt(q_mid, k_mid), 0.0).astype(BF16)
            incs = [_dot_tn(v[ci * c:(ci + 1) * c], k_end[ci]) for ci in range(n)]
        if i + 2 < groups:
            cum[i + 2] = cum_decay(i + 2)
        if i >= 0:
            o_intra = _dot(attn, v)
        if 0 <= i + 1 < groups:
            ops[i + 1] = scaled_operands(i + 1, cum.pop(i + 1))
        if i >= 0:
            state_t = state_part(i, o_intra, q_dec, decay, incs, state_t)


def _gla_core(proj, gk, g_norm, batch, seq):
    dk = gk.shape[1]
    hk = dk // GLA_HEADS
    hv = g_norm.shape[0]
    dv = hv * GLA_HEADS
    assert proj.shape[1] == 2 * dk + 2 * dv and seq % GLA_GROUP == 0 and GLA_GROUP % GLA_CHUNK == 0
    proj3 = proj.reshape(batch, seq, proj.shape[1])
    gk3 = gk.reshape(batch, seq, dk)
    k_off, v_off, r_off = dk // hk, (2 * dk) // hv, (2 * dk + dv) // hv
    return pl.pallas_call(
        _gla_kernel,
        grid=(batch, GLA_HEADS),
        in_specs=[
            pl.BlockSpec((1, seq, hk), lambda b, h: (b, 0, h)),
            pl.BlockSpec((1, seq, hk), lambda b, h: (b, 0, k_off + h)),
            pl.BlockSpec((1, seq, hv), lambda b, h: (b, 0, v_off + h)),
            pl.BlockSpec((1, seq, hv), lambda b, h: (b, 0, r_off + h)),
            pl.BlockSpec((1, seq, hk), lambda b, h: (b, 0, h)),
            _resident((1, hv)),
        ],
        out_specs=pl.BlockSpec((1, seq, hv), lambda b, h: (b, 0, h)),
        out_shape=jax.ShapeDtypeStruct((batch, seq, dv), BF16),
        compiler_params=pltpu.CompilerParams(
            dimension_semantics=("parallel", "parallel"), vmem_limit_bytes=VMEM_LIMIT_BYTES),
        name="gla_core",
    )(proj3, proj3, proj3, proj3, gk3, g_norm.reshape(1, hv))


def _diff_attn_kernel(lq1_ref, lk1_ref, lq2_ref, lk2_ref, gn_ref, q_ref, k_ref, vt_ref, o_ref,
                      qq_ref, m_ref, acc_ref, *, lambda_init):
    tiles, d2, tq = vt_ref.shape[1], vt_ref.shape[2], vt_ref.shape[3]
    d = d2 // 2
    ts = ATTN_SUB_TILE
    assert tq % ts == 0

    lane = lax.broadcasted_iota(jnp.int32, (tq, d2), 1)
    for t in range(tiles):
        q = q_ref[0, t * tq:(t + 1) * tq, :].astype(F32) * (d ** -0.5 * math.log2(math.e))
        qq_ref[t, 0:tq, :] = jnp.where(lane < d, q, 0.0).astype(BF16)
        qq_ref[t, tq:2 * tq, :] = jnp.where(lane >= d, q, 0.0).astype(BF16)
    lam = (jnp.exp(jnp.sum(lq1_ref[...] * lk1_ref[...], keepdims=True))
           - jnp.exp(jnp.sum(lq2_ref[...] * lk2_ref[...], keepdims=True)) + lambda_init)
    out_gain = gn_ref[...] * (1.0 - lambda_init)

    def query_ranges(t, kt, off):
        if kt < t or off == 0:
            return [(0, 2 * tq)]
        return [(off, tq), (tq + off, 2 * tq)]

    def scores(t, kt):
        pieces = []
        for off in range(0, tq, ts):
            k = k_ref[0, kt * tq + off:kt * tq + off + ts, :]
            for l0, l1 in query_ranges(t, kt, off):
                st = _dot_nt(k, qq_ref[t, l0:l1, :])
                if kt == t:
                    k_pos = off + lax.broadcasted_iota(jnp.int32, st.shape, 0)
                    q_pos = (l0 + lax.broadcasted_iota(jnp.int32, st.shape, 1)) & (tq - 1)
                    st = jnp.where(k_pos <= q_pos, st, -jnp.inf)
                pieces.append((off, l0, l1, st))
        return pieces

    def update(pieces, kt):
        m_prev = m_ref[...]
        for _, l0, l1, st in pieces:
            m_ref[:, l0:l1] = jnp.maximum(m_ref[:, l0:l1], jnp.max(st, axis=0, keepdims=True))
        m_new = m_ref[...]
        alpha = jnp.exp2(m_prev - m_new)
        edges = sorted({e for _, l0, l1, _ in pieces for e in (l0, l1)})
        for a, b in zip(edges[:-1], edges[1:]):
            parts = [(off, st[:, a - l0:b - l0]) for off, l0, l1, st in pieces if l0 <= a and b <= l1]
            p = jnp.concatenate([jnp.exp2(st - m_new[:, a:b]).astype(BF16) for _, st in parts], axis=0)
            vt = jnp.concatenate([vt_ref[0, kt, :, off:off + ts] for off, _ in parts], axis=1)
            vt = jnp.concatenate([vt, jnp.ones((ATTN_SUM_ROWS, vt.shape[1]), BF16)], axis=0)
            acc_ref[:, a:b] = alpha[:, a:b] * acc_ref[:, a:b] + _dot(vt, p)

    def finalize(t):
        o_all = acc_ref[0:d2, :] / acc_ref[d2:d2 + 1, :]
        o_t = o_all[:, 0:tq] - lam * o_all[:, tq:2 * tq]
        inv_rms = lax.rsqrt(jnp.mean(o_t * o_t, axis=0, keepdims=True) + SUBLN_EPS)
        o_ref[0, t * tq:(t + 1) * tq, :] = (o_t * inv_rms * out_gain).T.astype(o_ref.dtype)

    units = [(t, kt) for t in range(tiles) for kt in range(t + 1)]
    pending = {}
    for i in range(len(units) + ATTN_LOOKAHEAD):
        if i < len(units):
            pending[i] = scores(*units[i])
        j = i - ATTN_LOOKAHEAD
        if j >= 0:
            t, kt = units[j]
            if kt == 0:
                m_ref[...] = jnp.full_like(m_ref, -jnp.inf)
                acc_ref[...] = jnp.zeros_like(acc_ref)
            update(pending.pop(j), kt)
            if kt == t:
                finalize(t)


def _diff_attn(qk, v_t, lam_q1, lam_k1, lam_q2, lam_k2, g_norm, batch, seq, lambda_init):
    d_model = qk.shape[1] // 2
    d2 = d_model // DIFF_HEADS
    d = d2 // 2
    tq = ATTN_TILE
    tiles = seq // tq
    assert seq % tq == 0 and g_norm.shape[0] == d2 and v_t.shape == (batch, tiles, d_model, tq)
    qk3 = qk.reshape(batch, seq, 2 * d_model)
    return pl.pallas_call(
        functools.partial(_diff_attn_kernel, lambda_init=lambda_init),
        grid=(batch, DIFF_HEADS),
        in_specs=[_resident((1, d))] * 4 + [
            _resident((d2, 1)),
            pl.BlockSpec((1, seq, d2), lambda b, h: (b, 0, h)),
            pl.BlockSpec((1, seq, d2), lambda b, h: (b, 0, DIFF_HEADS + h)),
            pl.BlockSpec((1, tiles, d2, tq), lambda b, h: (b, 0, h, 0)),
        ],
        out_specs=pl.BlockSpec((1, seq, d2), lambda b, h: (b, 0, h)),
        out_shape=jax.ShapeDtypeStruct((batch, seq, d_model), BF16),
        scratch_shapes=[
            pltpu.VMEM((tiles, 2 * tq, d2), BF16),
            pltpu.VMEM((1, 2 * tq), F32),
            pltpu.VMEM((d2 + ATTN_SUM_ROWS, 2 * tq), F32),
        ],
        compiler_params=pltpu.CompilerParams(
            dimension_semantics=("parallel", "parallel"), vmem_limit_bytes=VMEM_LIMIT_BYTES),
        name="diff_attn",
    )(lam_q1.reshape(1, d), lam_k1.reshape(1, d), lam_q2.reshape(1, d), lam_k2.reshape(1, d),
      g_norm.reshape(d2, 1), qk3, qk3, v_t)


def _mix_out_ffn_kernel(x_ref, a_ref, wo_ref, gf_ref, wi_ref, w2_ref, gl_ref, o_ref, acc_ref, *, final_norm):
    d_ff = w2_ref.shape[0]
    x1 = x_ref[...] + _dot(a_ref[...], wo_ref[...])
    h = _rmsnorm_rows(x1, gf_ref[...], NORM_EPS).astype(BF16)
    acc_ref[...] = x1
    for f0 in range(0, d_ff, FFN_COL_TILE):
        gate = _dot(h, wi_ref[:, f0:f0 + FFN_COL_TILE])
        up = _dot(h, wi_ref[:, d_ff + f0:d_ff + f0 + FFN_COL_TILE])
        act = (gate * jax.nn.sigmoid(gate) * up).astype(BF16)
        acc_ref[...] += _dot(act, w2_ref[f0:f0 + FFN_COL_TILE, :])
    x2 = acc_ref[...]
    if final_norm:
        x2 = _rmsnorm_rows(x2, gl_ref[...], NORM_EPS)
    o_ref[...] = x2


def _mix_out_ffn(x2d, mix, w_out, layer, g_ffn_all, w_in_all, w2_all, g_last, final_norm):
    t, d = x2d.shape
    d_ff = w2_all.shape[1]
    assert t % ROW_TILE == 0 and d_ff % FFN_COL_TILE == 0
    row_spec = pl.BlockSpec((ROW_TILE, d), lambda i: (i, 0))
    return pl.pallas_call(
        functools.partial(_mix_out_ffn_kernel, final_norm=final_norm),
        grid=(t // ROW_TILE,),
        in_specs=[row_spec, row_spec, _resident((d, d)), _resident((1, d), layer),
                  _resident((d, 2 * d_ff), layer), _resident((d_ff, d), layer), _resident((1, d))],
        out_specs=row_spec,
        out_shape=jax.ShapeDtypeStruct((t, d), F32),
        scratch_shapes=[pltpu.VMEM((ROW_TILE, d), F32)],
        compiler_params=pltpu.CompilerParams(
            dimension_semantics=("parallel",), vmem_limit_bytes=VMEM_LIMIT_BYTES),
        name="mix_out_ffn",
    )(x2d, mix, w_out.astype(BF16), g_ffn_all.reshape(-1, 1, d), w_in_all, w2_all, g_last.reshape(1, d))


def kernel(x, gla_w_in, gla_w_gate_a, gla_w_gate_b, gla_b_gate, gla_norm, gla_w_out, diff_w_in, diff_lam_q1, diff_lam_k1, diff_lam_q2, diff_lam_k2, diff_norm, diff_w_out, norm_mixer, norm_ffn, ffn_w_in, ffn_w_out, norm_final):
    batch, seq, d = x.shape
    x2d = x.reshape(batch * seq, d)
    ffn_w_in_bf, ffn_w_out_bf = ffn_w_in.astype(BF16), ffn_w_out.astype(BF16)

    proj, gk = _norm_proj_gate(x2d, norm_mixer[0], gla_w_in[0],
                               gla_w_gate_a[0], gla_w_gate_b[0], gla_b_gate[0])
    mix = _gla_core(proj, gk, gla_norm[0], batch, seq).reshape(batch * seq, -1)
    x2d = _mix_out_ffn(x2d, mix, gla_w_out[0], 0, norm_ffn, ffn_w_in_bf, ffn_w_out_bf, norm_final,
                       final_norm=False)

    lambda_init = 0.8 - 0.6 * math.exp(-0.3 * 1)
    qk, v_t = _norm_proj_vt(x2d, norm_mixer[1], diff_w_in[0], 2 * d, batch, seq)
    mix = _diff_attn(qk, v_t, diff_lam_q1[0], diff_lam_k1[0], diff_lam_q2[0], diff_lam_k2[0],
                     diff_norm[0], batch, seq, lambda_init).reshape(batch * seq, -1)
    x2d = _mix_out_ffn(x2d, mix, diff_w_out[0], 1, norm_ffn, ffn_w_in_bf, ffn_w_out_bf, norm_final,
                       final_norm=True)
    return x2d.reshape(batch, seq, d)
```

```python
import functools
import math

import jax
import jax.numpy as jnp
from jax import lax
from jax.experimental import pallas as pl
from jax.experimental.pallas import tpu as pltpu

F32 = jnp.float32
BF16 = jnp.bfloat16

NORM_EPS = 1e-6
SUBLN_EPS = 1e-5

GLA_HEADS = 4
GLA_GATE_RANK = 16
GLA_TAU = 16.0
GLA_CHUNK = 64

DIFF_HEADS = 8

V7X_VMEM_BYTES = 64 * 1024 * 1024
VMEM_LIMIT_BYTES = V7X_VMEM_BYTES - 8 * 1024 * 1024

ROW_TILE = 512
PROJ_COL_TILE = 1024
FFN_COL_TILE = 256
GLA_GROUP = 256
ATTN_TILE = ROW_TILE
ATTN_SUB_TILE = 256
ATTN_SUM_ROWS = 16
ATTN_LOOKAHEAD = 2


def _resident(shape, layer=None):
    if layer is None:
        index, block = (0,) * len(shape), tuple(shape)
    else:
        index, block = (layer,) + (0,) * len(shape), (None,) + tuple(shape)
    return pl.BlockSpec(block, lambda *_: index, pipeline_mode=pl.Buffered(1))


def _rmsnorm_rows(x, gain, eps):
    return x * lax.rsqrt(jnp.mean(x * x, axis=-1, keepdims=True) + eps) * gain


def _dot(a, b):
    return jnp.dot(a, b, preferred_element_type=F32)


def _dot_nt(a, b):
    return lax.dot_general(a, b, (((1,), (1,)), ((), ())), preferred_element_type=F32)


def _dot_tn(a, b):
    return lax.dot_general(a, b, (((0,), (0,)), ((), ())), preferred_element_type=F32)


def _project(h, w_ref, o_ref):
    for c0 in range(0, o_ref.shape[1], PROJ_COL_TILE):
        cols = slice(c0, c0 + PROJ_COL_TILE)
        o_ref[:, cols] = _dot(h, w_ref[:, cols]).astype(o_ref.dtype)


def _norm_proj_gate_kernel(x_ref, g_ref, w_ref, wa_ref, wb_ref, bg_ref, o_ref, gk_ref):
    h = _rmsnorm_rows(x_ref[...], g_ref[...], NORM_EPS).astype(BF16)
    low = _dot(h, wa_ref[...])
    logits = _dot(low.astype(BF16), wb_ref[...]) + bg_ref[...]
    log_sig = jnp.minimum(logits, 0.0) - jnp.log(1.0 + jnp.exp(-jnp.abs(logits)))
    gk_ref[...] = log_sig * (math.log2(math.e) / GLA_TAU)
    _project(h, w_ref, o_ref)


def _norm_proj_vt_kernel(x_ref, g_ref, w_ref, wvt_ref, o_ref, vt_ref):
    h = _rmsnorm_rows(x_ref[...], g_ref[...], NORM_EPS).astype(BF16)
    _project(h, w_ref, o_ref)
    vt_ref[0, 0] = _dot_nt(wvt_ref[...], h).astype(vt_ref.dtype)


def _norm_proj_call(kernel, name, x2d, gain, w, n_out, extra_in, extra_specs, extra_out_shape, extra_out_spec):
    t, d = x2d.shape
    assert t % ROW_TILE == 0 and n_out % PROJ_COL_TILE == 0 and w.shape[1] >= n_out
    return pl.pallas_call(
        kernel,
        grid=(t // ROW_TILE,),
        in_specs=[pl.BlockSpec((ROW_TILE, d), lambda i: (i, 0)), _resident((1, d)),
                  _resident((d, n_out))] + extra_specs,
        out_specs=[pl.BlockSpec((ROW_TILE, n_out), lambda i: (i, 0)), extra_out_spec],
        out_shape=[jax.ShapeDtypeStruct((t, n_out), BF16), extra_out_shape],
        compiler_params=pltpu.CompilerParams(
            dimension_semantics=("parallel",), vmem_limit_bytes=VMEM_LIMIT_BYTES),
        name=name,
    )(x2d, gain.reshape(1, d), w.astype(BF16), *extra_in)


def _norm_proj_gate(x2d, gain, w, w_a, w_b, b_g):
    t, d = x2d.shape
    rank, dk = w_b.shape
    return _norm_proj_call(
        _norm_proj_gate_kernel, "norm_proj_gate", x2d, gain, w, w.shape[1],
        [w_a.astype(BF16), w_b.astype(BF16), b_g.reshape(1, dk)],
        [_resident((d, rank)), _resident((rank, dk)), _resident((1, dk))],
        jax.ShapeDtypeStruct((t, dk), F32), pl.BlockSpec((ROW_TILE, dk), lambda i: (i, 0)))


def _norm_proj_vt(x2d, gain, w, n_qk, batch, seq):
    t, d = x2d.shape
    d_v = w.shape[1] - n_qk
    tiles = seq // ROW_TILE
    return _norm_proj_call(
        _norm_proj_vt_kernel, "norm_proj_vt", x2d, gain, w, n_qk,
        [w[:, n_qk:].T.astype(BF16)], [_resident((d_v, d))],
        jax.ShapeDtypeStruct((batch, tiles, d_v, ROW_TILE), BF16),
        pl.BlockSpec((1, 1, d_v, ROW_TILE), lambda i: (i // tiles, i % tiles, 0, 0)))


def _gla_kernel(q_ref, k_ref, v_ref, r_ref, gk_ref, gn_ref, o_ref):
    c = GLA_CHUNK
    grp = GLA_GROUP
    n = grp // c
    seq, hk = q_ref.shape[1], q_ref.shape[2]
    hv = v_ref.shape[2]
    shift = c.bit_length() - 1
    row = lax.broadcasted_iota(jnp.int32, (grp, grp), 0)
    col = lax.broadcasted_iota(jnp.int32, (grp, grp), 1)
    mask = (col <= row) & ((row >> shift) == (col >> shift))
    tri = jnp.where(mask, 1.0, 0.0).astype(BF16)
    gain = gn_ref[...]
    q_scale = hk ** -0.5

    def rows_of(gi):
        return slice(gi * grp, (gi + 1) * grp)

    def cum_decay(gi):
        g = gk_ref[0, rows_of(gi), :]
        g_hi = g.astype(BF16)
        g_lo = (g - g_hi.astype(F32)).astype(BF16)
        return _dot(tri, g_hi) + _dot(tri, g_lo)

    def scaled_operands(gi, b):
        b = b.reshape(n, c, hk)
        b_mid = b[:, c // 2:c // 2 + 1, :]
        b_last = b[:, c - 1:c, :]
        q = (q_ref[0, rows_of(gi), :].astype(F32) * q_scale).reshape(n, c, hk)
        k = k_ref[0, rows_of(gi), :].astype(F32).reshape(n, c, hk)
        q_mid = (q * jnp.exp2(b - b_mid)).reshape(grp, hk).astype(BF16)
        k_mid = (k * jnp.exp2(b_mid - b)).reshape(grp, hk).astype(BF16)
        q_dec = (q * jnp.exp2(b)).astype(BF16)
        k_end = (k * jnp.exp2(b_last - b)).astype(BF16)
        return q_mid, k_mid, q_dec, k_end, jnp.exp2(b_last)

    def state_part(gi, o_intra, q_dec, decay, incs, state_t):
        outs = []
        for ci in range(n):
            outs.append(o_intra[ci * c:(ci + 1) * c] + _dot_nt(q_dec[ci], state_t.astype(BF16)))
            state_t = state_t * decay[ci] + incs[ci]
        o = _rmsnorm_rows(jnp.concatenate(outs, axis=0), gain, NORM_EPS)
        r = r_ref[0, rows_of(gi), :].astype(F32)
        o_ref[0, rows_of(gi), :] = (o * (r * jax.nn.sigmoid(r))).astype(o_ref.dtype)
        return state_t

    groups = seq // grp
    state_t = jnp.zeros((hv, hk), F32)
    cum, ops = {}, {}
    for i in range(-2, groups):
        if i >= 0:
            q_mid, k_mid, q_dec, k_end, decay = ops.pop(i)
            v = v_ref[0, rows_of(i), :]
            attn = jnp.where(mask, _dot_nt(q_mid, k_mid), 0.0).astype(BF16)
            incs = [_dot_tn(v[ci * c:(ci + 1) * c], k_end[ci]) for ci in range(n)]
        if i + 2 < groups:
            cum[i + 2] = cum_decay(i + 2)
        if i >= 0:
            o_intra = _dot(attn, v)
        if 0 <= i + 1 < groups:
            ops[i + 1] = scaled_operands(i + 1, cum.pop(i + 1))
        if i >= 0:
            state_t = state_part(i, o_intra, q_dec, decay, incs, state_t)


def _gla_core(proj, gk, g_norm, batch, seq):
    dk = gk.shape[1]
    hk = dk // GLA_HEADS
    hv = g_norm.shape[0]
    dv = hv * GLA_HEADS
    assert proj.shape[1] == 2 * dk + 2 * dv and seq % GLA_GROUP == 0 and GLA_GROUP % GLA_CHUNK == 0
    proj3 = proj.reshape(batch, seq, proj.shape[1])
    gk3 = gk.reshape(batch, seq, dk)
    k_off, v_off, r_off = dk // hk, (2 * dk) // hv, (2 * dk + dv) // hv
    return pl.pallas_call(
        _gla_kernel,
        grid=(batch, GLA_HEADS),
        in_specs=[
            pl.BlockSpec((1, seq, hk), lambda b, h: (b, 0, h)),
            pl.BlockSpec((1, seq, hk), lambda b, h: (b, 0, k_off + h)),
            pl.BlockSpec((1, seq, hv), lambda b, h: (b, 0, v_off + h)),
            pl.BlockSpec((1, seq, hv), lambda b, h: (b, 0, r_off + h)),
            pl.BlockSpec((1, seq, hk), lambda b, h: (b, 0, h)),
            _resident((1, hv)),
        ],
        out_specs=pl.BlockSpec((1, seq, hv), lambda b, h: (b, 0, h)),
        out_shape=jax.ShapeDtypeStruct((batch, seq, dv), BF16),
        compiler_params=pltpu.CompilerParams(
            dimension_semantics=("parallel", "parallel"), vmem_limit_bytes=VMEM_LIMIT_BYTES),
        name="gla_core",
    )(proj3, proj3, proj3, proj3, gk3, g_norm.reshape(1, hv))


def _diff_attn_kernel(lq1_ref, lk1_ref, lq2_ref, lk2_ref, gn_ref, q_ref, k_ref, vt_ref, o_ref,
                      qq_ref, m_ref, acc_ref, *, lambda_init):
    tiles, d2, tq = vt_ref.shape[1], vt_ref.shape[2], vt_ref.shape[3]
    d = d2 // 2
    ts = ATTN_SUB_TILE
    assert tq % ts == 0

    lane = lax.broadcasted_iota(jnp.int32, (tq, d2), 1)
    for t in range(tiles):
        q = q_ref[0, t * tq:(t + 1) * tq, :].astype(F32) * (d ** -0.5 * math.log2(math.e))
        qq_ref[t, 0:tq, :] = jnp.where(lane < d, q, 0.0).astype(BF16)
        qq_ref[t, tq:2 * tq, :] = jnp.where(lane >= d, q, 0.0).astype(BF16)
    lam = (jnp.exp(jnp.sum(lq1_ref[...] * lk1_ref[...], keepdims=True))
           - jnp.exp(jnp.sum(lq2_ref[...] * lk2_ref[...], keepdims=True)) + lambda_init)
    out_gain = gn_ref[...] * (1.0 - lambda_init)

    def query_ranges(t, kt, off):
        if kt < t or off == 0:
            return [(0, 2 * tq)]
        return [(off, tq), (tq + off, 2 * tq)]

    def scores(t, kt):
        pieces = []
        for off in range(0, tq, ts):
            k = k_ref[0, kt * tq + off:kt * tq + off + ts, :]
            for l0, l1 in query_ranges(t, kt, off):
                st = _dot_nt(k, qq_ref[t, l0:l1, :])
                if kt == t:
                    k_pos = off + lax.broadcasted_iota(jnp.int32, st.shape, 0)
                    q_pos = (l0 + lax.broadcasted_iota(jnp.int32, st.shape, 1)) & (tq - 1)
                    st = jnp.where(k_pos <= q_pos, st, -jnp.inf)
                pieces.append((off, l0, l1, st))
        return pieces

    def update(pieces, kt):
        m_prev = m_ref[...]
        for _, l0, l1, st in pieces:
            m_ref[:, l0:l1] = jnp.maximum(m_ref[:, l0:l1], jnp.max(st, axis=0, keepdims=True))
        m_new = m_ref[...]
        alpha = jnp.exp2(m_prev - m_new)
        edges = sorted({e for _, l0, l1, _ in pieces for e in (l0, l1)})
        for a, b in zip(edges[:-1], edges[1:]):
            parts = [(off, st[:, a - l0:b - l0]) for off, l0, l1, st in pieces if l0 <= a and b <= l1]
            p = jnp.concatenate([jnp.exp2((st - m_new[:, a:b]).astype(BF16)) for _, st in parts], axis=0)
            vt = jnp.concatenate([vt_ref[0, kt, :, off:off + ts] for off, _ in parts], axis=1)
            vt = jnp.concatenate([vt, jnp.ones((ATTN_SUM_ROWS, vt.shape[1]), BF16)], axis=0)
            acc_ref[:, a:b] = alpha[:, a:b] * acc_ref[:, a:b] + _dot(vt, p)

    def finalize(t):
        o_all = acc_ref[0:d2, :] / acc_ref[d2:d2 + 1, :]
        o_t = o_all[:, 0:tq] - lam * o_all[:, tq:2 * tq]
        inv_rms = lax.rsqrt(jnp.mean(o_t * o_t, axis=0, keepdims=True) + SUBLN_EPS)
        o_ref[0, t * tq:(t + 1) * tq, :] = (o_t * inv_rms * out_gain).T.astype(o_ref.dtype)

    units = [(t, kt) for t in range(tiles) for kt in range(t + 1)]
    pending = {}
    for i in range(len(units) + ATTN_LOOKAHEAD):
        if i < len(units):
            pending[i] = scores(*units[i])
        j = i - ATTN_LOOKAHEAD
        if j >= 0:
            t, kt = units[j]
            if kt == 0:
                m_ref[...] = jnp.full_like(m_ref, -jnp.inf)
                acc_ref[...] = jnp.zeros_like(acc_ref)
            update(pending.pop(j), kt)
            if kt == t:
                finalize(t)


def _diff_attn(qk, v_t, lam_q1, lam_k1, lam_q2, lam_k2, g_norm, batch, seq, lambda_init):
    d_model = qk.shape[1] // 2
    d2 = d_model // DIFF_HEADS
    d = d2 // 2
    tq = ATTN_TILE
    tiles = seq // tq
    assert seq % tq == 0 and g_norm.shape[0] == d2 and v_t.shape == (batch, tiles, d_model, tq)
    qk3 = qk.reshape(batch, seq, 2 * d_model)
    return pl.pallas_call(
        functools.partial(_diff_attn_kernel, lambda_init=lambda_init),
        grid=(batch, DIFF_HEADS),
        in_specs=[_resident((1, d))] * 4 + [
            _resident((d2, 1)),
            pl.BlockSpec((1, seq, d2), lambda b, h: (b, 0, h)),
            pl.BlockSpec((1, seq, d2), lambda b, h: (b, 0, DIFF_HEADS + h)),
            pl.BlockSpec((1, tiles, d2, tq), lambda b, h: (b, 0, h, 0)),
        ],
        out_specs=pl.BlockSpec((1, seq, d2), lambda b, h: (b, 0, h)),
        out_shape=jax.ShapeDtypeStruct((batch, seq, d_model), BF16),
        scratch_shapes=[
            pltpu.VMEM((tiles, 2 * tq, d2), BF16),
            pltpu.VMEM((1, 2 * tq), F32),
            pltpu.VMEM((d2 + ATTN_SUM_ROWS, 2 * tq), F32),
        ],
        compiler_params=pltpu.CompilerParams(
            dimension_semantics=("parallel", "parallel"), vmem_limit_bytes=VMEM_LIMIT_BYTES),
        name="diff_attn",
    )(lam_q1.reshape(1, d), lam_k1.reshape(1, d), lam_q2.reshape(1, d), lam_k2.reshape(1, d),
      g_norm.reshape(d2, 1), qk3, qk3, v_t)


def _mix_out_ffn_kernel(x_ref, a_ref, wo_ref, gf_ref, wi_ref, w2_ref, gl_ref, o_ref, acc_ref, *, final_norm):
    d_ff = w2_ref.shape[0]
    x1 = x_ref[...] + _dot(a_ref[...], wo_ref[...])
    h = _rmsnorm_rows(x1, gf_ref[...], NORM_EPS).astype(BF16)
    acc_ref[...] = x1
    for f0 in range(0, d_ff, FFN_COL_TILE):
        gate = _dot(h, wi_ref[:, f0:f0 + FFN_COL_TILE])
        up = _dot(h, wi_ref[:, d_ff + f0:d_ff + f0 + FFN_COL_TILE])
        act = (gate * jax.nn.sigmoid(gate) * up).astype(BF16)
        acc_ref[...] += _dot(act, w2_ref[f0:f0 + FFN_COL_TILE, :])
    x2 = acc_ref[...]
    if final_norm:
        x2 = _rmsnorm_rows(x2, gl_ref[...], NORM_EPS)
    o_ref[...] = x2


def _mix_out_ffn(x2d, mix, w_out, layer, g_ffn_all, w_in_all, w2_all, g_last, final_norm):
    t, d = x2d.shape
    d_ff = w2_all.shape[1]
    assert t % ROW_TILE == 0 and d_ff % FFN_COL_TILE == 0
    row_spec = pl.BlockSpec((ROW_TILE, d), lambda i: (i, 0))
    return pl.pallas_call(
        functools.partial(_mix_out_ffn_kernel, final_norm=final_norm),
        grid=(t // ROW_TILE,),
        in_specs=[row_spec, row_spec, _resident((d, d)), _resident((1, d), layer),
                  _resident((d, 2 * d_ff), layer), _resident((d_ff, d), layer), _resident((1, d))],
        out_specs=row_spec,
        out_shape=jax.ShapeDtypeStruct((t, d), F32),
        scratch_shapes=[pltpu.VMEM((ROW_TILE, d), F32)],
        compiler_params=pltpu.CompilerParams(
            dimension_semantics=("parallel",), vmem_limit_bytes=VMEM_LIMIT_BYTES),
        name="mix_out_ffn",
    )(x2d, mix, w_out.astype(BF16), g_ffn_all.reshape(-1, 1, d), w_in_all, w2_all, g_last.reshape(1, d))


def kernel(x, gla_w_in, gla_w_gate_a, gla_w_gate_b, gla_b_gate, gla_norm, gla_w_out, diff_w_in, diff_lam_q1, diff_lam_k1, diff_lam_q2, diff_lam_k2, diff_norm, diff_w_out, norm_mixer, norm_ffn, ffn_w_in, ffn_w_out, norm_final):
    batch, seq, d = x.shape
    x2d = x.reshape(batch * seq, d)
    ffn_w_in_bf, ffn_w_out_bf = ffn_w_in.astype(BF16), ffn_w_out.astype(BF16)

    proj, gk = _norm_proj_gate(x2d, norm_mixer[0], gla_w_in[0],
                               gla_w_gate_a[0], gla_w_gate_b[0], gla_b_gate[0])
    mix = _gla_core(proj, gk, gla_norm[0], batch, seq).reshape(batch * seq, -1)
    x2d = _mix_out_ffn(x2d, mix, gla_w_out[0], 0, norm_ffn, ffn_w_in_bf, ffn_w_out_bf, norm_final,
                       final_norm=False)

    lambda_init = 0.8 - 0.6 * math.exp(-0.3 * 1)
    qk, v_t = _norm_proj_vt(x2d, norm_mixer[1], diff_w_in[0], 2 * d, batch, seq)
    mix = _diff_attn(qk, v_t, diff_lam_q1[0], diff_lam_k1[0], diff_lam_q2[0], diff_lam_k2[0],
                     diff_norm[0], batch, seq, lambda_init).reshape(batch * seq, -1)
    x2d = _mix_out_ffn(x2d, mix, diff_w_out[0], 1, norm_ffn, ffn_w_in_bf, ffn_w_out_bf, norm_final,
                       final_norm=True)
    return x2d.reshape(batch, seq, d)
```

```python
import functools
import math

import jax
import jax.numpy as jnp
from jax import lax
from jax.experimental import pallas as pl
from jax.experimental.pallas import tpu as pltpu

F32 = jnp.float32
BF16 = jnp.bfloat16

NORM_EPS = 1e-6
SUBLN_EPS = 1e-5

GLA_HEADS = 4
GLA_GATE_RANK = 16
GLA_TAU = 16.0
GLA_CHUNK = 64

DIFF_HEADS = 8

V7X_VMEM_BYTES = 64 * 1024 * 1024
VMEM_LIMIT_BYTES = V7X_VMEM_BYTES - 8 * 1024 * 1024

ROW_TILE = 512
PROJ_COL_TILE = 1024
FFN_COL_TILE = 256
GLA_GROUP = 256
ATTN_TILE = ROW_TILE
ATTN_SUB_TILE = 256
ATTN_SUM_ROWS = 16
ATTN_LOOKAHEAD = 3


def _resident(shape, layer=None):
    if layer is None:
        index, block = (0,) * len(shape), tuple(shape)
    else:
        index, block = (layer,) + (0,) * len(shape), (None,) + tuple(shape)
    return pl.BlockSpec(block, lambda *_: index, pipeline_mode=pl.Buffered(1))


def _rmsnorm_rows(x, gain, eps):
    return x * lax.rsqrt(jnp.mean(x * x, axis=-1, keepdims=True) + eps) * gain


def _dot(a, b):
    return jnp.dot(a, b, preferred_element_type=F32)


def _dot_nt(a, b):
    return lax.dot_general(a, b, (((1,), (1,)), ((), ())), preferred_element_type=F32)


def _dot_tn(a, b):
    return lax.dot_general(a, b, (((0,), (0,)), ((), ())), preferred_element_type=F32)


def _project(h, w_ref, o_ref):
    for c0 in range(0, o_ref.shape[1], PROJ_COL_TILE):
        cols = slice(c0, c0 + PROJ_COL_TILE)
        o_ref[:, cols] = _dot(h, w_ref[:, cols]).astype(o_ref.dtype)


def _norm_proj_gate_kernel(x_ref, g_ref, w_ref, wa_ref, wb_ref, bg_ref, o_ref, gk_ref):
    h = _rmsnorm_rows(x_ref[...], g_ref[...], NORM_EPS).astype(BF16)
    low = _dot(h, wa_ref[...])
    logits = _dot(low.astype(BF16), wb_ref[...]) + bg_ref[...]
    log_sig = jnp.minimum(logits, 0.0) - jnp.log(1.0 + jnp.exp(-jnp.abs(logits)))
    gk_ref[...] = log_sig * (math.log2(math.e) / GLA_TAU)
    _project(h, w_ref, o_ref)


def _norm_proj_vt_kernel(x_ref, g_ref, w_ref, wv_ref, o_ref, vt_ref, wvt_ref):
    @pl.when(pl.program_id(0) == 0)
    def _():
        wvt_ref[...] = wv_ref[...].T

    h = _rmsnorm_rows(x_ref[...], g_ref[...], NORM_EPS).astype(BF16)
    _project(h, w_ref, o_ref)
    vt_ref[0, 0] = _dot_nt(wvt_ref[...], h).astype(vt_ref.dtype)


def _norm_proj_call(kernel, name, x2d, gain, w_bf, n_out, extra_in, extra_specs, extra_out_shape,
                    extra_out_spec, scratch_shapes=(), semantics="parallel"):
    t, d = x2d.shape
    assert t % ROW_TILE == 0 and n_out % PROJ_COL_TILE == 0 and w_bf.shape[1] >= n_out
    return pl.pallas_call(
        kernel,
        grid=(t // ROW_TILE,),
        in_specs=[pl.BlockSpec((ROW_TILE, d), lambda i: (i, 0)), _resident((1, d)),
                  _resident((d, n_out))] + extra_specs,
        out_specs=[pl.BlockSpec((ROW_TILE, n_out), lambda i: (i, 0)), extra_out_spec],
        out_shape=[jax.ShapeDtypeStruct((t, n_out), BF16), extra_out_shape],
        scratch_shapes=list(scratch_shapes),
        compiler_params=pltpu.CompilerParams(
            dimension_semantics=(semantics,), vmem_limit_bytes=VMEM_LIMIT_BYTES),
        name=name,
    )(x2d, gain.reshape(1, d), w_bf, *extra_in)


def _norm_proj_gate(x2d, gain, w, w_a, w_b, b_g):
    t, d = x2d.shape
    rank, dk = w_b.shape
    return _norm_proj_call(
        _norm_proj_gate_kernel, "norm_proj_gate", x2d, gain, w.astype(BF16), w.shape[1],
        [w_a.astype(BF16), w_b.astype(BF16), b_g.reshape(1, dk)],
        [_resident((d, rank)), _resident((rank, dk)), _resident((1, dk))],
        jax.ShapeDtypeStruct((t, dk), F32), pl.BlockSpec((ROW_TILE, dk), lambda i: (i, 0)))


def _norm_proj_vt(x2d, gain, w, n_qk, batch, seq):
    t, d = x2d.shape
    d_v = w.shape[1] - n_qk
    assert n_qk % d_v == 0
    tiles = seq // ROW_TILE
    w_bf = w.astype(BF16)
    value_cols = pl.BlockSpec((d, d_v), lambda i: (0, n_qk // d_v), pipeline_mode=pl.Buffered(1))
    return _norm_proj_call(
        _norm_proj_vt_kernel, "norm_proj_vt", x2d, gain, w_bf, n_qk, [w_bf], [value_cols],
        jax.ShapeDtypeStruct((batch, tiles, d_v, ROW_TILE), BF16),
        pl.BlockSpec((1, 1, d_v, ROW_TILE), lambda i: (i // tiles, i % tiles, 0, 0)),
        scratch_shapes=[pltpu.VMEM((d_v, d), BF16)], semantics="arbitrary")


def _gla_kernel(q_ref, k_ref, v_ref, r_ref, gk_ref, gn_ref, o_ref):
    c = GLA_CHUNK
    grp = GLA_GROUP
    n = grp // c
    seq, hk = q_ref.shape[1], q_ref.shape[2]
    hv = v_ref.shape[2]
    shift = c.bit_length() - 1
    row = lax.broadcasted_iota(jnp.int32, (grp, grp), 0)
    col = lax.broadcasted_iota(jnp.int32, (grp, grp), 1)
    mask = (col <= row) & ((row >> shift) == (col >> shift))
    tri = jnp.where(mask, 1.0, 0.0).astype(BF16)
    gain = gn_ref[...]
    q_scale = hk ** -0.5

    def rows_of(gi):
        return slice(gi * grp, (gi + 1) * grp)

    def cum_decay(gi):
        g = gk_ref[0, rows_of(gi), :]
        g_hi = g.astype(BF16)
        g_lo = (g - g_hi.astype(F32)).astype(BF16)
        return _dot(tri, g_hi) + _dot(tri, g_lo)

    def scaled_operands(gi, b):
        b = b.reshape(n, c, hk)
        b_mid = b[:, c // 2:c // 2 + 1, :]
        b_last = b[:, c - 1:c, :]
        q = (q_ref[0, rows_of(gi), :].astype(F32) * q_scale).reshape(n, c, hk)
        k = k_ref[0, rows_of(gi), :].astype(F32).reshape(n, c, hk)
        q_mid = (q * jnp.exp2(b - b_mid)).reshape(grp, hk).astype(BF16)
        k_mid = (k * jnp.exp2(b_mid - b)).reshape(grp, hk).astype(BF16)
        q_dec = (q * jnp.exp2(b)).astype(BF16)
        k_end = (k * jnp.exp2(b_last - b)).astype(BF16)
        return q_mid, k_mid, q_dec, k_end, jnp.exp2(b_last)

    def state_part(gi, o_intra, q_dec, decay, incs, state_t):
        outs = []
        for ci in range(n):
            outs.append(o_intra[ci * c:(ci + 1) * c] + _dot_nt(q_dec[ci], state_t.astype(BF16)))
            state_t = state_t * decay[ci] + incs[ci]
        o = _rmsnorm_rows(jnp.concatenate(outs, axis=0), gain, NORM_EPS)
        r = r_ref[0, rows_of(gi), :].astype(F32)
        o_ref[0, rows_of(gi), :] = (o * (r * jax.nn.sigmoid(r))).astype(o_ref.dtype)
        return state_t

    groups = seq // grp
    state_t = jnp.zeros((hv, hk), F32)
    cum, ops = {}, {}
    for i in range(-2, groups):
        if i >= 0:
            q_mid, k_mid, q_dec, k_end, decay = ops.pop(i)
            v = v_ref[0, rows_of(i), :]
            attn = jnp.where(mask, _dot_nt(q_mid, k_mid), 0.0).astype(BF16)
            incs = [_dot_tn(v[ci * c:(ci + 1) * c], k_end[ci]) for ci in range(n)]
        if i + 2 < groups:
            cum[i + 2] = cum_decay(i + 2)
        if i >= 0:
            o_intra = _dot(attn, v)
        if 0 <= i + 1 < groups:
            ops[i + 1] = scaled_operands(i + 1, cum.pop(i + 1))
        if i >= 0:
            state_t = state_part(i, o_intra, q_dec, decay, incs, state_t)


def _gla_core(proj, gk, g_norm, batch, seq):
    dk = gk.shape[1]
    hk = dk // GLA_HEADS
    hv = g_norm.shape[0]
    dv = hv * GLA_HEADS
    assert proj.shape[1] == 2 * dk + 2 * dv and seq % GLA_GROUP == 0 and GLA_GROUP % GLA_CHUNK == 0
    proj3 = proj.reshape(batch, seq, proj.shape[1])
    gk3 = gk.reshape(batch, seq, dk)
    k_off, v_off, r_off = dk // hk, (2 * dk) // hv, (2 * dk + dv) // hv
    return pl.pallas_call(
        _gla_kernel,
        grid=(batch, GLA_HEADS),
        in_specs=[
            pl.BlockSpec((1, seq, hk), lambda b, h: (b, 0, h)),
            pl.BlockSpec((1, seq, hk), lambda b, h: (b, 0, k_off + h)),
            pl.BlockSpec((1, seq, hv), lambda b, h: (b, 0, v_off + h)),
            pl.BlockSpec((1, seq, hv), lambda b, h: (b, 0, r_off + h)),
            pl.BlockSpec((1, seq, hk), lambda b, h: (b, 0, h)),
            _resident((1, hv)),
        ],
        out_specs=pl.BlockSpec((1, seq, hv), lambda b, h: (b, 0, h)),
        out_shape=jax.ShapeDtypeStruct((batch, seq, dv), BF16),
        compiler_params=pltpu.CompilerParams(
            dimension_semantics=("parallel", "parallel"), vmem_limit_bytes=VMEM_LIMIT_BYTES),
        name="gla_core",
    )(proj3, proj3, proj3, proj3, gk3, g_norm.reshape(1, hv))


def _diff_attn_kernel(lq1_ref, lk1_ref, lq2_ref, lk2_ref, gn_ref, q_ref, k_ref, vt_ref, o_ref,
                      qq_ref, m_ref, acc_ref, *, lambda_init):
    tiles, d2, tq = vt_ref.shape[1], vt_ref.shape[2], vt_ref.shape[3]
    d = d2 // 2
    ts = ATTN_SUB_TILE
    assert tq % ts == 0

    lane = lax.broadcasted_iota(jnp.int32, (tq, d2), 1)
    for t in range(tiles):
        q = q_ref[0, t * tq:(t + 1) * tq, :].astype(F32) * (d ** -0.5 * math.log2(math.e))
        qq_ref[t, 0:tq, :] = jnp.where(lane < d, q, 0.0).astype(BF16)
        qq_ref[t, tq:2 * tq, :] = jnp.where(lane >= d, q, 0.0).astype(BF16)
    lam = (jnp.exp(jnp.sum(lq1_ref[...] * lk1_ref[...], keepdims=True))
           - jnp.exp(jnp.sum(lq2_ref[...] * lk2_ref[...], keepdims=True)) + lambda_init)
    out_gain = gn_ref[...] * (1.0 - lambda_init)

    def query_ranges(t, kt, off):
        if kt < t or off == 0:
            return [(0, 2 * tq)]
        return [(off, tq), (tq + off, 2 * tq)]

    def scores(t, kt):
        pieces = []
        for off in range(0, tq, ts):
            k = k_ref[0, kt * tq + off:kt * tq + off + ts, :]
            for l0, l1 in query_ranges(t, kt, off):
                st = _dot_nt(k, qq_ref[t, l0:l1, :])
                if kt == t:
                    k_pos = off + lax.broadcasted_iota(jnp.int32, st.shape, 0)
                    q_pos = (l0 + lax.broadcasted_iota(jnp.int32, st.shape, 1)) & (tq - 1)
                    st = jnp.where(k_pos <= q_pos, st, -jnp.inf)
                pieces.append((off, l0, l1, st))
        return pieces

    def update(pieces, kt):
        m_prev = m_ref[...]
        for _, l0, l1, st in pieces:
            m_ref[:, l0:l1] = jnp.maximum(m_ref[:, l0:l1], jnp.max(st, axis=0, keepdims=True))
        m_new = m_ref[...]
        alpha = jnp.exp2(m_prev - m_new)
        edges = sorted({e for _, l0, l1, _ in pieces for e in (l0, l1)})
        for a, b in zip(edges[:-1], edges[1:]):
            parts = [(off, st[:, a - l0:b - l0]) for off, l0, l1, st in pieces if l0 <= a and b <= l1]
            p = jnp.concatenate([jnp.exp2(st - m_new[:, a:b]).astype(BF16) for _, st in parts], axis=0)
            vt = jnp.concatenate([vt_ref[0, kt, :, off:off + ts] for off, _ in parts], axis=1)
            vt = jnp.concatenate([vt, jnp.ones((ATTN_SUM_ROWS, vt.shape[1]), BF16)], axis=0)
            acc_ref[:, a:b] = alpha[:, a:b] * acc_ref[:, a:b] + _dot(vt, p)

    def finalize(t):
        o_all = acc_ref[0:d2, :] / acc_ref[d2:d2 + 1, :]
        o_t = o_all[:, 0:tq] - lam * o_all[:, tq:2 * tq]
        inv_rms = lax.rsqrt(jnp.mean(o_t * o_t, axis=0, keepdims=True) + SUBLN_EPS)
        o_ref[0, t * tq:(t + 1) * tq, :] = (o_t * inv_rms * out_gain).T.astype(o_ref.dtype)

    units = [(t, kt) for t in range(tiles) for kt in range(t + 1)]
    pending = {}
    for i in range(len(units) + ATTN_LOOKAHEAD):
        if i < len(units):
            pending[i] = scores(*units[i])
        j = i - ATTN_LOOKAHEAD
        if j >= 0:
            t, kt = units[j]
            if kt == 0:
                m_ref[...] = jnp.full_like(m_ref, -jnp.inf)
                acc_ref[...] = jnp.zeros_like(acc_ref)
            update(pending.pop(j), kt)
            if kt == t:
                finalize(t)


def _diff_attn(qk, v_t, lam_q1, lam_k1, lam_q2, lam_k2, g_norm, batch, seq, lambda_init):
    d_model = qk.shape[1] // 2
    d2 = d_model // DIFF_HEADS
    d = d2 // 2
    tq = ATTN_TILE
    tiles = seq // tq
    assert seq % tq == 0 and g_norm.shape[0] == d2 and v_t.shape == (batch, tiles, d_model, tq)
    qk3 = qk.reshape(batch, seq, 2 * d_model)
    return pl.pallas_call(
        functools.partial(_diff_attn_kernel, lambda_init=lambda_init),
        grid=(batch, DIFF_HEADS),
        in_specs=[_resident((1, d))] * 4 + [
            _resident((d2, 1)),
            pl.BlockSpec((1, seq, d2), lambda b, h: (b, 0, h)),
            pl.BlockSpec((1, seq, d2), lambda b, h: (b, 0, DIFF_HEADS + h)),
            pl.BlockSpec((1, tiles, d2, tq), lambda b, h: (b, 0, h, 0)),
        ],
        out_specs=pl.BlockSpec((1, seq, d2), lambda b, h: (b, 0, h)),
        out_shape=jax.ShapeDtypeStruct((batch, seq, d_model), BF16),
        scratch_shapes=[
            pltpu.VMEM((tiles, 2 * tq, d2), BF16),
            pltpu.VMEM((1, 2 * tq), F32),
            pltpu.VMEM((d2 + ATTN_SUM_ROWS, 2 * tq), F32),
        ],
        compiler_params=pltpu.CompilerParams(
            dimension_semantics=("parallel", "parallel"), vmem_limit_bytes=VMEM_LIMIT_BYTES),
        name="diff_attn",
    )(lam_q1.reshape(1, d), lam_k1.reshape(1, d), lam_q2.reshape(1, d), lam_k2.reshape(1, d),
      g_norm.reshape(d2, 1), qk3, qk3, v_t)


def _mix_out_ffn_kernel(x_ref, a_ref, wo_ref, gf_ref, wi_ref, w2_ref, gl_ref, o_ref, acc_ref, *, final_norm):
    d_ff = w2_ref.shape[0]
    x1 = x_ref[...] + _dot(a_ref[...], wo_ref[...])
    h = _rmsnorm_rows(x1, gf_ref[...], NORM_EPS).astype(BF16)
    acc_ref[...] = x1
    for f0 in range(0, d_ff, FFN_COL_TILE):
        gate = _dot(h, wi_ref[:, f0:f0 + FFN_COL_TILE])
        up = _dot(h, wi_ref[:, d_ff + f0:d_ff + f0 + FFN_COL_TILE])
        act = (gate * jax.nn.sigmoid(gate) * up).astype(BF16)
        acc_ref[...] += _dot(act, w2_ref[f0:f0 + FFN_COL_TILE, :])
    x2 = acc_ref[...]
    if final_norm:
        x2 = _rmsnorm_rows(x2, gl_ref[...], NORM_EPS)
    o_ref[...] = x2


def _mix_out_ffn(x2d, mix, w_out, layer, g_ffn_all, w_in_all, w2_all, g_last, final_norm):
    t, d = x2d.shape
    d_ff = w2_all.shape[1]
    assert t % ROW_TILE == 0 and d_ff % FFN_COL_TILE == 0
    row_spec = pl.BlockSpec((ROW_TILE, d), lambda i: (i, 0))
    return pl.pallas_call(
        functools.partial(_mix_out_ffn_kernel, final_norm=final_norm),
        grid=(t // ROW_TILE,),
        in_specs=[row_spec, row_spec, _resident((d, d)), _resident((1, d), layer),
                  _resident((d, 2 * d_ff), layer), _resident((d_ff, d), layer), _resident((1, d))],
        out_specs=row_spec,
        out_shape=jax.ShapeDtypeStruct((t, d), F32),
        scratch_shapes=[pltpu.VMEM((ROW_TILE, d), F32)],
        compiler_params=pltpu.CompilerParams(
            dimension_semantics=("parallel",), vmem_limit_bytes=VMEM_LIMIT_BYTES),
        name="mix_out_ffn",
    )(x2d, mix, w_out.astype(BF16), g_ffn_all.reshape(-1, 1, d), w_in_all, w2_all, g_last.reshape(1, d))


def kernel(x, gla_w_in, gla_w_gate_a, gla_w_gate_b, gla_b_gate, gla_norm, gla_w_out, diff_w_in, diff_lam_q1, diff_lam_k1, diff_lam_q2, diff_lam_k2, diff_norm, diff_w_out, norm_mixer, norm_ffn, ffn_w_in, ffn_w_out, norm_final):
    batch, seq, d = x.shape
    x2d = x.reshape(batch * seq, d)
    ffn_w_in_bf, ffn_w_out_bf = ffn_w_in.astype(BF16), ffn_w_out.astype(BF16)

    proj, gk = _norm_proj_gate(x2d, norm_mixer[0], gla_w_in[0],
                               gla_w_gate_a[0], gla_w_gate_b[0], gla_b_gate[0])
    mix = _gla_core(proj, gk, gla_norm[0], batch, seq).reshape(batch * seq, -1)
    x2d = _mix_out_ffn(x2d, mix, gla_w_out[0], 0, norm_ffn, ffn_w_in_bf, ffn_w_out_bf, norm_final,
                       final_norm=False)

    lambda_init = 0.8 - 0.6 * math.exp(-0.3 * 1)
    qk, v_t = _norm_proj_vt(x2d, norm_mixer[1], diff_w_in[0], 2 * d, batch, seq)
    mix = _diff_attn(qk, v_t, diff_lam_q1[0], diff_lam_k1[0], diff_lam_q2[0], diff_lam_k2[0],
                     diff_norm[0], batch, seq, lambda_init).reshape(batch * seq, -1)
    x2d = _mix_out_ffn(x2d, mix, diff_w_out[0], 1, norm_ffn, ffn_w_in_bf, ffn_w_out_bf, norm_final,
                       final_norm=True)
    return x2d.reshape(batch, seq, d)
```

```python
import functools
import math

import jax
import jax.numpy as jnp
from jax import lax
from jax.experimental import pallas as pl
from jax.experimental.pallas import tpu as pltpu

F32 = jnp.float32
BF16 = jnp.bfloat16

NORM_EPS = 1e-6
SUBLN_EPS = 1e-5

GLA_HEADS = 4
GLA_GATE_RANK = 16
GLA_TAU = 16.0
GLA_CHUNK = 64

DIFF_HEADS = 8

V7X_VMEM_BYTES = 64 * 1024 * 1024
VMEM_LIMIT_BYTES = V7X_VMEM_BYTES - 8 * 1024 * 1024

ROW_TILE = 512
PROJ_COL_TILE = 1024
FFN_COL_TILE = 256
GLA_GROUP = 256
ATTN_TILE = ROW_TILE
ATTN_SUB_TILE = 256
ATTN_SUM_ROWS = 16
ATTN_LOOKAHEAD = 2


def _resident(shape, layer=None):
    if layer is None:
        index, block = (0,) * len(shape), tuple(shape)
    else:
        index, block = (layer,) + (0,) * len(shape), (None,) + tuple(shape)
    return pl.BlockSpec(block, lambda *_: index, pipeline_mode=pl.Buffered(1))


def _rmsnorm_rows(x, gain, eps):
    return x * lax.rsqrt(jnp.mean(x * x, axis=-1, keepdims=True) + eps) * gain


def _dot(a, b):
    return jnp.dot(a, b, preferred_element_type=F32)


def _dot_nt(a, b):
    return lax.dot_general(a, b, (((1,), (1,)), ((), ())), preferred_element_type=F32)


def _dot_tn(a, b):
    return lax.dot_general(a, b, (((0,), (0,)), ((), ())), preferred_element_type=F32)


def _project(h, w_ref, o_ref):
    for c0 in range(0, o_ref.shape[1], PROJ_COL_TILE):
        cols = slice(c0, c0 + PROJ_COL_TILE)
        o_ref[:, cols] = _dot(h, w_ref[:, cols]).astype(o_ref.dtype)


def _norm_proj_gate_kernel(x_ref, g_ref, w_ref, wa_ref, wb_ref, bg_ref, o_ref, gk_ref):
    h = _rmsnorm_rows(x_ref[...], g_ref[...], NORM_EPS).astype(BF16)
    low = _dot(h, wa_ref[...])
    logits = _dot(low.astype(BF16), wb_ref[...]) + bg_ref[...]
    log_sig = jnp.minimum(logits, 0.0) - jnp.log(1.0 + jnp.exp(-jnp.abs(logits)))
    gk_ref[...] = log_sig * (math.log2(math.e) / GLA_TAU)
    _project(h, w_ref, o_ref)


def _norm_proj_vt_kernel(x_ref, g_ref, w_ref, wv_ref, o_ref, vt_ref, wvt_ref):
    @pl.when(pl.program_id(0) == 0)
    def _():
        wvt_ref[...] = wv_ref[...].T

    h = _rmsnorm_rows(x_ref[...], g_ref[...], NORM_EPS).astype(BF16)
    _project(h, w_ref, o_ref)
    vt_ref[0, 0] = _dot_nt(wvt_ref[...], h).astype(vt_ref.dtype)


def _norm_proj_call(kernel, name, x2d, gain, w_bf, n_out, extra_in, extra_specs, extra_out_shape,
                    extra_out_spec, scratch_shapes=(), semantics="parallel"):
    t, d = x2d.shape
    assert t % ROW_TILE == 0 and n_out % PROJ_COL_TILE == 0 and w_bf.shape[1] >= n_out
    return pl.pallas_call(
        kernel,
        grid=(t // ROW_TILE,),
        in_specs=[pl.BlockSpec((ROW_TILE, d), lambda i: (i, 0)), _resident((1, d)),
                  _resident((d, n_out))] + extra_specs,
        out_specs=[pl.BlockSpec((ROW_TILE, n_out), lambda i: (i, 0)), extra_out_spec],
        out_shape=[jax.ShapeDtypeStruct((t, n_out), BF16), extra_out_shape],
        scratch_shapes=list(scratch_shapes),
        compiler_params=pltpu.CompilerParams(
            dimension_semantics=(semantics,), vmem_limit_bytes=VMEM_LIMIT_BYTES),
        name=name,
    )(x2d, gain.reshape(1, d), w_bf, *extra_in)


def _norm_proj_gate(x2d, gain, w, w_a, w_b, b_g):
    t, d = x2d.shape
    rank, dk = w_b.shape
    return _norm_proj_call(
        _norm_proj_gate_kernel, "norm_proj_gate", x2d, gain, w.astype(BF16), w.shape[1],
        [w_a.astype(BF16), w_b.astype(BF16), b_g.reshape(1, dk)],
        [_resident((d, rank)), _resident((rank, dk)), _resident((1, dk))],
        jax.ShapeDtypeStruct((t, dk), F32), pl.BlockSpec((ROW_TILE, dk), lambda i: (i, 0)))


def _norm_proj_vt(x2d, gain, w, n_qk, batch, seq):
    t, d = x2d.shape
    d_v = w.shape[1] - n_qk
    assert n_qk % d_v == 0
    tiles = seq // ROW_TILE
    w_bf = w.astype(BF16)
    value_cols = pl.BlockSpec((d, d_v), lambda i: (0, n_qk // d_v), pipeline_mode=pl.Buffered(1))
    return _norm_proj_call(
        _norm_proj_vt_kernel, "norm_proj_vt", x2d, gain, w_bf, n_qk, [w_bf], [value_cols],
        jax.ShapeDtypeStruct((batch, tiles, d_v, ROW_TILE), BF16),
        pl.BlockSpec((1, 1, d_v, ROW_TILE), lambda i: (i // tiles, i % tiles, 0, 0)),
        scratch_shapes=[pltpu.VMEM((d_v, d), BF16)], semantics="arbitrary")


def _gla_kernel(q_ref, k_ref, v_ref, r_ref, gk_ref, gn_ref, o_ref):
    c = GLA_CHUNK
    grp = GLA_GROUP
    n = grp // c
    seq, hk = q_ref.shape[1], q_ref.shape[2]
    hv = v_ref.shape[2]
    shift = c.bit_length() - 1
    row = lax.broadcasted_iota(jnp.int32, (grp, grp), 0)
    col = lax.broadcasted_iota(jnp.int32, (grp, grp), 1)
    mask = (col <= row) & ((row >> shift) == (col >> shift))
    tri = jnp.where(mask, 1.0, 0.0).astype(BF16)
    gain = gn_ref[...]
    q_scale = hk ** -0.5

    def rows_of(gi):
        return slice(gi * grp, (gi + 1) * grp)

    def cum_decay(gi):
        g = gk_ref[0, rows_of(gi), :]
        g_hi = g.astype(BF16)
        g_lo = (g - g_hi.astype(F32)).astype(BF16)
        return _dot(tri, g_hi) + _dot(tri, g_lo)

    def scaled_operands(gi, b):
        b = b.reshape(n, c, hk)
        b_mid = b[:, c // 2:c // 2 + 1, :]
        b_last = b[:, c - 1:c, :]
        q = (q_ref[0, rows_of(gi), :].astype(F32) * q_scale).reshape(n, c, hk)
        k = k_ref[0, rows_of(gi), :].astype(F32).reshape(n, c, hk)
        q_mid = (q * jnp.exp2(b - b_mid)).reshape(grp, hk).astype(BF16)
        k_mid = (k * jnp.exp2(b_mid - b)).reshape(grp, hk).astype(BF16)
        q_dec = (q * jnp.exp2(b)).astype(BF16)
        k_end = (k * jnp.exp2(b_last - b)).astype(BF16)
        return q_mid, k_mid, q_dec, k_end, jnp.exp2(b_last)

    def state_part(gi, o_intra, q_dec, decay, incs, state_t):
        outs = []
        for ci in range(n):
            outs.append(o_intra[ci * c:(ci + 1) * c] + _dot_nt(q_dec[ci], state_t.astype(BF16)))
            state_t = state_t * decay[ci] + incs[ci]
        o = _rmsnorm_rows(jnp.concatenate(outs, axis=0), gain, NORM_EPS)
        r = r_ref[0, rows_of(gi), :].astype(F32)
        o_ref[0, rows_of(gi), :] = (o * (r * jax.nn.sigmoid(r))).astype(o_ref.dtype)
        return state_t

    groups = seq // grp
    state_t = jnp.zeros((hv, hk), F32)
    cum, ops = {}, {}
    for i in range(-2, groups):
        if i >= 0:
            q_mid, k_mid, q_dec, k_end, decay = ops.pop(i)
            v = v_ref[0, rows_of(i), :]
            attn = jnp.where(mask, _dot_nt(q_mid, k_mid), 0.0).astype(BF16)
            incs = [_dot_tn(v[ci * c:(ci + 1) * c], k_end[ci]) for ci in range(n)]
        if i + 2 < groups:
            cum[i + 2] = cum_decay(i + 2)
        if i >= 0:
            o_intra = _dot(attn, v)
        if 0 <= i + 1 < groups:
            ops[i + 1] = scaled_operands(i + 1, cum.pop(i + 1))
        if i >= 0:
            state_t = state_part(i, o_intra, q_dec, decay, incs, state_t)


def _gla_core(proj, gk, g_norm, batch, seq):
    dk = gk.shape[1]
    hk = dk // GLA_HEADS
    hv = g_norm.shape[0]
    dv = hv * GLA_HEADS
    assert proj.shape[1] == 2 * dk + 2 * dv and seq % GLA_GROUP == 0 and GLA_GROUP % GLA_CHUNK == 0
    proj3 = proj.reshape(batch, seq, proj.shape[1])
    gk3 = gk.reshape(batch, seq, dk)
    k_off, v_off, r_off = dk // hk, (2 * dk) // hv, (2 * dk + dv) // hv
    return pl.pallas_call(
        _gla_kernel,
        grid=(batch, GLA_HEADS),
        in_specs=[
            pl.BlockSpec((1, seq, hk), lambda b, h: (b, 0, h)),
            pl.BlockSpec((1, seq, hk), lambda b, h: (b, 0, k_off + h)),
            pl.BlockSpec((1, seq, hv), lambda b, h: (b, 0, v_off + h)),
            pl.BlockSpec((1, seq, hv), lambda b, h: (b, 0, r_off + h)),
            pl.BlockSpec((1, seq, hk), lambda b, h: (b, 0, h)),
            _resident((1, hv)),
        ],
        out_specs=pl.BlockSpec((1, seq, hv), lambda b, h: (b, 0, h)),
        out_shape=jax.ShapeDtypeStruct((batch, seq, dv), BF16),
        compiler_params=pltpu.CompilerParams(
            dimension_semantics=("parallel", "parallel"), vmem_limit_bytes=VMEM_LIMIT_BYTES),
        name="gla_core",
    )(proj3, proj3, proj3, proj3, gk3, g_norm.reshape(1, hv))


def _diff_attn_kernel(lq1_ref, lk1_ref, lq2_ref, lk2_ref, gn_ref, q_ref, k_ref, vt_ref, o_ref,
                      qq_ref, m_ref, acc_ref, *, lambda_init):
    tiles, d2, tq = vt_ref.shape[1], vt_ref.shape[2], vt_ref.shape[3]
    d = d2 // 2
    ts = ATTN_SUB_TILE
    assert tq % ts == 0

    lane = lax.broadcasted_iota(jnp.int32, (tq, d2), 1)
    for t in range(tiles):
        q = q_ref[0, t * tq:(t + 1) * tq, :].astype(F32) * (d ** -0.5 * math.log2(math.e))
        qq_ref[t, 0:tq, :] = jnp.where(lane < d, q, 0.0).astype(BF16)
        qq_ref[t, tq:2 * tq, :] = jnp.where(lane >= d, q, 0.0).astype(BF16)
    lam = (jnp.exp(jnp.sum(lq1_ref[...] * lk1_ref[...], keepdims=True))
           - jnp.exp(jnp.sum(lq2_ref[...] * lk2_ref[...], keepdims=True)) + lambda_init)
    out_gain = gn_ref[...] * (1.0 - lambda_init)

    def query_ranges(t, kt, off):
        if kt < t or off == 0:
            return [(0, 2 * tq)]
        return [(off, tq), (tq + off, 2 * tq)]

    def scores(t, kt):
        nk = ts if kt == t else tq
        pieces = []
        for off in range(0, tq, nk):
            k = k_ref[0, kt * tq + off:kt * tq + off + nk, :]
            for l0, l1 in query_ranges(t, kt, off):
                st = _dot_nt(k, qq_ref[t, l0:l1, :])
                if kt == t:
                    k_pos = off + lax.broadcasted_iota(jnp.int32, st.shape, 0)
                    q_pos = (l0 + lax.broadcasted_iota(jnp.int32, st.shape, 1)) & (tq - 1)
                    st = jnp.where(k_pos <= q_pos, st, -jnp.inf)
                pieces.append((off, nk, l0, l1, st))
        return pieces

    def update(pieces, kt):
        m_prev = m_ref[...]
        for _, _, l0, l1, st in pieces:
            m_ref[:, l0:l1] = jnp.maximum(m_ref[:, l0:l1], jnp.max(st, axis=0, keepdims=True))
        m_new = m_ref[...]
        alpha = jnp.exp2(m_prev - m_new)
        edges = sorted({e for _, _, l0, l1, _ in pieces for e in (l0, l1)})
        for a, b in zip(edges[:-1], edges[1:]):
            parts = [(off, nk, st[:, a - l0:b - l0]) for off, nk, l0, l1, st in pieces
                     if l0 <= a and b <= l1]
            p = jnp.concatenate([jnp.exp2(st - m_new[:, a:b]).astype(BF16) for _, _, st in parts], axis=0)
            vt = jnp.concatenate([vt_ref[0, kt, :, off:off + nk] for off, nk, _ in parts], axis=1)
            vt = jnp.concatenate([vt, jnp.ones((ATTN_SUM_ROWS, vt.shape[1]), BF16)], axis=0)
            acc_ref[:, a:b] = alpha[:, a:b] * acc_ref[:, a:b] + _dot(vt, p)

    def finalize(t):
        o_all = acc_ref[0:d2, :] / acc_ref[d2:d2 + 1, :]
        o_t = o_all[:, 0:tq] - lam * o_all[:, tq:2 * tq]
        inv_rms = lax.rsqrt(jnp.mean(o_t * o_t, axis=0, keepdims=True) + SUBLN_EPS)
        o_ref[0, t * tq:(t + 1) * tq, :] = (o_t * inv_rms * out_gain).T.astype(o_ref.dtype)

    units = [(t, kt) for t in range(tiles) for kt in range(t + 1)]
    pending = {}
    for i in range(len(units) + ATTN_LOOKAHEAD):
        if i < len(units):
            pending[i] = scores(*units[i])
        j = i - ATTN_LOOKAHEAD
        if j >= 0:
            t, kt = units[j]
            if kt == 0:
                m_ref[...] = jnp.full_like(m_ref, -jnp.inf)
                acc_ref[...] = jnp.zeros_like(acc_ref)
            update(pending.pop(j), kt)
            if kt == t:
                finalize(t)


def _diff_attn(qk, v_t, lam_q1, lam_k1, lam_q2, lam_k2, g_norm, batch, seq, lambda_init):
    d_model = qk.shape[1] // 2
    d2 = d_model // DIFF_HEADS
    d = d2 // 2
    tq = ATTN_TILE
    tiles = seq // tq
    assert seq % tq == 0 and g_norm.shape[0] == d2 and v_t.shape == (batch, tiles, d_model, tq)
    qk3 = qk.reshape(batch, seq, 2 * d_model)
    return pl.pallas_call(
        functools.partial(_diff_attn_kernel, lambda_init=lambda_init),
        grid=(batch, DIFF_HEADS),
        in_specs=[_resident((1, d))] * 4 + [
            _resident((d2, 1)),
            pl.BlockSpec((1, seq, d2), lambda b, h: (b, 0, h)),
            pl.BlockSpec((1, seq, d2), lambda b, h: (b, 0, DIFF_HEADS + h)),
            pl.BlockSpec((1, tiles, d2, tq), lambda b, h: (b, 0, h, 0)),
        ],
        out_specs=pl.BlockSpec((1, seq, d2), lambda b, h: (b, 0, h)),
        out_shape=jax.ShapeDtypeStruct((batch, seq, d_model), BF16),
        scratch_shapes=[
            pltpu.VMEM((tiles, 2 * tq, d2), BF16),
            pltpu.VMEM((1, 2 * tq), F32),
            pltpu.VMEM((d2 + ATTN_SUM_ROWS, 2 * tq), F32),
        ],
        compiler_params=pltpu.CompilerParams(
            dimension_semantics=("parallel", "parallel"), vmem_limit_bytes=VMEM_LIMIT_BYTES),
        name="diff_attn",
    )(lam_q1.reshape(1, d), lam_k1.reshape(1, d), lam_q2.reshape(1, d), lam_k2.reshape(1, d),
      g_norm.reshape(d2, 1), qk3, qk3, v_t)


def _mix_out_ffn_kernel(x_ref, a_ref, wo_ref, gf_ref, wi_ref, w2_ref, gl_ref, o_ref, acc_ref, *, final_norm):
    d_ff = w2_ref.shape[0]
    x1 = x_ref[...] + _dot(a_ref[...], wo_ref[...])
    h = _rmsnorm_rows(x1, gf_ref[...], NORM_EPS).astype(BF16)
    acc_ref[...] = x1
    for f0 in range(0, d_ff, FFN_COL_TILE):
        gate = _dot(h, wi_ref[:, f0:f0 + FFN_COL_TILE])
        up = _dot(h, wi_ref[:, d_ff + f0:d_ff + f0 + FFN_COL_TILE])
        act = (gate * jax.nn.sigmoid(gate) * up).astype(BF16)
        acc_ref[...] += _dot(act, w2_ref[f0:f0 + FFN_COL_TILE, :])
    x2 = acc_ref[...]
    if final_norm:
        x2 = _rmsnorm_rows(x2, gl_ref[...], NORM_EPS)
    o_ref[...] = x2


def _mix_out_ffn(x2d, mix, w_out, layer, g_ffn_all, w_in_all, w2_all, g_last, final_norm):
    t, d = x2d.shape
    d_ff = w2_all.shape[1]
    assert t % ROW_TILE == 0 and d_ff % FFN_COL_TILE == 0
    row_spec = pl.BlockSpec((ROW_TILE, d), lambda i: (i, 0))
    return pl.pallas_call(
        functools.partial(_mix_out_ffn_kernel, final_norm=final_norm),
        grid=(t // ROW_TILE,),
        in_specs=[row_spec, row_spec, _resident((d, d)), _resident((1, d), layer),
                  _resident((d, 2 * d_ff), layer), _resident((d_ff, d), layer), _resident((1, d))],
        out_specs=row_spec,
        out_shape=jax.ShapeDtypeStruct((t, d), F32),
        scratch_shapes=[pltpu.VMEM((ROW_TILE, d), F32)],
        compiler_params=pltpu.CompilerParams(
            dimension_semantics=("parallel",), vmem_limit_bytes=VMEM_LIMIT_BYTES),
        name="mix_out_ffn",
    )(x2d, mix, w_out.astype(BF16), g_ffn_all.reshape(-1, 1, d), w_in_all, w2_all, g_last.reshape(1, d))


def kernel(x, gla_w_in, gla_w_gate_a, gla_w_gate_b, gla_b_gate, gla_norm, gla_w_out, diff_w_in, diff_lam_q1, diff_lam_k1, diff_lam_q2, diff_lam_k2, diff_norm, diff_w_out, norm_mixer, norm_ffn, ffn_w_in, ffn_w_out, norm_final):
    batch, seq, d = x.shape
    x2d = x.reshape(batch * seq, d)
    ffn_w_in_bf, ffn_w_out_bf = ffn_w_in.astype(BF16), ffn_w_out.astype(BF16)

    proj, gk = _norm_proj_gate(x2d, norm_mixer[0], gla_w_in[0],
                               gla_w_gate_a[0], gla_w_gate_b[0], gla_b_gate[0])
    mix = _gla_core(proj, gk, gla_norm[0], batch, seq).reshape(batch * seq, -1)
    x2d = _mix_out_ffn(x2d, mix, gla_w_out[0], 0, norm_ffn, ffn_w_in_bf, ffn_w_out_bf, norm_final,
                       final_norm=False)

    lambda_init = 0.8 - 0.6 * math.exp(-0.3 * 1)
    qk, v_t = _norm_proj_vt(x2d, norm_mixer[1], diff_w_in[0], 2 * d, batch, seq)
    mix = _diff_attn(qk, v_t, diff_lam_q1[0], diff_lam_k1[0], diff_lam_q2[0], diff_lam_k2[0],
                     diff_norm[0], batch, seq, lambda_init).reshape(batch * seq, -1)
    x2d = _mix_out_ffn(x2d, mix, diff_w_out[0], 1, norm_ffn, ffn_w_in_bf, ffn_w_out_bf, norm_final,
                       final_norm=True)
    return x2d.reshape(batch, seq, d)
```

```python
import functools
import math

import jax
import jax.numpy as jnp
from jax import lax
from jax.experimental import pallas as pl
from jax.experimental.pallas import tpu as pltpu

F32 = jnp.float32
BF16 = jnp.bfloat16

NORM_EPS = 1e-6
SUBLN_EPS = 1e-5

GLA_HEADS = 4
GLA_GATE_RANK = 16
GLA_TAU = 16.0
GLA_CHUNK = 64

DIFF_HEADS = 8

V7X_VMEM_BYTES = 64 * 1024 * 1024
VMEM_LIMIT_BYTES = V7X_VMEM_BYTES - 8 * 1024 * 1024

PROJ_ROW_TILE = 1024
FFN_ROW_TILE = 1024
PROJ_COL_TILE = 1024
FFN_COL_TILE = 256
GLA_GROUP = 256
ATTN_TILE = 512
ATTN_SUB_TILE = 256
ATTN_SUM_ROWS = 16
ATTN_LOOKAHEAD = 2


def _resident(shape, layer=None):
    if layer is None:
        index, block = (0,) * len(shape), tuple(shape)
    else:
        index, block = (layer,) + (0,) * len(shape), (None,) + tuple(shape)
    return pl.BlockSpec(block, lambda *_: index, pipeline_mode=pl.Buffered(1))


def _rmsnorm_rows(x, gain, eps):
    return x * lax.rsqrt(jnp.mean(x * x, axis=-1, keepdims=True) + eps) * gain


def _dot(a, b):
    return jnp.dot(a, b, preferred_element_type=F32)


def _dot_nt(a, b):
    return lax.dot_general(a, b, (((1,), (1,)), ((), ())), preferred_element_type=F32)


def _dot_tn(a, b):
    return lax.dot_general(a, b, (((0,), (0,)), ((), ())), preferred_element_type=F32)


def _project(h, w_ref, o_ref):
    for c0 in range(0, o_ref.shape[1], PROJ_COL_TILE):
        cols = slice(c0, c0 + PROJ_COL_TILE)
        o_ref[:, cols] = _dot(h, w_ref[:, cols]).astype(o_ref.dtype)


def _norm_proj_gate_kernel(x_ref, g_ref, w_ref, wa_ref, wb_ref, bg_ref, o_ref, gk_ref):
    h = _rmsnorm_rows(x_ref[...], g_ref[...], NORM_EPS).astype(BF16)
    low = _dot(h, wa_ref[...])
    logits = _dot(low.astype(BF16), wb_ref[...]) + bg_ref[...]
    log_sig = jnp.minimum(logits, 0.0) - jnp.log(1.0 + jnp.exp(-jnp.abs(logits)))
    gk_ref[...] = log_sig * (math.log2(math.e) / GLA_TAU)
    _project(h, w_ref, o_ref)


def _norm_proj_vt_kernel(x_ref, g_ref, w_ref, wv_ref, o_ref, vt_ref, wvt_ref):
    @pl.when(pl.program_id(0) == 0)
    def _():
        wvt_ref[...] = wv_ref[...].T

    h = _rmsnorm_rows(x_ref[...], g_ref[...], NORM_EPS).astype(BF16)
    _project(h, w_ref, o_ref)
    tile = vt_ref.shape[3]
    for j in range(vt_ref.shape[1]):
        vt_ref[0, j] = _dot_nt(wvt_ref[...], h[j * tile:(j + 1) * tile]).astype(vt_ref.dtype)


def _norm_proj_call(kernel, name, x2d, gain, w_bf, n_out, extra_in, extra_specs, extra_out_shape,
                    extra_out_spec, scratch_shapes=(), semantics="parallel"):
    t, d = x2d.shape
    assert t % PROJ_ROW_TILE == 0 and n_out % PROJ_COL_TILE == 0 and w_bf.shape[1] >= n_out
    return pl.pallas_call(
        kernel,
        grid=(t // PROJ_ROW_TILE,),
        in_specs=[pl.BlockSpec((PROJ_ROW_TILE, d), lambda i: (i, 0)), _resident((1, d)),
                  _resident((d, n_out))] + extra_specs,
        out_specs=[pl.BlockSpec((PROJ_ROW_TILE, n_out), lambda i: (i, 0)), extra_out_spec],
        out_shape=[jax.ShapeDtypeStruct((t, n_out), BF16), extra_out_shape],
        scratch_shapes=list(scratch_shapes),
        compiler_params=pltpu.CompilerParams(
            dimension_semantics=(semantics,), vmem_limit_bytes=VMEM_LIMIT_BYTES),
        name=name,
    )(x2d, gain.reshape(1, d), w_bf, *extra_in)


def _norm_proj_gate(x2d, gain, w, w_a, w_b, b_g):
    t, d = x2d.shape
    rank, dk = w_b.shape
    return _norm_proj_call(
        _norm_proj_gate_kernel, "norm_proj_gate", x2d, gain, w.astype(BF16), w.shape[1],
        [w_a.astype(BF16), w_b.astype(BF16), b_g.reshape(1, dk)],
        [_resident((d, rank)), _resident((rank, dk)), _resident((1, dk))],
        jax.ShapeDtypeStruct((t, dk), F32), pl.BlockSpec((PROJ_ROW_TILE, dk), lambda i: (i, 0)))


def _norm_proj_vt(x2d, gain, w, n_qk, batch, seq):
    t, d = x2d.shape
    d_v = w.shape[1] - n_qk
    assert n_qk % d_v == 0 and seq % PROJ_ROW_TILE == 0 and PROJ_ROW_TILE % ATTN_TILE == 0
    steps = seq // PROJ_ROW_TILE
    per_step = PROJ_ROW_TILE // ATTN_TILE
    w_bf = w.astype(BF16)
    value_cols = pl.BlockSpec((d, d_v), lambda i: (0, n_qk // d_v), pipeline_mode=pl.Buffered(1))
    return _norm_proj_call(
        _norm_proj_vt_kernel, "norm_proj_vt", x2d, gain, w_bf, n_qk, [w_bf], [value_cols],
        jax.ShapeDtypeStruct((batch, seq // ATTN_TILE, d_v, ATTN_TILE), BF16),
        pl.BlockSpec((1, per_step, d_v, ATTN_TILE), lambda i: (i // steps, i % steps, 0, 0)),
        scratch_shapes=[pltpu.VMEM((d_v, d), BF16)], semantics="arbitrary")


def _gla_kernel(q_ref, k_ref, v_ref, r_ref, gk_ref, gn_ref, o_ref):
    c = GLA_CHUNK
    grp = GLA_GROUP
    n = grp // c
    seq, hk = q_ref.shape[1], q_ref.shape[2]
    hv = v_ref.shape[2]
    shift = c.bit_length() - 1
    row = lax.broadcasted_iota(jnp.int32, (grp, grp), 0)
    col = lax.broadcasted_iota(jnp.int32, (grp, grp), 1)
    mask = (col <= row) & ((row >> shift) == (col >> shift))
    tri = jnp.where(mask, 1.0, 0.0).astype(BF16)
    gain = gn_ref[...]
    q_scale = hk ** -0.5

    def rows_of(gi):
        return slice(gi * grp, (gi + 1) * grp)

    def cum_decay(gi):
        g = gk_ref[0, rows_of(gi), :]
        g_hi = g.astype(BF16)
        g_lo = (g - g_hi.astype(F32)).astype(BF16)
        return _dot(tri, g_hi) + _dot(tri, g_lo)

    def scaled_operands(gi, b):
        b = b.reshape(n, c, hk)
        b_mid = b[:, c // 2:c // 2 + 1, :]
        b_last = b[:, c - 1:c, :]
        q = (q_ref[0, rows_of(gi), :].astype(F32) * q_scale).reshape(n, c, hk)
        k = k_ref[0, rows_of(gi), :].astype(F32).reshape(n, c, hk)
        q_mid = (q * jnp.exp2(b - b_mid)).reshape(grp, hk).astype(BF16)
        k_mid = (k * jnp.exp2(b_mid - b)).reshape(grp, hk).astype(BF16)
        q_dec = (q * jnp.exp2(b)).astype(BF16)
        k_end = (k * jnp.exp2(b_last - b)).astype(BF16)
        return q_mid, k_mid, q_dec, k_end, jnp.exp2(b_last)

    def state_part(gi, o_intra, q_dec, decay, incs, state_t):
        outs = []
        for ci in range(n):
            outs.append(o_intra[ci * c:(ci + 1) * c] + _dot_nt(q_dec[ci], state_t.astype(BF16)))
            state_t = state_t * decay[ci] + incs[ci]
        o = _rmsnorm_rows(jnp.concatenate(outs, axis=0), gain, NORM_EPS)
        r = r_ref[0, rows_of(gi), :].astype(F32)
        o_ref[0, rows_of(gi), :] = (o * (r * jax.nn.sigmoid(r))).astype(o_ref.dtype)
        return state_t

    groups = seq // grp
    state_t = jnp.zeros((hv, hk), F32)
    cum, ops = {}, {}
    for i in range(-2, groups):
        if i >= 0:
            q_mid, k_mid, q_dec, k_end, decay = ops.pop(i)
            v = v_ref[0, rows_of(i), :]
            attn = jnp.where(mask, _dot_nt(q_mid, k_mid), 0.0).astype(BF16)
            incs = [_dot_tn(v[ci * c:(ci + 1) * c], k_end[ci]) for ci in range(n)]
        if i + 2 < groups:
            cum[i + 2] = cum_decay(i + 2)
        if i >= 0:
            o_intra = _dot(attn, v)
        if 0 <= i + 1 < groups:
            ops[i + 1] = scaled_operands(i + 1, cum.pop(i + 1))
        if i >= 0:
            state_t = state_part(i, o_intra, q_dec, decay, incs, state_t)


def _gla_core(proj, gk, g_norm, batch, seq):
    dk = gk.shape[1]
    hk = dk // GLA_HEADS
    hv = g_norm.shape[0]
    dv = hv * GLA_HEADS
    assert proj.shape[1] == 2 * dk + 2 * dv and seq % GLA_GROUP == 0 and GLA_GROUP % GLA_CHUNK == 0
    proj3 = proj.reshape(batch, seq, proj.shape[1])
    gk3 = gk.reshape(batch, seq, dk)
    k_off, v_off, r_off = dk // hk, (2 * dk) // hv, (2 * dk + dv) // hv
    return pl.pallas_call(
        _gla_kernel,
        grid=(batch, GLA_HEADS),
        in_specs=[
            pl.BlockSpec((1, seq, hk), lambda b, h: (b, 0, h)),
            pl.BlockSpec((1, seq, hk), lambda b, h: (b, 0, k_off + h)),
            pl.BlockSpec((1, seq, hv), lambda b, h: (b, 0, v_off + h)),
            pl.BlockSpec((1, seq, hv), lambda b, h: (b, 0, r_off + h)),
            pl.BlockSpec((1, seq, hk), lambda b, h: (b, 0, h)),
            _resident((1, hv)),
        ],
        out_specs=pl.BlockSpec((1, seq, hv), lambda b, h: (b, 0, h)),
        out_shape=jax.ShapeDtypeStruct((batch, seq, dv), BF16),
        compiler_params=pltpu.CompilerParams(
            dimension_semantics=("parallel", "parallel"), vmem_limit_bytes=VMEM_LIMIT_BYTES),
        name="gla_core",
    )(proj3, proj3, proj3, proj3, gk3, g_norm.reshape(1, hv))


def _diff_attn_kernel(lq1_ref, lk1_ref, lq2_ref, lk2_ref, gn_ref, q_ref, k_ref, vt_ref, o_ref,
                      qq_ref, m_ref, acc_ref, *, lambda_init):
    tiles, d2, tq = vt_ref.shape[1], vt_ref.shape[2], vt_ref.shape[3]
    d = d2 // 2
    ts = ATTN_SUB_TILE
    assert tq % ts == 0

    lane = lax.broadcasted_iota(jnp.int32, (tq, d2), 1)
    for t in range(tiles):
        q = q_ref[0, t * tq:(t + 1) * tq, :].astype(F32) * (d ** -0.5 * math.log2(math.e))
        qq_ref[t, 0:tq, :] = jnp.where(lane < d, q, 0.0).astype(BF16)
        qq_ref[t, tq:2 * tq, :] = jnp.where(lane >= d, q, 0.0).astype(BF16)
    lam = (jnp.exp(jnp.sum(lq1_ref[...] * lk1_ref[...], keepdims=True))
           - jnp.exp(jnp.sum(lq2_ref[...] * lk2_ref[...], keepdims=True)) + lambda_init)
    out_gain = gn_ref[...] * (1.0 - lambda_init)

    def query_ranges(t, kt, off):
        if kt < t or off == 0:
            return [(0, 2 * tq)]
        return [(off, tq), (tq + off, 2 * tq)]

    def scores(t, kt):
        nk = ts if kt == t else tq
        pieces = []
        for off in range(0, tq, nk):
            k = k_ref[0, kt * tq + off:kt * tq + off + nk, :]
            for l0, l1 in query_ranges(t, kt, off):
                st = _dot_nt(k, qq_ref[t, l0:l1, :])
                if kt == t:
                    k_pos = off + lax.broadcasted_iota(jnp.int32, st.shape, 0)
                    q_pos = (l0 + lax.broadcasted_iota(jnp.int32, st.shape, 1)) & (tq - 1)
                    st = jnp.where(k_pos <= q_pos, st, -jnp.inf)
                pieces.append((off, nk, l0, l1, st))
        return pieces

    def update(pieces, kt):
        m_prev = m_ref[...]
        for _, _, l0, l1, st in pieces:
            m_ref[:, l0:l1] = jnp.maximum(m_ref[:, l0:l1], jnp.max(st, axis=0, keepdims=True))
        m_new = m_ref[...]
        alpha = jnp.exp2(m_prev - m_new)
        edges = sorted({e for _, _, l0, l1, _ in pieces for e in (l0, l1)})
        for a, b in zip(edges[:-1], edges[1:]):
            parts = [(off, nk, st[:, a - l0:b - l0]) for off, nk, l0, l1, st in pieces
                     if l0 <= a and b <= l1]
            p = jnp.concatenate([jnp.exp2(st - m_new[:, a:b]).astype(BF16) for _, _, st in parts], axis=0)
            vt = jnp.concatenate([vt_ref[0, kt, :, off:off + nk] for off, nk, _ in parts], axis=1)
            vt = jnp.concatenate([vt, jnp.ones((ATTN_SUM_ROWS, vt.shape[1]), BF16)], axis=0)
            acc_ref[:, a:b] = alpha[:, a:b] * acc_ref[:, a:b] + _dot(vt, p)

    def finalize(t):
        o_all = acc_ref[0:d2, :] / acc_ref[d2:d2 + 1, :]
        o_t = o_all[:, 0:tq] - lam * o_all[:, tq:2 * tq]
        inv_rms = lax.rsqrt(jnp.mean(o_t * o_t, axis=0, keepdims=True) + SUBLN_EPS)
        o_ref[0, t * tq:(t + 1) * tq, :] = (o_t * inv_rms * out_gain).T.astype(o_ref.dtype)

    units = [(t, kt) for t in range(tiles) for kt in range(t + 1)]
    pending = {}
    for i in range(len(units) + ATTN_LOOKAHEAD):
        if i < len(units):
            pending[i] = scores(*units[i])
        j = i - ATTN_LOOKAHEAD
        if j >= 0:
            t, kt = units[j]
            if kt == 0:
                m_ref[...] = jnp.full_like(m_ref, -jnp.inf)
                acc_ref[...] = jnp.zeros_like(acc_ref)
            update(pending.pop(j), kt)
            if kt == t:
                finalize(t)


def _diff_attn(qk, v_t, lam_q1, lam_k1, lam_q2, lam_k2, g_norm, batch, seq, lambda_init):
    d_model = qk.shape[1] // 2
    d2 = d_model // DIFF_HEADS
    d = d2 // 2
    tq = ATTN_TILE
    tiles = seq // tq
    assert seq % tq == 0 and g_norm.shape[0] == d2 and v_t.shape == (batch, tiles, d_model, tq)
    qk3 = qk.reshape(batch, seq, 2 * d_model)
    return pl.pallas_call(
        functools.partial(_diff_attn_kernel, lambda_init=lambda_init),
        grid=(batch, DIFF_HEADS),
        in_specs=[_resident((1, d))] * 4 + [
            _resident((d2, 1)),
            pl.BlockSpec((1, seq, d2), lambda b, h: (b, 0, h)),
            pl.BlockSpec((1, seq, d2), lambda b, h: (b, 0, DIFF_HEADS + h)),
            pl.BlockSpec((1, tiles, d2, tq), lambda b, h: (b, 0, h, 0)),
        ],
        out_specs=pl.BlockSpec((1, seq, d2), lambda b, h: (b, 0, h)),
        out_shape=jax.ShapeDtypeStruct((batch, seq, d_model), BF16),
        scratch_shapes=[
            pltpu.VMEM((tiles, 2 * tq, d2), BF16),
            pltpu.VMEM((1, 2 * tq), F32),
            pltpu.VMEM((d2 + ATTN_SUM_ROWS, 2 * tq), F32),
        ],
        compiler_params=pltpu.CompilerParams(
            dimension_semantics=("parallel", "parallel"), vmem_limit_bytes=VMEM_LIMIT_BYTES),
        name="diff_attn",
    )(lam_q1.reshape(1, d), lam_k1.reshape(1, d), lam_q2.reshape(1, d), lam_k2.reshape(1, d),
      g_norm.reshape(d2, 1), qk3, qk3, v_t)


def _mix_out_ffn_kernel(x_ref, a_ref, wo_ref, gf_ref, wi_ref, w2_ref, gl_ref, o_ref, acc_ref, *, final_norm):
    d_ff = w2_ref.shape[0]
    x1 = x_ref[...] + _dot(a_ref[...], wo_ref[...])
    h = _rmsnorm_rows(x1, gf_ref[...], NORM_EPS).astype(BF16)
    acc_ref[...] = x1
    for f0 in range(0, d_ff, FFN_COL_TILE):
        gate = _dot(h, wi_ref[:, f0:f0 + FFN_COL_TILE])
        up = _dot(h, wi_ref[:, d_ff + f0:d_ff + f0 + FFN_COL_TILE])
        act = (gate * jax.nn.sigmoid(gate) * up).astype(BF16)
        acc_ref[...] += _dot(act, w2_ref[f0:f0 + FFN_COL_TILE, :])
    x2 = acc_ref[...]
    if final_norm:
        x2 = _rmsnorm_rows(x2, gl_ref[...], NORM_EPS)
    o_ref[...] = x2


def _mix_out_ffn(x2d, mix, w_out, layer, g_ffn_all, w_in_all, w2_all, g_last, final_norm):
    t, d = x2d.shape
    d_ff = w2_all.shape[1]
    assert t % FFN_ROW_TILE == 0 and d_ff % FFN_COL_TILE == 0
    row_spec = pl.BlockSpec((FFN_ROW_TILE, d), lambda i: (i, 0))
    return pl.pallas_call(
        functools.partial(_mix_out_ffn_kernel, final_norm=final_norm),
        grid=(t // FFN_ROW_TILE,),
        in_specs=[row_spec, row_spec, _resident((d, d)), _resident((1, d), layer),
                  _resident((d, 2 * d_ff), layer), _resident((d_ff, d), layer), _resident((1, d))],
        out_specs=row_spec,
        out_shape=jax.ShapeDtypeStruct((t, d), F32),
        scratch_shapes=[pltpu.VMEM((FFN_ROW_TILE, d), F32)],
        compiler_params=pltpu.CompilerParams(
            dimension_semantics=("parallel",), vmem_limit_bytes=VMEM_LIMIT_BYTES),
        name="mix_out_ffn",
    )(x2d, mix, w_out.astype(BF16), g_ffn_all.reshape(-1, 1, d), w_in_all, w2_all, g_last.reshape(1, d))


def kernel(x, gla_w_in, gla_w_gate_a, gla_w_gate_b, gla_b_gate, gla_norm, gla_w_out, diff_w_in, diff_lam_q1, diff_lam_k1, diff_lam_q2, diff_lam_k2, diff_norm, diff_w_out, norm_mixer, norm_ffn, ffn_w_in, ffn_w_out, norm_final):
    batch, seq, d = x.shape
    x2d = x.reshape(batch * seq, d)
    ffn_w_in_bf, ffn_w_out_bf = ffn_w_in.astype(BF16), ffn_w_out.astype(BF16)

    proj, gk = _norm_proj_gate(x2d, norm_mixer[0], gla_w_in[0],
                               gla_w_gate_a[0], gla_w_gate_b[0], gla_b_gate[0])
    mix = _gla_core(proj, gk, gla_norm[0], batch, seq).reshape(batch * seq, -1)
    x2d = _mix_out_ffn(x2d, mix, gla_w_out[0], 0, norm_ffn, ffn_w_in_bf, ffn_w_out_bf, norm_final,
                       final_norm=False)

    lambda_init = 0.8 - 0.6 * math.exp(-0.3 * 1)
    qk, v_t = _norm_proj_vt(x2d, norm_mixer[1], diff_w_in[0], 2 * d, batch, seq)
    mix = _diff_attn(qk, v_t, diff_lam_q1[0], diff_lam_k1[0], diff_lam_q2[0], diff_lam_k2[0],
                     diff_norm[0], batch, seq, lambda_init).reshape(batch * seq, -1)
    x2d = _mix_out_ffn(x2d, mix, diff_w_out[0], 1, norm_ffn, ffn_w_in_bf, ffn_w_out_bf, norm_final,
                       final_norm=True)
    return x2d.reshape(batch, seq, d)
```

```python
import functools
import math

import jax
import jax.numpy as jnp
from jax import lax
from jax.experimental import pallas as pl
from jax.experimental.pallas import tpu as pltpu

F32 = jnp.float32
BF16 = jnp.bfloat16

NORM_EPS = 1e-6
SUBLN_EPS = 1e-5

GLA_HEADS = 4
GLA_GATE_RANK = 16
GLA_TAU = 16.0
GLA_CHUNK = 64

DIFF_HEADS = 8

V7X_VMEM_BYTES = 64 * 1024 * 1024
VMEM_LIMIT_BYTES = V7X_VMEM_BYTES - 8 * 1024 * 1024

PROJ_ROW_TILE = 1024
FFN_ROW_TILE = 1024
PROJ_COL_TILE = 1024
FFN_COL_TILE = 256
GLA_GROUP = 256
ATTN_TILE = 512
ATTN_SUB_TILE = 256
ATTN_SUM_ROWS = 16
ATTN_LOOKAHEAD = 2
ATTN_HEADS_PER_STEP = 2


def _resident(shape, layer=None):
    if layer is None:
        index, block = (0,) * len(shape), tuple(shape)
    else:
        index, block = (layer,) + (0,) * len(shape), (None,) + tuple(shape)
    return pl.BlockSpec(block, lambda *_: index, pipeline_mode=pl.Buffered(1))


def _rmsnorm_rows(x, gain, eps):
    return x * lax.rsqrt(jnp.mean(x * x, axis=-1, keepdims=True) + eps) * gain


def _dot(a, b):
    return jnp.dot(a, b, preferred_element_type=F32)


def _dot_nt(a, b):
    return lax.dot_general(a, b, (((1,), (1,)), ((), ())), preferred_element_type=F32)


def _dot_tn(a, b):
    return lax.dot_general(a, b, (((0,), (0,)), ((), ())), preferred_element_type=F32)


def _project(h, w_ref, o_ref):
    for c0 in range(0, o_ref.shape[1], PROJ_COL_TILE):
        cols = slice(c0, c0 + PROJ_COL_TILE)
        o_ref[:, cols] = _dot(h, w_ref[:, cols]).astype(o_ref.dtype)


def _norm_proj_gate_kernel(x_ref, g_ref, w_ref, wa_ref, wb_ref, bg_ref, o_ref, gk_ref):
    h = _rmsnorm_rows(x_ref[...], g_ref[...], NORM_EPS).astype(BF16)
    low = _dot(h, wa_ref[...])
    logits = _dot(low.astype(BF16), wb_ref[...]) + bg_ref[...]
    log_sig = jnp.minimum(logits, 0.0) - jnp.log(1.0 + jnp.exp(-jnp.abs(logits)))
    gk_ref[...] = log_sig * (math.log2(math.e) / GLA_TAU)
    _project(h, w_ref, o_ref)


def _norm_proj_vt_kernel(x_ref, g_ref, w_ref, wv_ref, o_ref, vt_ref, wvt_ref):
    @pl.when(pl.program_id(0) == 0)
    def _():
        wvt_ref[...] = wv_ref[...].T

    h = _rmsnorm_rows(x_ref[...], g_ref[...], NORM_EPS).astype(BF16)
    _project(h, w_ref, o_ref)
    tile = vt_ref.shape[3]
    for j in range(vt_ref.shape[1]):
        vt_ref[0, j] = _dot_nt(wvt_ref[...], h[j * tile:(j + 1) * tile]).astype(vt_ref.dtype)


def _norm_proj_call(kernel, name, x2d, gain, w_bf, n_out, extra_in, extra_specs, extra_out_shape,
                    extra_out_spec, scratch_shapes=(), semantics="parallel"):
    t, d = x2d.shape
    assert t % PROJ_ROW_TILE == 0 and n_out % PROJ_COL_TILE == 0 and w_bf.shape[1] >= n_out
    return pl.pallas_call(
        kernel,
        grid=(t // PROJ_ROW_TILE,),
        in_specs=[pl.BlockSpec((PROJ_ROW_TILE, d), lambda i: (i, 0)), _resident((1, d)),
                  _resident((d, n_out))] + extra_specs,
        out_specs=[pl.BlockSpec((PROJ_ROW_TILE, n_out), lambda i: (i, 0)), extra_out_spec],
        out_shape=[jax.ShapeDtypeStruct((t, n_out), BF16), extra_out_shape],
        scratch_shapes=list(scratch_shapes),
        compiler_params=pltpu.CompilerParams(
            dimension_semantics=(semantics,), vmem_limit_bytes=VMEM_LIMIT_BYTES),
        name=name,
    )(x2d, gain.reshape(1, d), w_bf, *extra_in)


def _norm_proj_gate(x2d, gain, w, w_a, w_b, b_g):
    t, d = x2d.shape
    rank, dk = w_b.shape
    return _norm_proj_call(
        _norm_proj_gate_kernel, "norm_proj_gate", x2d, gain, w.astype(BF16), w.shape[1],
        [w_a.astype(BF16), w_b.astype(BF16), b_g.reshape(1, dk)],
        [_resident((d, rank)), _resident((rank, dk)), _resident((1, dk))],
        jax.ShapeDtypeStruct((t, dk), F32), pl.BlockSpec((PROJ_ROW_TILE, dk), lambda i: (i, 0)))


def _norm_proj_vt(x2d, gain, w, n_qk, batch, seq):
    t, d = x2d.shape
    d_v = w.shape[1] - n_qk
    assert n_qk % d_v == 0 and seq % PROJ_ROW_TILE == 0 and PROJ_ROW_TILE % ATTN_TILE == 0
    steps = seq // PROJ_ROW_TILE
    per_step = PROJ_ROW_TILE // ATTN_TILE
    w_bf = w.astype(BF16)
    value_cols = pl.BlockSpec((d, d_v), lambda i: (0, n_qk // d_v), pipeline_mode=pl.Buffered(1))
    return _norm_proj_call(
        _norm_proj_vt_kernel, "norm_proj_vt", x2d, gain, w_bf, n_qk, [w_bf], [value_cols],
        jax.ShapeDtypeStruct((batch, seq // ATTN_TILE, d_v, ATTN_TILE), BF16),
        pl.BlockSpec((1, per_step, d_v, ATTN_TILE), lambda i: (i // steps, i % steps, 0, 0)),
        scratch_shapes=[pltpu.VMEM((d_v, d), BF16)], semantics="arbitrary")


def _gla_kernel(q_ref, k_ref, v_ref, r_ref, gk_ref, gn_ref, o_ref):
    c = GLA_CHUNK
    grp = GLA_GROUP
    n = grp // c
    seq, hk = q_ref.shape[1], q_ref.shape[2]
    hv = v_ref.shape[2]
    shift = c.bit_length() - 1
    row = lax.broadcasted_iota(jnp.int32, (grp, grp), 0)
    col = lax.broadcasted_iota(jnp.int32, (grp, grp), 1)
    mask = (col <= row) & ((row >> shift) == (col >> shift))
    tri = jnp.where(mask, 1.0, 0.0).astype(BF16)
    gain = gn_ref[...]
    q_scale = hk ** -0.5

    def rows_of(gi):
        return slice(gi * grp, (gi + 1) * grp)

    def cum_decay(gi):
        g = gk_ref[0, rows_of(gi), :]
        g_hi = g.astype(BF16)
        g_lo = (g - g_hi.astype(F32)).astype(BF16)
        return _dot(tri, g_hi) + _dot(tri, g_lo)

    def scaled_operands(gi, b):
        b = b.reshape(n, c, hk)
        b_mid = b[:, c // 2:c // 2 + 1, :]
        b_last = b[:, c - 1:c, :]
        q = (q_ref[0, rows_of(gi), :].astype(F32) * q_scale).reshape(n, c, hk)
        k = k_ref[0, rows_of(gi), :].astype(F32).reshape(n, c, hk)
        q_mid = (q * jnp.exp2(b - b_mid)).reshape(grp, hk).astype(BF16)
        k_mid = (k * jnp.exp2(b_mid - b)).reshape(grp, hk).astype(BF16)
        q_dec = (q * jnp.exp2(b)).astype(BF16)
        k_end = (k * jnp.exp2(b_last - b)).astype(BF16)
        return q_mid, k_mid, q_dec, k_end, jnp.exp2(b_last)

    def state_part(gi, o_intra, q_dec, decay, incs, state_t):
        outs = []
        for ci in range(n):
            outs.append(o_intra[ci * c:(ci + 1) * c] + _dot_nt(q_dec[ci], state_t.astype(BF16)))
            state_t = state_t * decay[ci] + incs[ci]
        o = _rmsnorm_rows(jnp.concatenate(outs, axis=0), gain, NORM_EPS)
        r = r_ref[0, rows_of(gi), :].astype(F32)
        o_ref[0, rows_of(gi), :] = (o * (r * jax.nn.sigmoid(r))).astype(o_ref.dtype)
        return state_t

    groups = seq // grp
    state_t = jnp.zeros((hv, hk), F32)
    cum, ops = {}, {}
    for i in range(-2, groups):
        if i >= 0:
            q_mid, k_mid, q_dec, k_end, decay = ops.pop(i)
            v = v_ref[0, rows_of(i), :]
            attn = jnp.where(mask, _dot_nt(q_mid, k_mid), 0.0).astype(BF16)
            incs = [_dot_tn(v[ci * c:(ci + 1) * c], k_end[ci]) for ci in range(n)]
        if i + 2 < groups:
            cum[i + 2] = cum_decay(i + 2)
        if i >= 0:
            o_intra = _dot(attn, v)
        if 0 <= i + 1 < groups:
            ops[i + 1] = scaled_operands(i + 1, cum.pop(i + 1))
        if i >= 0:
            state_t = state_part(i, o_intra, q_dec, decay, incs, state_t)


def _gla_core(proj, gk, g_norm, batch, seq):
    dk = gk.shape[1]
    hk = dk // GLA_HEADS
    hv = g_norm.shape[0]
    dv = hv * GLA_HEADS
    assert proj.shape[1] == 2 * dk + 2 * dv and seq % GLA_GROUP == 0 and GLA_GROUP % GLA_CHUNK == 0
    proj3 = proj.reshape(batch, seq, proj.shape[1])
    gk3 = gk.reshape(batch, seq, dk)
    k_off, v_off, r_off = dk // hk, (2 * dk) // hv, (2 * dk + dv) // hv
    return pl.pallas_call(
        _gla_kernel,
        grid=(batch, GLA_HEADS),
        in_specs=[
            pl.BlockSpec((1, seq, hk), lambda b, h: (b, 0, h)),
            pl.BlockSpec((1, seq, hk), lambda b, h: (b, 0, k_off + h)),
            pl.BlockSpec((1, seq, hv), lambda b, h: (b, 0, v_off + h)),
            pl.BlockSpec((1, seq, hv), lambda b, h: (b, 0, r_off + h)),
            pl.BlockSpec((1, seq, hk), lambda b, h: (b, 0, h)),
            _resident((1, hv)),
        ],
        out_specs=pl.BlockSpec((1, seq, hv), lambda b, h: (b, 0, h)),
        out_shape=jax.ShapeDtypeStruct((batch, seq, dv), BF16),
        compiler_params=pltpu.CompilerParams(
            dimension_semantics=("parallel", "parallel"), vmem_limit_bytes=VMEM_LIMIT_BYTES),
        name="gla_core",
    )(proj3, proj3, proj3, proj3, gk3, g_norm.reshape(1, hv))


def _diff_attn_kernel(lq1_ref, lk1_ref, lq2_ref, lk2_ref, gn_ref, q_ref, k_ref, vt_ref, o_ref,
                      qq_ref, m_ref, acc_ref, *, lambda_init):
    tiles, tq = vt_ref.shape[1], vt_ref.shape[3]
    heads = ATTN_HEADS_PER_STEP
    d2 = vt_ref.shape[2] // heads
    d = d2 // 2
    ts = ATTN_SUB_TILE
    assert tq % ts == 0

    def head_lanes(hh):
        return slice(hh * d2, (hh + 1) * d2)

    lane = lax.broadcasted_iota(jnp.int32, (tq, d2), 1)
    for hh in range(heads):
        for t in range(tiles):
            q = q_ref[0, t * tq:(t + 1) * tq, head_lanes(hh)].astype(F32) * (d ** -0.5 * math.log2(math.e))
            qq_ref[hh * tiles + t, 0:tq, :] = jnp.where(lane < d, q, 0.0).astype(BF16)
            qq_ref[hh * tiles + t, tq:2 * tq, :] = jnp.where(lane >= d, q, 0.0).astype(BF16)
    lam = (jnp.exp(jnp.sum(lq1_ref[...] * lk1_ref[...], keepdims=True))
           - jnp.exp(jnp.sum(lq2_ref[...] * lk2_ref[...], keepdims=True)) + lambda_init)
    out_gain = gn_ref[...] * (1.0 - lambda_init)

    def query_ranges(t, kt, off):
        if kt < t or off == 0:
            return [(0, 2 * tq)]
        return [(off, tq), (tq + off, 2 * tq)]

    def scores(hh, t, kt):
        nk = ts if kt == t else tq
        pieces = []
        for off in range(0, tq, nk):
            k = k_ref[0, kt * tq + off:kt * tq + off + nk, head_lanes(hh)]
            for l0, l1 in query_ranges(t, kt, off):
                st = _dot_nt(k, qq_ref[hh * tiles + t, l0:l1, :])
                if kt == t:
                    k_pos = off + lax.broadcasted_iota(jnp.int32, st.shape, 0)
                    q_pos = (l0 + lax.broadcasted_iota(jnp.int32, st.shape, 1)) & (tq - 1)
                    st = jnp.where(k_pos <= q_pos, st, -jnp.inf)
                pieces.append((off, nk, l0, l1, st))
        return pieces

    def update(pieces, hh, kt):
        m_prev = m_ref[...]
        for _, _, l0, l1, st in pieces:
            m_ref[:, l0:l1] = jnp.maximum(m_ref[:, l0:l1], jnp.max(st, axis=0, keepdims=True))
        m_new = m_ref[...]
        alpha = jnp.exp2(m_prev - m_new)
        edges = sorted({e for _, _, l0, l1, _ in pieces for e in (l0, l1)})
        for a, b in zip(edges[:-1], edges[1:]):
            parts = [(off, nk, st[:, a - l0:b - l0]) for off, nk, l0, l1, st in pieces
                     if l0 <= a and b <= l1]
            p = jnp.concatenate([jnp.exp2(st - m_new[:, a:b]).astype(BF16) for _, _, st in parts], axis=0)
            vt = jnp.concatenate([vt_ref[0, kt, head_lanes(hh), off:off + nk] for off, nk, _ in parts],
                                 axis=1)
            vt = jnp.concatenate([vt, jnp.ones((ATTN_SUM_ROWS, vt.shape[1]), BF16)], axis=0)
            acc_ref[:, a:b] = alpha[:, a:b] * acc_ref[:, a:b] + _dot(vt, p)

    def finalize(hh, t):
        o_all = acc_ref[0:d2, :] / acc_ref[d2:d2 + 1, :]
        o_t = o_all[:, 0:tq] - lam * o_all[:, tq:2 * tq]
        inv_rms = lax.rsqrt(jnp.mean(o_t * o_t, axis=0, keepdims=True) + SUBLN_EPS)
        o_ref[0, t * tq:(t + 1) * tq, head_lanes(hh)] = (o_t * inv_rms * out_gain).T.astype(o_ref.dtype)

    units = [(hh, t, kt) for hh in range(heads) for t in range(tiles) for kt in range(t + 1)]
    pending = {}
    for i in range(len(units) + ATTN_LOOKAHEAD):
        if i < len(units):
            pending[i] = scores(*units[i])
        j = i - ATTN_LOOKAHEAD
        if j >= 0:
            hh, t, kt = units[j]
            if kt == 0:
                m_ref[...] = jnp.full_like(m_ref, -jnp.inf)
                acc_ref[...] = jnp.zeros_like(acc_ref)
            update(pending.pop(j), hh, kt)
            if kt == t:
                finalize(hh, t)


def _diff_attn(qk, v_t, lam_q1, lam_k1, lam_q2, lam_k2, g_norm, batch, seq, lambda_init):
    d_model = qk.shape[1] // 2
    d2 = d_model // DIFF_HEADS
    d = d2 // 2
    tq = ATTN_TILE
    tiles = seq // tq
    heads = ATTN_HEADS_PER_STEP
    assert seq % tq == 0 and g_norm.shape[0] == d2 and v_t.shape == (batch, tiles, d_model, tq)
    assert DIFF_HEADS % heads == 0
    qk3 = qk.reshape(batch, seq, 2 * d_model)
    k_off = DIFF_HEADS // heads
    return pl.pallas_call(
        functools.partial(_diff_attn_kernel, lambda_init=lambda_init),
        grid=(batch, DIFF_HEADS // heads),
        in_specs=[_resident((1, d))] * 4 + [
            _resident((d2, 1)),
            pl.BlockSpec((1, seq, heads * d2), lambda b, h: (b, 0, h)),
            pl.BlockSpec((1, seq, heads * d2), lambda b, h: (b, 0, k_off + h)),
            pl.BlockSpec((1, tiles, heads * d2, tq), lambda b, h: (b, 0, h, 0)),
        ],
        out_specs=pl.BlockSpec((1, seq, heads * d2), lambda b, h: (b, 0, h)),
        out_shape=jax.ShapeDtypeStruct((batch, seq, d_model), BF16),
        scratch_shapes=[
            pltpu.VMEM((heads * tiles, 2 * tq, d2), BF16),
            pltpu.VMEM((1, 2 * tq), F32),
            pltpu.VMEM((d2 + ATTN_SUM_ROWS, 2 * tq), F32),
        ],
        compiler_params=pltpu.CompilerParams(
            dimension_semantics=("parallel", "parallel"), vmem_limit_bytes=VMEM_LIMIT_BYTES),
        name="diff_attn",
    )(lam_q1.reshape(1, d), lam_k1.reshape(1, d), lam_q2.reshape(1, d), lam_k2.reshape(1, d),
      g_norm.reshape(d2, 1), qk3, qk3, v_t)


def _mix_out_ffn_kernel(x_ref, a_ref, wo_ref, gf_ref, wi_ref, w2_ref, gl_ref, o_ref, acc_ref, *, final_norm):
    d_ff = w2_ref.shape[0]
    x1 = x_ref[...] + _dot(a_ref[...], wo_ref[...])
    h = _rmsnorm_rows(x1, gf_ref[...], NORM_EPS).astype(BF16)
    acc_ref[...] = x1
    for f0 in range(0, d_ff, FFN_COL_TILE):
        gate = _dot(h, wi_ref[:, f0:f0 + FFN_COL_TILE])
        up = _dot(h, wi_ref[:, d_ff + f0:d_ff + f0 + FFN_COL_TILE])
        act = (gate * jax.nn.sigmoid(gate) * up).astype(BF16)
        acc_ref[...] += _dot(act, w2_ref[f0:f0 + FFN_COL_TILE, :])
    x2 = acc_ref[...]
    if final_norm:
        x2 = _rmsnorm_rows(x2, gl_ref[...], NORM_EPS)
    o_ref[...] = x2


def _mix_out_ffn(x2d, mix, w_out, layer, g_ffn_all, w_in_all, w2_all, g_last, final_norm):
    t, d = x2d.shape
    d_ff = w2_all.shape[1]
    assert t % FFN_ROW_TILE == 0 and d_ff % FFN_COL_TILE == 0
    row_spec = pl.BlockSpec((FFN_ROW_TILE, d), lambda i: (i, 0))
    return pl.pallas_call(
        functools.partial(_mix_out_ffn_kernel, final_norm=final_norm),
        grid=(t // FFN_ROW_TILE,),
        in_specs=[row_spec, row_spec, _resident((d, d)), _resident((1, d), layer),
                  _resident((d, 2 * d_ff), layer), _resident((d_ff, d), layer), _resident((1, d))],
        out_specs=row_spec,
        out_shape=jax.ShapeDtypeStruct((t, d), F32),
        scratch_shapes=[pltpu.VMEM((FFN_ROW_TILE, d), F32)],
        compiler_params=pltpu.CompilerParams(
            dimension_semantics=("parallel",), vmem_limit_bytes=VMEM_LIMIT_BYTES),
        name="mix_out_ffn",
    )(x2d, mix, w_out.astype(BF16), g_ffn_all.reshape(-1, 1, d), w_in_all, w2_all, g_last.reshape(1, d))


def kernel(x, gla_w_in, gla_w_gate_a, gla_w_gate_b, gla_b_gate, gla_norm, gla_w_out, diff_w_in, diff_lam_q1, diff_lam_k1, diff_lam_q2, diff_lam_k2, diff_norm, diff_w_out, norm_mixer, norm_ffn, ffn_w_in, ffn_w_out, norm_final):
    batch, seq, d = x.shape
    x2d = x.reshape(batch * seq, d)
    ffn_w_in_bf, ffn_w_out_bf = ffn_w_in.astype(BF16), ffn_w_out.astype(BF16)

    proj, gk = _norm_proj_gate(x2d, norm_mixer[0], gla_w_in[0],
                               gla_w_gate_a[0], gla_w_gate_b[0], gla_b_gate[0])
    mix = _gla_core(proj, gk, gla_norm[0], batch, seq).reshape(batch * seq, -1)
    x2d = _mix_out_ffn(x2d, mix, gla_w_out[0], 0, norm_ffn, ffn_w_in_bf, ffn_w_out_bf, norm_final,
                       final_norm=False)

    lambda_init = 0.8 - 0.6 * math.exp(-0.3 * 1)
    qk, v_t = _norm_proj_vt(x2d, norm_mixer[1], diff_w_in[0], 2 * d, batch, seq)
    mix = _diff_attn(qk, v_t, diff_lam_q1[0], diff_lam_k1[0], diff_lam_q2[0], diff_lam_k2[0],
                     diff_norm[0], batch, seq, lambda_init).reshape(batch * seq, -1)
    x2d = _mix_out_ffn(x2d, mix, diff_w_out[0], 1, norm_ffn, ffn_w_in_bf, ffn_w_out_bf, norm_final,
                       final_norm=True)
    return x2d.reshape(batch, seq, d)
```

```python
import functools
import math

import jax
import jax.numpy as jnp
from jax import lax
from jax.experimental import pallas as pl
from jax.experimental.pallas import tpu as pltpu

F32 = jnp.float32
BF16 = jnp.bfloat16

NORM_EPS = 1e-6
SUBLN_EPS = 1e-5

GLA_HEADS = 4
GLA_GATE_RANK = 16
GLA_TAU = 16.0
GLA_CHUNK = 64

DIFF_HEADS = 8

V7X_VMEM_BYTES = 64 * 1024 * 1024
VMEM_LIMIT_BYTES = V7X_VMEM_BYTES - 8 * 1024 * 1024

PROJ_ROW_TILE = 1024
FFN_ROW_TILE = 1024
PROJ_COL_TILE = 1024
FFN_COL_TILE = 256
GLA_GROUP = 256
ATTN_TILE = 512
ATTN_SUB_TILE = 256
ATTN_SUM_ROWS = 16
ATTN_LOOKAHEAD = 2
ATTN_HEADS_PER_STEP = 2
ATTN_MAX_FIXED_SHIFT = 40.0
ATTN_BOUND_SLACK = 1.02


def _resident(shape, layer=None):
    if layer is None:
        index, block = (0,) * len(shape), tuple(shape)
    else:
        index, block = (layer,) + (0,) * len(shape), (None,) + tuple(shape)
    return pl.BlockSpec(block, lambda *_: index, pipeline_mode=pl.Buffered(1))


def _rmsnorm_rows(x, gain, eps):
    return x * lax.rsqrt(jnp.mean(x * x, axis=-1, keepdims=True) + eps) * gain


def _dot(a, b):
    return jnp.dot(a, b, preferred_element_type=F32)


def _dot_nt(a, b):
    return lax.dot_general(a, b, (((1,), (1,)), ((), ())), preferred_element_type=F32)


def _dot_tn(a, b):
    return lax.dot_general(a, b, (((0,), (0,)), ((), ())), preferred_element_type=F32)


def _project(h, w_ref, o_ref):
    for c0 in range(0, o_ref.shape[1], PROJ_COL_TILE):
        cols = slice(c0, c0 + PROJ_COL_TILE)
        o_ref[:, cols] = _dot(h, w_ref[:, cols]).astype(o_ref.dtype)


def _norm_proj_gate_kernel(x_ref, g_ref, w_ref, wa_ref, wb_ref, bg_ref, o_ref, gk_ref):
    h = _rmsnorm_rows(x_ref[...], g_ref[...], NORM_EPS).astype(BF16)
    low = _dot(h, wa_ref[...])
    logits = _dot(low.astype(BF16), wb_ref[...]) + bg_ref[...]
    log_sig = jnp.minimum(logits, 0.0) - jnp.log(1.0 + jnp.exp(-jnp.abs(logits)))
    gk_ref[...] = log_sig * (math.log2(math.e) / GLA_TAU)
    _project(h, w_ref, o_ref)


def _norm_proj_vt_kernel(x_ref, g_ref, w_ref, wv_ref, o_ref, vt_ref, wvt_ref):
    @pl.when(pl.program_id(0) == 0)
    def _():
        wvt_ref[...] = wv_ref[...].T

    h = _rmsnorm_rows(x_ref[...], g_ref[...], NORM_EPS).astype(BF16)
    _project(h, w_ref, o_ref)
    tile = vt_ref.shape[3]
    for j in range(vt_ref.shape[1]):
        vt_ref[0, j] = _dot_nt(wvt_ref[...], h[j * tile:(j + 1) * tile]).astype(vt_ref.dtype)


def _norm_proj_call(kernel, name, x2d, gain, w_bf, n_out, extra_in, extra_specs, extra_out_shape,
                    extra_out_spec, scratch_shapes=(), semantics="parallel"):
    t, d = x2d.shape
    assert t % PROJ_ROW_TILE == 0 and n_out % PROJ_COL_TILE == 0 and w_bf.shape[1] >= n_out
    return pl.pallas_call(
        kernel,
        grid=(t // PROJ_ROW_TILE,),
        in_specs=[pl.BlockSpec((PROJ_ROW_TILE, d), lambda i: (i, 0)), _resident((1, d)),
                  _resident((d, n_out))] + extra_specs,
        out_specs=[pl.BlockSpec((PROJ_ROW_TILE, n_out), lambda i: (i, 0)), extra_out_spec],
        out_shape=[jax.ShapeDtypeStruct((t, n_out), BF16), extra_out_shape],
        scratch_shapes=list(scratch_shapes),
        compiler_params=pltpu.CompilerParams(
            dimension_semantics=(semantics,), vmem_limit_bytes=VMEM_LIMIT_BYTES),
        name=name,
    )(x2d, gain.reshape(1, d), w_bf, *extra_in)


def _norm_proj_gate(x2d, gain, w, w_a, w_b, b_g):
    t, d = x2d.shape
    rank, dk = w_b.shape
    return _norm_proj_call(
        _norm_proj_gate_kernel, "norm_proj_gate", x2d, gain, w.astype(BF16), w.shape[1],
        [w_a.astype(BF16), w_b.astype(BF16), b_g.reshape(1, dk)],
        [_resident((d, rank)), _resident((rank, dk)), _resident((1, dk))],
        jax.ShapeDtypeStruct((t, dk), F32), pl.BlockSpec((PROJ_ROW_TILE, dk), lambda i: (i, 0)))


def _norm_proj_vt(x2d, gain, w, n_qk, batch, seq):
    t, d = x2d.shape
    d_v = w.shape[1] - n_qk
    assert n_qk % d_v == 0 and seq % PROJ_ROW_TILE == 0 and PROJ_ROW_TILE % ATTN_TILE == 0
    steps = seq // PROJ_ROW_TILE
    per_step = PROJ_ROW_TILE // ATTN_TILE
    w_bf = w.astype(BF16)
    value_cols = pl.BlockSpec((d, d_v), lambda i: (0, n_qk // d_v), pipeline_mode=pl.Buffered(1))
    return _norm_proj_call(
        _norm_proj_vt_kernel, "norm_proj_vt", x2d, gain, w_bf, n_qk, [w_bf], [value_cols],
        jax.ShapeDtypeStruct((batch, seq // ATTN_TILE, d_v, ATTN_TILE), BF16),
        pl.BlockSpec((1, per_step, d_v, ATTN_TILE), lambda i: (i // steps, i % steps, 0, 0)),
        scratch_shapes=[pltpu.VMEM((d_v, d), BF16)], semantics="arbitrary")


def _gla_kernel(q_ref, k_ref, v_ref, r_ref, gk_ref, gn_ref, o_ref):
    c = GLA_CHUNK
    grp = GLA_GROUP
    n = grp // c
    seq, hk = q_ref.shape[1], q_ref.shape[2]
    hv = v_ref.shape[2]
    shift = c.bit_length() - 1
    row = lax.broadcasted_iota(jnp.int32, (grp, grp), 0)
    col = lax.broadcasted_iota(jnp.int32, (grp, grp), 1)
    mask = (col <= row) & ((row >> shift) == (col >> shift))
    tri = jnp.where(mask, 1.0, 0.0).astype(BF16)
    gain = gn_ref[...]
    q_scale = hk ** -0.5

    def rows_of(gi):
        return slice(gi * grp, (gi + 1) * grp)

    def cum_decay(gi):
        g = gk_ref[0, rows_of(gi), :]
        g_hi = g.astype(BF16)
        g_lo = (g - g_hi.astype(F32)).astype(BF16)
        return _dot(tri, g_hi) + _dot(tri, g_lo)

    def scaled_operands(gi, b):
        b = b.reshape(n, c, hk)
        b_mid = b[:, c // 2:c // 2 + 1, :]
        b_last = b[:, c - 1:c, :]
        q = (q_ref[0, rows_of(gi), :].astype(F32) * q_scale).reshape(n, c, hk)
        k = k_ref[0, rows_of(gi), :].astype(F32).reshape(n, c, hk)
        q_mid = (q * jnp.exp2(b - b_mid)).reshape(grp, hk).astype(BF16)
        k_mid = (k * jnp.exp2(b_mid - b)).reshape(grp, hk).astype(BF16)
        q_dec = (q * jnp.exp2(b)).astype(BF16)
        k_end = (k * jnp.exp2(b_last - b)).astype(BF16)
        return q_mid, k_mid, q_dec, k_end, jnp.exp2(b_last)

    def state_part(gi, o_intra, q_dec, decay, incs, state_t):
        outs = []
        for ci in range(n):
            outs.append(o_intra[ci * c:(ci + 1) * c] + _dot_nt(q_dec[ci], state_t.astype(BF16)))
            state_t = state_t * decay[ci] + incs[ci]
        o = _rmsnorm_rows(jnp.concatenate(outs, axis=0), gain, NORM_EPS)
        r = r_ref[0, rows_of(gi), :].astype(F32)
        o_ref[0, rows_of(gi), :] = (o * (r * jax.nn.sigmoid(r))).astype(o_ref.dtype)
        return state_t

    groups = seq // grp
    state_t = jnp.zeros((hv, hk), F32)
    cum, ops = {}, {}
    for i in range(-2, groups):
        if i >= 0:
            q_mid, k_mid, q_dec, k_end, decay = ops.pop(i)
            v = v_ref[0, rows_of(i), :]
            attn = jnp.where(mask, _dot_nt(q_mid, k_mid), 0.0).astype(BF16)
            incs = [_dot_tn(v[ci * c:(ci + 1) * c], k_end[ci]) for ci in range(n)]
        if i + 2 < groups:
            cum[i + 2] = cum_decay(i + 2)
        if i >= 0:
            o_intra = _dot(attn, v)
        if 0 <= i + 1 < groups:
            ops[i + 1] = scaled_operands(i + 1, cum.pop(i + 1))
        if i >= 0:
            state_t = state_part(i, o_intra, q_dec, decay, incs, state_t)


def _gla_core(proj, gk, g_norm, batch, seq):
    dk = gk.shape[1]
    hk = dk // GLA_HEADS
    hv = g_norm.shape[0]
    dv = hv * GLA_HEADS
    assert proj.shape[1] == 2 * dk + 2 * dv and seq % GLA_GROUP == 0 and GLA_GROUP % GLA_CHUNK == 0
    proj3 = proj.reshape(batch, seq, proj.shape[1])
    gk3 = gk.reshape(batch, seq, dk)
    k_off, v_off, r_off = dk // hk, (2 * dk) // hv, (2 * dk + dv) // hv
    return pl.pallas_call(
        _gla_kernel,
        grid=(batch, GLA_HEADS),
        in_specs=[
            pl.BlockSpec((1, seq, hk), lambda b, h: (b, 0, h)),
            pl.BlockSpec((1, seq, hk), lambda b, h: (b, 0, k_off + h)),
            pl.BlockSpec((1, seq, hv), lambda b, h: (b, 0, v_off + h)),
            pl.BlockSpec((1, seq, hv), lambda b, h: (b, 0, r_off + h)),
            pl.BlockSpec((1, seq, hk), lambda b, h: (b, 0, h)),
            _resident((1, hv)),
        ],
        out_specs=pl.BlockSpec((1, seq, hv), lambda b, h: (b, 0, h)),
        out_shape=jax.ShapeDtypeStruct((batch, seq, dv), BF16),
        compiler_params=pltpu.CompilerParams(
            dimension_semantics=("parallel", "parallel"), vmem_limit_bytes=VMEM_LIMIT_BYTES),
        name="gla_core",
    )(proj3, proj3, proj3, proj3, gk3, g_norm.reshape(1, hv))


def _diff_attn_kernel(lq1_ref, lk1_ref, lq2_ref, lk2_ref, gn_ref, q_ref, k_ref, vt_ref, o_ref,
                      qq_ref, m_ref, acc_ref, *, lambda_init):
    tiles, tq = vt_ref.shape[1], vt_ref.shape[3]
    heads = ATTN_HEADS_PER_STEP
    d2 = vt_ref.shape[2] // heads
    d = d2 // 2
    ts = ATTN_SUB_TILE
    assert tq % ts == 0

    def head_lanes(hh):
        return slice(hh * d2, (hh + 1) * d2)

    lane = lax.broadcasted_iota(jnp.int32, (tq, d2), 1)
    for hh in range(heads):
        for t in range(tiles):
            q = q_ref[0, t * tq:(t + 1) * tq, head_lanes(hh)].astype(F32) * (d ** -0.5 * math.log2(math.e))
            qq_ref[hh * tiles + t, 0:tq, :] = jnp.where(lane < d, q, 0.0).astype(BF16)
            qq_ref[hh * tiles + t, tq:2 * tq, :] = jnp.where(lane >= d, q, 0.0).astype(BF16)
    lam = (jnp.exp(jnp.sum(lq1_ref[...] * lk1_ref[...], keepdims=True))
           - jnp.exp(jnp.sum(lq2_ref[...] * lk2_ref[...], keepdims=True)) + lambda_init)
    out_gain = gn_ref[...] * (1.0 - lambda_init)

    def query_ranges(t, kt, off):
        if kt < t or off == 0:
            return [(0, 2 * tq)]
        return [(off, tq), (tq + off, 2 * tq)]

    def scores(hh, t, kt):
        nk = ts if kt == t else tq
        pieces = []
        for off in range(0, tq, nk):
            k = k_ref[0, kt * tq + off:kt * tq + off + nk, head_lanes(hh)]
            for l0, l1 in query_ranges(t, kt, off):
                st = _dot_nt(k, qq_ref[hh * tiles + t, l0:l1, :])
                if kt == t:
                    k_pos = off + lax.broadcasted_iota(jnp.int32, st.shape, 0)
                    q_pos = (l0 + lax.broadcasted_iota(jnp.int32, st.shape, 1)) & (tq - 1)
                    st = jnp.where(k_pos <= q_pos, st, -jnp.inf)
                pieces.append((off, nk, l0, l1, st))
        return pieces

    def accumulate(pieces, hh, kt, shift, alpha):
        edges = sorted({e for _, _, l0, l1, _ in pieces for e in (l0, l1)})
        for a, b in zip(edges[:-1], edges[1:]):
            parts = [(off, nk, st[:, a - l0:b - l0]) for off, nk, l0, l1, st in pieces
                     if l0 <= a and b <= l1]
            p = jnp.concatenate([jnp.exp2(st - shift[:, a:b]).astype(BF16) for _, _, st in parts], axis=0)
            vt = jnp.concatenate([vt_ref[0, kt, head_lanes(hh), off:off + nk] for off, nk, _ in parts],
                                 axis=1)
            vt = jnp.concatenate([vt, jnp.ones((ATTN_SUM_ROWS, vt.shape[1]), BF16)], axis=0)
            prev = acc_ref[:, a:b] if alpha is None else alpha[:, a:b] * acc_ref[:, a:b]
            acc_ref[:, a:b] = prev + _dot(vt, p)

    def update_running_max(pieces, hh, t, kt):
        m_prev = m_ref[...]
        for _, _, l0, l1, st in pieces:
            m_ref[:, l0:l1] = jnp.maximum(m_ref[:, l0:l1], jnp.max(st, axis=0, keepdims=True))
        m_new = m_ref[...]
        accumulate(pieces, hh, kt, m_new, jnp.exp2(m_prev - m_new))

    def score_bounds(hh):
        ones = jnp.ones((8, d2), BF16)
        k = k_ref[0, :, head_lanes(hh)]
        k_norm2 = jnp.max(_dot_nt(ones, k * k)[0:1, :])
        bounds = []
        for t in range(tiles):
            qq = qq_ref[hh * tiles + t]
            q_norm2 = _dot_nt(ones, qq * qq)[0:1, :]
            bounds.append(jnp.sqrt(q_norm2 * k_norm2) * ATTN_BOUND_SLACK)
        return bounds

    bounds = {hh: score_bounds(hh) for hh in range(heads)}
    worst = jnp.max(jnp.concatenate([b for hh in range(heads) for b in bounds[hh]], axis=1))

    def update_fixed_shift(pieces, hh, t, kt):
        accumulate(pieces, hh, kt, bounds[hh][t], None)

    def finalize(hh, t):
        o_all = acc_ref[0:d2, :] / acc_ref[d2:d2 + 1, :]
        o_t = o_all[:, 0:tq] - lam * o_all[:, tq:2 * tq]
        inv_rms = lax.rsqrt(jnp.mean(o_t * o_t, axis=0, keepdims=True) + SUBLN_EPS)
        o_ref[0, t * tq:(t + 1) * tq, head_lanes(hh)] = (o_t * inv_rms * out_gain).T.astype(o_ref.dtype)

    units = [(hh, t, kt) for hh in range(heads) for t in range(tiles) for kt in range(t + 1)]

    def sweep(update):
        pending = {}
        for i in range(len(units) + ATTN_LOOKAHEAD):
            if i < len(units):
                pending[i] = scores(*units[i])
            j = i - ATTN_LOOKAHEAD
            if j >= 0:
                hh, t, kt = units[j]
                if kt == 0:
                    m_ref[...] = jnp.full_like(m_ref, -jnp.inf)
                    acc_ref[...] = jnp.zeros_like(acc_ref)
                update(pending.pop(j), hh, t, kt)
                if kt == t:
                    finalize(hh, t)

    bound_is_small = worst <= ATTN_MAX_FIXED_SHIFT

    @pl.when(bound_is_small)
    def _():
        sweep(update_fixed_shift)

    @pl.when(jnp.logical_not(bound_is_small))
    def _():
        sweep(update_running_max)


def _diff_attn(qk, v_t, lam_q1, lam_k1, lam_q2, lam_k2, g_norm, batch, seq, lambda_init):
    d_model = qk.shape[1] // 2
    d2 = d_model // DIFF_HEADS
    d = d2 // 2
    tq = ATTN_TILE
    tiles = seq // tq
    heads = ATTN_HEADS_PER_STEP
    assert seq % tq == 0 and g_norm.shape[0] == d2 and v_t.shape == (batch, tiles, d_model, tq)
    assert DIFF_HEADS % heads == 0
    qk3 = qk.reshape(batch, seq, 2 * d_model)
    k_off = DIFF_HEADS // heads
    return pl.pallas_call(
        functools.partial(_diff_attn_kernel, lambda_init=lambda_init),
        grid=(batch, DIFF_HEADS // heads),
        in_specs=[_resident((1, d))] * 4 + [
            _resident((d2, 1)),
            pl.BlockSpec((1, seq, heads * d2), lambda b, h: (b, 0, h)),
            pl.BlockSpec((1, seq, heads * d2), lambda b, h: (b, 0, k_off + h)),
            pl.BlockSpec((1, tiles, heads * d2, tq), lambda b, h: (b, 0, h, 0)),
        ],
        out_specs=pl.BlockSpec((1, seq, heads * d2), lambda b, h: (b, 0, h)),
        out_shape=jax.ShapeDtypeStruct((batch, seq, d_model), BF16),
        scratch_shapes=[
            pltpu.VMEM((heads * tiles, 2 * tq, d2), BF16),
            pltpu.VMEM((1, 2 * tq), F32),
            pltpu.VMEM((d2 + ATTN_SUM_ROWS, 2 * tq), F32),
        ],
        compiler_params=pltpu.CompilerParams(
            dimension_semantics=("parallel", "parallel"), vmem_limit_bytes=VMEM_LIMIT_BYTES),
        name="diff_attn",
    )(lam_q1.reshape(1, d), lam_k1.reshape(1, d), lam_q2.reshape(1, d), lam_k2.reshape(1, d),
      g_norm.reshape(d2, 1), qk3, qk3, v_t)


def _mix_out_ffn_kernel(x_ref, a_ref, wo_ref, gf_ref, wi_ref, w2_ref, gl_ref, o_ref, acc_ref, *, final_norm):
    d_ff = w2_ref.shape[0]
    x1 = x_ref[...] + _dot(a_ref[...], wo_ref[...])
    h = _rmsnorm_rows(x1, gf_ref[...], NORM_EPS).astype(BF16)
    acc_ref[...] = x1
    for f0 in range(0, d_ff, FFN_COL_TILE):
        gate = _dot(h, wi_ref[:, f0:f0 + FFN_COL_TILE])
        up = _dot(h, wi_ref[:, d_ff + f0:d_ff + f0 + FFN_COL_TILE])
        act = (gate * jax.nn.sigmoid(gate) * up).astype(BF16)
        acc_ref[...] += _dot(act, w2_ref[f0:f0 + FFN_COL_TILE, :])
    x2 = acc_ref[...]
    if final_norm:
        x2 = _rmsnorm_rows(x2, gl_ref[...], NORM_EPS)
    o_ref[...] = x2


def _mix_out_ffn(x2d, mix, w_out, layer, g_ffn_all, w_in_all, w2_all, g_last, final_norm):
    t, d = x2d.shape
    d_ff = w2_all.shape[1]
    assert t % FFN_ROW_TILE == 0 and d_ff % FFN_COL_TILE == 0
    row_spec = pl.BlockSpec((FFN_ROW_TILE, d), lambda i: (i, 0))
    return pl.pallas_call(
        functools.partial(_mix_out_ffn_kernel, final_norm=final_norm),
        grid=(t // FFN_ROW_TILE,),
        in_specs=[row_spec, row_spec, _resident((d, d)), _resident((1, d), layer),
                  _resident((d, 2 * d_ff), layer), _resident((d_ff, d), layer), _resident((1, d))],
        out_specs=row_spec,
        out_shape=jax.ShapeDtypeStruct((t, d), F32),
        scratch_shapes=[pltpu.VMEM((FFN_ROW_TILE, d), F32)],
        compiler_params=pltpu.CompilerParams(
            dimension_semantics=("parallel",), vmem_limit_bytes=VMEM_LIMIT_BYTES),
        name="mix_out_ffn",
    )(x2d, mix, w_out.astype(BF16), g_ffn_all.reshape(-1, 1, d), w_in_all, w2_all, g_last.reshape(1, d))


def kernel(x, gla_w_in, gla_w_gate_a, gla_w_gate_b, gla_b_gate, gla_norm, gla_w_out, diff_w_in, diff_lam_q1, diff_lam_k1, diff_lam_q2, diff_lam_k2, diff_norm, diff_w_out, norm_mixer, norm_ffn, ffn_w_in, ffn_w_out, norm_final):
    batch, seq, d = x.shape
    x2d = x.reshape(batch * seq, d)
    ffn_w_in_bf, ffn_w_out_bf = ffn_w_in.astype(BF16), ffn_w_out.astype(BF16)

    proj, gk = _norm_proj_gate(x2d, norm_mixer[0], gla_w_in[0],
                               gla_w_gate_a[0], gla_w_gate_b[0], gla_b_gate[0])
    mix = _gla_core(proj, gk, gla_norm[0], batch, seq).reshape(batch * seq, -1)
    x2d = _mix_out_ffn(x2d, mix, gla_w_out[0], 0, norm_ffn, ffn_w_in_bf, ffn_w_out_bf, norm_final,
                       final_norm=False)

    lambda_init = 0.8 - 0.6 * math.exp(-0.3 * 1)
    qk, v_t = _norm_proj_vt(x2d, norm_mixer[1], diff_w_in[0], 2 * d, batch, seq)
    mix = _diff_attn(qk, v_t, diff_lam_q1[0], diff_lam_k1[0], diff_lam_q2[0], diff_lam_k2[0],
                     diff_norm[0], batch, seq, lambda_init).reshape(batch * seq, -1)
    x2d = _mix_out_ffn(x2d, mix, diff_w_out[0], 1, norm_ffn, ffn_w_in_bf, ffn_w_out_bf, norm_final,
                       final_norm=True)
    return x2d.reshape(batch, seq, d)
```

```python
import functools
import math

import jax
import jax.numpy as jnp
from jax import lax
from jax.experimental import pallas as pl
from jax.experimental.pallas import tpu as pltpu

F32 = jnp.float32
BF16 = jnp.bfloat16

NORM_EPS = 1e-6
SUBLN_EPS = 1e-5

GLA_HEADS = 4
GLA_GATE_RANK = 16
GLA_TAU = 16.0
GLA_CHUNK = 64

DIFF_HEADS = 8

V7X_VMEM_BYTES = 64 * 1024 * 1024
VMEM_LIMIT_BYTES = V7X_VMEM_BYTES - 8 * 1024 * 1024

PROJ_ROW_TILE = 1024
FFN_ROW_TILE = 1024
PROJ_COL_TILE = 1024
FFN_COL_TILE = 256
GLA_GROUP = 256
GLA_MAX_FACTORED_DECAY = 48.0
ATTN_TILE = 512
ATTN_SUB_TILE = 256
ATTN_SUM_ROWS = 16
ATTN_LOOKAHEAD = 2
ATTN_HEADS_PER_STEP = 2
ATTN_MAX_FIXED_SHIFT = 40.0
ATTN_BOUND_SLACK = 1.02


def _resident(shape, layer=None):
    if layer is None:
        index, block = (0,) * len(shape), tuple(shape)
    else:
        index, block = (layer,) + (0,) * len(shape), (None,) + tuple(shape)
    return pl.BlockSpec(block, lambda *_: index, pipeline_mode=pl.Buffered(1))


def _rmsnorm_rows(x, gain, eps):
    return x * lax.rsqrt(jnp.mean(x * x, axis=-1, keepdims=True) + eps) * gain


def _dot(a, b):
    return jnp.dot(a, b, preferred_element_type=F32)


def _dot_nt(a, b):
    return lax.dot_general(a, b, (((1,), (1,)), ((), ())), preferred_element_type=F32)


def _dot_tn(a, b):
    return lax.dot_general(a, b, (((0,), (0,)), ((), ())), preferred_element_type=F32)


def _project(h, w_ref, o_ref):
    for c0 in range(0, o_ref.shape[1], PROJ_COL_TILE):
        cols = slice(c0, c0 + PROJ_COL_TILE)
        o_ref[:, cols] = _dot(h, w_ref[:, cols]).astype(o_ref.dtype)


def _norm_proj_gate_kernel(x_ref, g_ref, w_ref, wa_ref, wb_ref, bg_ref, o_ref, gk_ref):
    h = _rmsnorm_rows(x_ref[...], g_ref[...], NORM_EPS).astype(BF16)
    low = _dot(h, wa_ref[...])
    logits = _dot(low.astype(BF16), wb_ref[...]) + bg_ref[...]
    log_sig = jnp.minimum(logits, 0.0) - jnp.log(1.0 + jnp.exp(-jnp.abs(logits)))
    gk_ref[...] = log_sig * (math.log2(math.e) / GLA_TAU)
    _project(h, w_ref, o_ref)


def _norm_proj_vt_kernel(x_ref, g_ref, w_ref, wv_ref, o_ref, vt_ref, wvt_ref):
    @pl.when(pl.program_id(0) == 0)
    def _():
        wvt_ref[...] = wv_ref[...].T

    h = _rmsnorm_rows(x_ref[...], g_ref[...], NORM_EPS).astype(BF16)
    _project(h, w_ref, o_ref)
    tile = vt_ref.shape[3]
    for j in range(vt_ref.shape[1]):
        vt_ref[0, j] = _dot_nt(wvt_ref[...], h[j * tile:(j + 1) * tile]).astype(vt_ref.dtype)


def _norm_proj_call(kernel, name, x2d, gain, w_bf, n_out, extra_in, extra_specs, extra_out_shape,
                    extra_out_spec, scratch_shapes=(), semantics="parallel"):
    t, d = x2d.shape
    assert t % PROJ_ROW_TILE == 0 and n_out % PROJ_COL_TILE == 0 and w_bf.shape[1] >= n_out
    return pl.pallas_call(
        kernel,
        grid=(t // PROJ_ROW_TILE,),
        in_specs=[pl.BlockSpec((PROJ_ROW_TILE, d), lambda i: (i, 0)), _resident((1, d)),
                  _resident((d, n_out))] + extra_specs,
        out_specs=[pl.BlockSpec((PROJ_ROW_TILE, n_out), lambda i: (i, 0)), extra_out_spec],
        out_shape=[jax.ShapeDtypeStruct((t, n_out), BF16), extra_out_shape],
        scratch_shapes=list(scratch_shapes),
        compiler_params=pltpu.CompilerParams(
            dimension_semantics=(semantics,), vmem_limit_bytes=VMEM_LIMIT_BYTES),
        name=name,
    )(x2d, gain.reshape(1, d), w_bf, *extra_in)


def _norm_proj_gate(x2d, gain, w, w_a, w_b, b_g):
    t, d = x2d.shape
    rank, dk = w_b.shape
    return _norm_proj_call(
        _norm_proj_gate_kernel, "norm_proj_gate", x2d, gain, w.astype(BF16), w.shape[1],
        [w_a.astype(BF16), w_b.astype(BF16), b_g.reshape(1, dk)],
        [_resident((d, rank)), _resident((rank, dk)), _resident((1, dk))],
        jax.ShapeDtypeStruct((t, dk), F32), pl.BlockSpec((PROJ_ROW_TILE, dk), lambda i: (i, 0)))


def _norm_proj_vt(x2d, gain, w, n_qk, batch, seq):
    t, d = x2d.shape
    d_v = w.shape[1] - n_qk
    assert n_qk % d_v == 0 and seq % PROJ_ROW_TILE == 0 and PROJ_ROW_TILE % ATTN_TILE == 0
    steps = seq // PROJ_ROW_TILE
    per_step = PROJ_ROW_TILE // ATTN_TILE
    w_bf = w.astype(BF16)
    value_cols = pl.BlockSpec((d, d_v), lambda i: (0, n_qk // d_v), pipeline_mode=pl.Buffered(1))
    return _norm_proj_call(
        _norm_proj_vt_kernel, "norm_proj_vt", x2d, gain, w_bf, n_qk, [w_bf], [value_cols],
        jax.ShapeDtypeStruct((batch, seq // ATTN_TILE, d_v, ATTN_TILE), BF16),
        pl.BlockSpec((1, per_step, d_v, ATTN_TILE), lambda i: (i // steps, i % steps, 0, 0)),
        scratch_shapes=[pltpu.VMEM((d_v, d), BF16)], semantics="arbitrary")


def _gla_kernel(q_ref, k_ref, v_ref, r_ref, gk_ref, gn_ref, o_ref):
    c = GLA_CHUNK
    grp = GLA_GROUP
    n = grp // c
    seq, hk = q_ref.shape[1], q_ref.shape[2]
    hv = v_ref.shape[2]
    shift = c.bit_length() - 1
    row = lax.broadcasted_iota(jnp.int32, (grp, grp), 0)
    col = lax.broadcasted_iota(jnp.int32, (grp, grp), 1)
    mask = (col <= row) & ((row >> shift) == (col >> shift))
    tri = jnp.where(mask, 1.0, 0.0).astype(BF16)
    gain = gn_ref[...]
    q_scale = hk ** -0.5

    def rows_of(gi):
        return slice(gi * grp, (gi + 1) * grp)

    def cum_decay(gi):
        g = gk_ref[0, rows_of(gi), :]
        g_hi = g.astype(BF16)
        g_lo = (g - g_hi.astype(F32)).astype(BF16)
        return _dot(tri, g_hi) + _dot(tri, g_lo)

    def scaled_operands(gi, b):
        b = b.reshape(n, c, hk)
        b_mid = b[:, c // 2:c // 2 + 1, :]
        b_last = b[:, c - 1:c, :]
        q = (q_ref[0, rows_of(gi), :].astype(F32) * q_scale).reshape(n, c, hk)
        k = k_ref[0, rows_of(gi), :].astype(F32).reshape(n, c, hk)
        q_mid = (q * jnp.exp2(b - b_mid)).reshape(grp, hk).astype(BF16)
        k_mid = (k * jnp.exp2(b_mid - b)).reshape(grp, hk).astype(BF16)
        q_dec = (q * jnp.exp2(b)).astype(BF16)
        k_end = (k * jnp.exp2(b_last - b)).astype(BF16)
        return q_mid, k_mid, q_dec, k_end, jnp.exp2(b_last)

    def state_part(gi, o_intra, q_dec, decay, incs, state_t):
        outs = []
        for ci in range(n):
            outs.append(o_intra[ci * c:(ci + 1) * c] + _dot_nt(q_dec[ci], state_t.astype(BF16)))
            state_t = state_t * decay[ci] + incs[ci]
        o = _rmsnorm_rows(jnp.concatenate(outs, axis=0), gain, NORM_EPS)
        r = r_ref[0, rows_of(gi), :].astype(F32)
        o_ref[0, rows_of(gi), :] = (o * (r * jax.nn.sigmoid(r))).astype(o_ref.dtype)
        return state_t

    def factored_sweep():
        groups = seq // grp
        state_t = jnp.zeros((hv, hk), F32)
        cum, ops = {}, {}
        for i in range(-2, groups):
            if i >= 0:
                q_mid, k_mid, q_dec, k_end, decay = ops.pop(i)
                v = v_ref[0, rows_of(i), :]
                attn = jnp.where(mask, _dot_nt(q_mid, k_mid), 0.0).astype(BF16)
                incs = [_dot_tn(v[ci * c:(ci + 1) * c], k_end[ci]) for ci in range(n)]
            if i + 2 < groups:
                cum[i + 2] = cum_decay(i + 2)
            if i >= 0:
                o_intra = _dot(attn, v)
            if 0 <= i + 1 < groups:
                ops[i + 1] = scaled_operands(i + 1, cum.pop(i + 1))
            if i >= 0:
                state_t = state_part(i, o_intra, q_dec, decay, incs, state_t)

    def direct_sweep():
        causal = (lax.broadcasted_iota(jnp.int32, (c, c), 1)
                  <= lax.broadcasted_iota(jnp.int32, (c, c), 0))
        tri_c = jnp.where(causal, 1.0, 0.0)
        assert hk >= c
        col_id = lax.broadcasted_iota(jnp.int32, (c, hk), 1)

        def chunk_step(ci, state_t):
            rows = pl.ds(pl.multiple_of(ci * c, c), c)
            b = jnp.dot(tri_c, gk_ref[0, rows, :], preferred_element_type=F32,
                        precision=lax.Precision.HIGHEST)
            q = q_ref[0, rows, :].astype(F32) * q_scale
            k = k_ref[0, rows, :].astype(F32)
            v = v_ref[0, rows, :]
            attn = jnp.zeros((c, hk), F32)
            for j in range(c):
                weight = jnp.exp2(jnp.minimum(b - b[j:j + 1, :], 0.0))
                column = jnp.sum(q * weight * k[j:j + 1, :], axis=-1, keepdims=True)
                attn = jnp.where(col_id == j, column, attn)
            attn = jnp.where(causal, attn[:, 0:c], 0.0).astype(BF16)
            b_last = b[c - 1:c, :]
            o = _dot(attn, v) + _dot_nt((q * jnp.exp2(b)).astype(BF16), state_t.astype(BF16))
            k_end = (k * jnp.exp2(b_last - b)).astype(BF16)
            state_t = state_t * jnp.exp2(b_last) + _dot_tn(v, k_end)
            o = _rmsnorm_rows(o, gain, NORM_EPS)
            r = r_ref[0, rows, :].astype(F32)
            o_ref[0, rows, :] = (o * (r * jax.nn.sigmoid(r))).astype(o_ref.dtype)
            return state_t

        lax.fori_loop(0, seq // c, chunk_step, jnp.zeros((hv, hk), F32))

    chunk_decay = [-jnp.sum(gk_ref[0, r0:r0 + c, :], axis=0, keepdims=True) for r0 in range(0, seq, c)]
    worst_decay = jnp.max(functools.reduce(jnp.maximum, chunk_decay))
    decay_is_moderate = worst_decay <= GLA_MAX_FACTORED_DECAY

    @pl.when(decay_is_moderate)
    def _():
        factored_sweep()

    @pl.when(jnp.logical_not(decay_is_moderate))
    def _():
        direct_sweep()


def _gla_core(proj, gk, g_norm, batch, seq):
    dk = gk.shape[1]
    hk = dk // GLA_HEADS
    hv = g_norm.shape[0]
    dv = hv * GLA_HEADS
    assert proj.shape[1] == 2 * dk + 2 * dv and seq % GLA_GROUP == 0 and GLA_GROUP % GLA_CHUNK == 0
    proj3 = proj.reshape(batch, seq, proj.shape[1])
    gk3 = gk.reshape(batch, seq, dk)
    k_off, v_off, r_off = dk // hk, (2 * dk) // hv, (2 * dk + dv) // hv
    return pl.pallas_call(
        _gla_kernel,
        grid=(batch, GLA_HEADS),
        in_specs=[
            pl.BlockSpec((1, seq, hk), lambda b, h: (b, 0, h)),
            pl.BlockSpec((1, seq, hk), lambda b, h: (b, 0, k_off + h)),
            pl.BlockSpec((1, seq, hv), lambda b, h: (b, 0, v_off + h)),
            pl.BlockSpec((1, seq, hv), lambda b, h: (b, 0, r_off + h)),
            pl.BlockSpec((1, seq, hk), lambda b, h: (b, 0, h)),
            _resident((1, hv)),
        ],
        out_specs=pl.BlockSpec((1, seq, hv), lambda b, h: (b, 0, h)),
        out_shape=jax.ShapeDtypeStruct((batch, seq, dv), BF16),
        compiler_params=pltpu.CompilerParams(
            dimension_semantics=("parallel", "parallel"), vmem_limit_bytes=VMEM_LIMIT_BYTES),
        name="gla_core",
    )(proj3, proj3, proj3, proj3, gk3, g_norm.reshape(1, hv))


def _diff_attn_kernel(lq1_ref, lk1_ref, lq2_ref, lk2_ref, gn_ref, q_ref, k_ref, vt_ref, o_ref,
                      qq_ref, m_ref, acc_ref, *, lambda_init):
    tiles, tq = vt_ref.shape[1], vt_ref.shape[3]
    heads = ATTN_HEADS_PER_STEP
    d2 = vt_ref.shape[2] // heads
    d = d2 // 2
    ts = ATTN_SUB_TILE
    assert tq % ts == 0

    def head_lanes(hh):
        return slice(hh * d2, (hh + 1) * d2)

    lane = lax.broadcasted_iota(jnp.int32, (tq, d2), 1)
    for hh in range(heads):
        for t in range(tiles):
            q = q_ref[0, t * tq:(t + 1) * tq, head_lanes(hh)].astype(F32) * (d ** -0.5 * math.log2(math.e))
            qq_ref[hh * tiles + t, 0:tq, :] = jnp.where(lane < d, q, 0.0).astype(BF16)
            qq_ref[hh * tiles + t, tq:2 * tq, :] = jnp.where(lane >= d, q, 0.0).astype(BF16)
    lam = (jnp.exp(jnp.sum(lq1_ref[...] * lk1_ref[...], keepdims=True))
           - jnp.exp(jnp.sum(lq2_ref[...] * lk2_ref[...], keepdims=True)) + lambda_init)
    out_gain = gn_ref[...] * (1.0 - lambda_init)

    def query_ranges(t, kt, off):
        if kt < t or off == 0:
            return [(0, 2 * tq)]
        return [(off, tq), (tq + off, 2 * tq)]

    def scores(hh, t, kt):
        nk = ts if kt == t else tq
        pieces = []
        for off in range(0, tq, nk):
            k = k_ref[0, kt * tq + off:kt * tq + off + nk, head_lanes(hh)]
            for l0, l1 in query_ranges(t, kt, off):
                st = _dot_nt(k, qq_ref[hh * tiles + t, l0:l1, :])
                if kt == t:
                    k_pos = off + lax.broadcasted_iota(jnp.int32, st.shape, 0)
                    q_pos = (l0 + lax.broadcasted_iota(jnp.int32, st.shape, 1)) & (tq - 1)
                    st = jnp.where(k_pos <= q_pos, st, -jnp.inf)
                pieces.append((off, nk, l0, l1, st))
        return pieces

    def accumulate(pieces, hh, kt, shift, alpha):
        edges = sorted({e for _, _, l0, l1, _ in pieces for e in (l0, l1)})
        for a, b in zip(edges[:-1], edges[1:]):
            parts = [(off, nk, st[:, a - l0:b - l0]) for off, nk, l0, l1, st in pieces
                     if l0 <= a and b <= l1]
            p = jnp.concatenate([jnp.exp2(st - shift[:, a:b]).astype(BF16) for _, _, st in parts], axis=0)
            vt = jnp.concatenate([vt_ref[0, kt, head_lanes(hh), off:off + nk] for off, nk, _ in parts],
                                 axis=1)
            vt = jnp.concatenate([vt, jnp.ones((ATTN_SUM_ROWS, vt.shape[1]), BF16)], axis=0)
            prev = acc_ref[:, a:b] if alpha is None else alpha[:, a:b] * acc_ref[:, a:b]
            acc_ref[:, a:b] = prev + _dot(vt, p)

    def update_running_max(pieces, hh, t, kt):
        m_prev = m_ref[...]
        for _, _, l0, l1, st in pieces:
            m_ref[:, l0:l1] = jnp.maximum(m_ref[:, l0:l1], jnp.max(st, axis=0, keepdims=True))
        m_new = m_ref[...]
        accumulate(pieces, hh, kt, m_new, jnp.exp2(m_prev - m_new))

    def score_bounds(hh):
        ones = jnp.ones((8, d2), BF16)
        k = k_ref[0, :, head_lanes(hh)]
        k_norm2 = jnp.max(_dot_nt(ones, k * k)[0:1, :])
        bounds = []
        for t in range(tiles):
            qq = qq_ref[hh * tiles + t]
            q_norm2 = _dot_nt(ones, qq * qq)[0:1, :]
            bounds.append(jnp.sqrt(q_norm2 * k_norm2) * ATTN_BOUND_SLACK)
        return bounds

    bounds = {hh: score_bounds(hh) for hh in range(heads)}
    worst = jnp.max(jnp.concatenate([b for hh in range(heads) for b in bounds[hh]], axis=1))

    def update_fixed_shift(pieces, hh, t, kt):
        accumulate(pieces, hh, kt, bounds[hh][t], None)

    def finalize(hh, t):
        o_all = acc_ref[0:d2, :] / acc_ref[d2:d2 + 1, :]
        o_t = o_all[:, 0:tq] - lam * o_all[:, tq:2 * tq]
        inv_rms = lax.rsqrt(jnp.mean(o_t * o_t, axis=0, keepdims=True) + SUBLN_EPS)
        o_ref[0, t * tq:(t + 1) * tq, head_lanes(hh)] = (o_t * inv_rms * out_gain).T.astype(o_ref.dtype)

    units = [(hh, t, kt) for hh in range(heads) for t in range(tiles) for kt in range(t + 1)]

    def sweep(update):
        pending = {}
        for i in range(len(units) + ATTN_LOOKAHEAD):
            if i < len(units):
                pending[i] = scores(*units[i])
            j = i - ATTN_LOOKAHEAD
            if j >= 0:
                hh, t, kt = units[j]
                if kt == 0:
                    m_ref[...] = jnp.full_like(m_ref, -jnp.inf)
                    acc_ref[...] = jnp.zeros_like(acc_ref)
                update(pending.pop(j), hh, t, kt)
                if kt == t:
                    finalize(hh, t)

    bound_is_small = worst <= ATTN_MAX_FIXED_SHIFT

    @pl.when(bound_is_small)
    def _():
        sweep(update_fixed_shift)

    @pl.when(jnp.logical_not(bound_is_small))
    def _():
        sweep(update_running_max)


def _diff_attn(qk, v_t, lam_q1, lam_k1, lam_q2, lam_k2, g_norm, batch, seq, lambda_init):
    d_model = qk.shape[1] // 2
    d2 = d_model // DIFF_HEADS
    d = d2 // 2
    tq = ATTN_TILE
    tiles = seq // tq
    heads = ATTN_HEADS_PER_STEP
    assert seq % tq == 0 and g_norm.shape[0] == d2 and v_t.shape == (batch, tiles, d_model, tq)
    assert DIFF_HEADS % heads == 0
    qk3 = qk.reshape(batch, seq, 2 * d_model)
    k_off = DIFF_HEADS // heads
    return pl.pallas_call(
        functools.partial(_diff_attn_kernel, lambda_init=lambda_init),
        grid=(batch, DIFF_HEADS // heads),
        in_specs=[_resident((1, d))] * 4 + [
            _resident((d2, 1)),
            pl.BlockSpec((1, seq, heads * d2), lambda b, h: (b, 0, h)),
            pl.BlockSpec((1, seq, heads * d2), lambda b, h: (b, 0, k_off + h)),
            pl.BlockSpec((1, tiles, heads * d2, tq), lambda b, h: (b, 0, h, 0)),
        ],
        out_specs=pl.BlockSpec((1, seq, heads * d2), lambda b, h: (b, 0, h)),
        out_shape=jax.ShapeDtypeStruct((batch, seq, d_model), BF16),
        scratch_shapes=[
            pltpu.VMEM((heads * tiles, 2 * tq, d2), BF16),
            pltpu.VMEM((1, 2 * tq), F32),
            pltpu.VMEM((d2 + ATTN_SUM_ROWS, 2 * tq), F32),
        ],
        compiler_params=pltpu.CompilerParams(
            dimension_semantics=("parallel", "parallel"), vmem_limit_bytes=VMEM_LIMIT_BYTES),
        name="diff_attn",
    )(lam_q1.reshape(1, d), lam_k1.reshape(1, d), lam_q2.reshape(1, d), lam_k2.reshape(1, d),
      g_norm.reshape(d2, 1), qk3, qk3, v_t)


def _mix_out_ffn_kernel(x_ref, a_ref, wo_ref, gf_ref, wi_ref, w2_ref, gl_ref, o_ref, acc_ref, *, final_norm):
    d_ff = w2_ref.shape[0]
    x1 = x_ref[...] + _dot(a_ref[...], wo_ref[...])
    h = _rmsnorm_rows(x1, gf_ref[...], NORM_EPS).astype(BF16)
    acc_ref[...] = x1
    for f0 in range(0, d_ff, FFN_COL_TILE):
        gate = _dot(h, wi_ref[:, f0:f0 + FFN_COL_TILE])
        up = _dot(h, wi_ref[:, d_ff + f0:d_ff + f0 + FFN_COL_TILE])
        act = (gate * jax.nn.sigmoid(gate) * up).astype(BF16)
        acc_ref[...] += _dot(act, w2_ref[f0:f0 + FFN_COL_TILE, :])
    x2 = acc_ref[...]
    if final_norm:
        x2 = _rmsnorm_rows(x2, gl_ref[...], NORM_EPS)
    o_ref[...] = x2


def _mix_out_ffn(x2d, mix, w_out, layer, g_ffn_all, w_in_all, w2_all, g_last, final_norm):
    t, d = x2d.shape
    d_ff = w2_all.shape[1]
    assert t % FFN_ROW_TILE == 0 and d_ff % FFN_COL_TILE == 0
    row_spec = pl.BlockSpec((FFN_ROW_TILE, d), lambda i: (i, 0))
    return pl.pallas_call(
        functools.partial(_mix_out_ffn_kernel, final_norm=final_norm),
        grid=(t // FFN_ROW_TILE,),
        in_specs=[row_spec, row_spec, _resident((d, d)), _resident((1, d), layer),
                  _resident((d, 2 * d_ff), layer), _resident((d_ff, d), layer), _resident((1, d))],
        out_specs=row_spec,
        out_shape=jax.ShapeDtypeStruct((t, d), F32),
        scratch_shapes=[pltpu.VMEM((FFN_ROW_TILE, d), F32)],
        compiler_params=pltpu.CompilerParams(
            dimension_semantics=("parallel",), vmem_limit_bytes=VMEM_LIMIT_BYTES),
        name="mix_out_ffn",
    )(x2d, mix, w_out.astype(BF16), g_ffn_all.reshape(-1, 1, d), w_in_all, w2_all, g_last.reshape(1, d))


def kernel(x, gla_w_in, gla_w_gate_a, gla_w_gate_b, gla_b_gate, gla_norm, gla_w_out, diff_w_in, diff_lam_q1, diff_lam_k1, diff_lam_q2, diff_lam_k2, diff_norm, diff_w_out, norm_mixer, norm_ffn, ffn_w_in, ffn_w_out, norm_final):
    batch, seq, d = x.shape
    x2d = x.reshape(batch * seq, d)
    ffn_w_in_bf, ffn_w_out_bf = ffn_w_in.astype(BF16), ffn_w_out.astype(BF16)

    proj, gk = _norm_proj_gate(x2d, norm_mixer[0], gla_w_in[0],
                               gla_w_gate_a[0], gla_w_gate_b[0], gla_b_gate[0])
    mix = _gla_core(proj, gk, gla_norm[0], batch, seq).reshape(batch * seq, -1)
    x2d = _mix_out_ffn(x2d, mix, gla_w_out[0], 0, norm_ffn, ffn_w_in_bf, ffn_w_out_bf, norm_final,
                       final_norm=False)

    lambda_init = 0.8 - 0.6 * math.exp(-0.3 * 1)
    qk, v_t = _norm_proj_vt(x2d, norm_mixer[1], diff_w_in[0], 2 * d, batch, seq)
    mix = _diff_attn(qk, v_t, diff_lam_q1[0], diff_lam_k1[0], diff_lam_q2[0], diff_lam_k2[0],
                     diff_norm[0], batch, seq, lambda_init).reshape(batch * seq, -1)
    x2d = _mix_out_ffn(x2d, mix, diff_w_out[0], 1, norm_ffn, ffn_w_in_bf, ffn_w_out_bf, norm_final,
                       final_norm=True)
    return x2d.reshape(batch, seq, d)
```

```python
import functools
import math

import jax
import jax.numpy as jnp
from jax import lax
from jax.experimental import pallas as pl
from jax.experimental.pallas import tpu as pltpu

F32 = jnp.float32
BF16 = jnp.bfloat16

NORM_EPS = 1e-6
SUBLN_EPS = 1e-5

GLA_HEADS = 4
GLA_TAU = 16.0
GLA_CHUNK = 64

DIFF_HEADS = 8

V7X_VMEM_BYTES = 64 * 1024 * 1024
VMEM_LIMIT_BYTES = V7X_VMEM_BYTES - 8 * 1024 * 1024

PROJ_ROW_TILE = 1024
FFN_ROW_TILE = 1024
PROJ_COL_TILE = 1024
FFN_COL_TILE = 256
GLA_GROUP = 256
GLA_MAX_FACTORED_DECAY = 48.0
ATTN_TILE = 512
ATTN_SUB_TILE = 256
ATTN_SUM_ROWS = 16
ATTN_LOOKAHEAD = 2
ATTN_HEADS_PER_STEP = 2
ATTN_MAX_FIXED_SHIFT = 40.0
ATTN_BOUND_SLACK = 1.02


def _resident(shape, layer=None):
    if layer is None:
        index, block = (0,) * len(shape), tuple(shape)
    else:
        index, block = (layer,) + (0,) * len(shape), (None,) + tuple(shape)
    return pl.BlockSpec(block, lambda *_: index, pipeline_mode=pl.Buffered(1))


def _rmsnorm_rows(x, gain, eps):
    return x * lax.rsqrt(jnp.mean(x * x, axis=-1, keepdims=True) + eps) * gain


def _dot(a, b):
    return jnp.dot(a, b, preferred_element_type=F32)


def _dot_nt(a, b):
    return lax.dot_general(a, b, (((1,), (1,)), ((), ())), preferred_element_type=F32)


def _dot_tn(a, b):
    return lax.dot_general(a, b, (((0,), (0,)), ((), ())), preferred_element_type=F32)


def _project(h, w_ref, o_ref):
    for c0 in range(0, o_ref.shape[1], PROJ_COL_TILE):
        cols = slice(c0, c0 + PROJ_COL_TILE)
        o_ref[:, cols] = _dot(h, w_ref[:, cols]).astype(o_ref.dtype)


def _norm_proj_gate_kernel(x_ref, g_ref, w_ref, wa_ref, wb_ref, bg_ref, o_ref, gk_ref, cd_ref):
    h = _rmsnorm_rows(x_ref[...], g_ref[...], NORM_EPS).astype(BF16)
    low = _dot(h, wa_ref[...])
    logits = _dot(low.astype(BF16), wb_ref[...]) + bg_ref[...]
    log_sig = jnp.minimum(logits, 0.0) - jnp.log(1.0 + jnp.exp(-jnp.abs(logits)))
    log2_decay = log_sig * (math.log2(math.e) / GLA_TAU)
    gk_ref[...] = log2_decay
    chunks, dk = cd_ref.shape
    cd_ref[...] = -jnp.sum(log2_decay.reshape(chunks, GLA_CHUNK, dk), axis=1)
    _project(h, w_ref, o_ref)


def _norm_proj_vt_kernel(x_ref, g_ref, w_ref, wv_ref, o_ref, vt_ref, wvt_ref):
    @pl.when(pl.program_id(0) == 0)
    def _():
        wvt_ref[...] = wv_ref[...].T

    h = _rmsnorm_rows(x_ref[...], g_ref[...], NORM_EPS).astype(BF16)
    _project(h, w_ref, o_ref)
    tile = vt_ref.shape[3]
    for j in range(vt_ref.shape[1]):
        vt_ref[0, j] = _dot_nt(wvt_ref[...], h[j * tile:(j + 1) * tile]).astype(vt_ref.dtype)


def _norm_proj_call(kernel, name, x2d, gain, w_bf, n_out, extra_in, extra_specs, extra_out_shapes,
                    extra_out_specs, scratch_shapes=(), semantics="parallel"):
    t, d = x2d.shape
    assert t % PROJ_ROW_TILE == 0 and n_out % PROJ_COL_TILE == 0 and w_bf.shape[1] >= n_out
    return pl.pallas_call(
        kernel,
        grid=(t // PROJ_ROW_TILE,),
        in_specs=[pl.BlockSpec((PROJ_ROW_TILE, d), lambda i: (i, 0)), _resident((1, d)),
                  _resident((d, n_out))] + extra_specs,
        out_specs=[pl.BlockSpec((PROJ_ROW_TILE, n_out), lambda i: (i, 0))] + extra_out_specs,
        out_shape=[jax.ShapeDtypeStruct((t, n_out), BF16)] + extra_out_shapes,
        scratch_shapes=list(scratch_shapes),
        compiler_params=pltpu.CompilerParams(
            dimension_semantics=(semantics,), vmem_limit_bytes=VMEM_LIMIT_BYTES),
        name=name,
    )(x2d, gain.reshape(1, d), w_bf, *extra_in)


def _norm_proj_gate(x2d, gain, w, w_a, w_b, b_g):
    t, d = x2d.shape
    rank, dk = w_b.shape
    chunks = PROJ_ROW_TILE // GLA_CHUNK
    return _norm_proj_call(
        _norm_proj_gate_kernel, "norm_proj_gate", x2d, gain, w.astype(BF16), w.shape[1],
        [w_a.astype(BF16), w_b.astype(BF16), b_g.reshape(1, dk)],
        [_resident((d, rank)), _resident((rank, dk)), _resident((1, dk))],
        [jax.ShapeDtypeStruct((t, dk), F32), jax.ShapeDtypeStruct((t // GLA_CHUNK, dk), F32)],
        [pl.BlockSpec((PROJ_ROW_TILE, dk), lambda i: (i, 0)), pl.BlockSpec((chunks, dk), lambda i: (i, 0))])


def _norm_proj_vt(x2d, gain, w, n_qk, batch, seq):
    t, d = x2d.shape
    d_v = w.shape[1] - n_qk
    assert n_qk % d_v == 0 and seq % PROJ_ROW_TILE == 0 and PROJ_ROW_TILE % ATTN_TILE == 0
    steps = seq // PROJ_ROW_TILE
    per_step = PROJ_ROW_TILE // ATTN_TILE
    w_bf = w.astype(BF16)
    value_cols = pl.BlockSpec((d, d_v), lambda i: (0, n_qk // d_v), pipeline_mode=pl.Buffered(1))
    return _norm_proj_call(
        _norm_proj_vt_kernel, "norm_proj_vt", x2d, gain, w_bf, n_qk, [w_bf], [value_cols],
        [jax.ShapeDtypeStruct((batch, seq // ATTN_TILE, d_v, ATTN_TILE), BF16)],
        [pl.BlockSpec((1, per_step, d_v, ATTN_TILE), lambda i: (i // steps, i % steps, 0, 0))],
        scratch_shapes=[pltpu.VMEM((d_v, d), BF16)], semantics="arbitrary")


def _gla_kernel(q_ref, k_ref, v_ref, r_ref, gk_ref, cd_ref, gn_ref, o_ref):
    c = GLA_CHUNK
    grp = GLA_GROUP
    n = grp // c
    seq, hk = q_ref.shape[1], q_ref.shape[2]
    hv = v_ref.shape[2]
    shift = c.bit_length() - 1
    row = lax.broadcasted_iota(jnp.int32, (grp, grp), 0)
    col = lax.broadcasted_iota(jnp.int32, (grp, grp), 1)
    mask = (col <= row) & ((row >> shift) == (col >> shift))
    tri = jnp.where(mask, 1.0, 0.0).astype(BF16)
    gain = gn_ref[...]
    q_scale = hk ** -0.5

    def rows_of(gi):
        return slice(gi * grp, (gi + 1) * grp)

    def cum_decay(gi):
        g = gk_ref[0, rows_of(gi), :]
        g_hi = g.astype(BF16)
        g_lo = (g - g_hi.astype(F32)).astype(BF16)
        return _dot(tri, g_hi) + _dot(tri, g_lo)

    def scaled_operands(gi, b):
        b = b.reshape(n, c, hk)
        b_mid = b[:, c // 2:c // 2 + 1, :]
        b_last = b[:, c - 1:c, :]
        q = (q_ref[0, rows_of(gi), :].astype(F32) * q_scale).reshape(n, c, hk)
        k = k_ref[0, rows_of(gi), :].astype(F32).reshape(n, c, hk)
        q_mid = (q * jnp.exp2(b - b_mid)).reshape(grp, hk).astype(BF16)
        k_mid = (k * jnp.exp2(b_mid - b)).reshape(grp, hk).astype(BF16)
        q_dec = (q * jnp.exp2(b)).astype(BF16)
        k_end = (k * jnp.exp2(b_last - b)).astype(BF16)
        return q_mid, k_mid, q_dec, k_end, jnp.exp2(b_last)

    def state_part(gi, o_intra, q_dec, decay, incs, state_t):
        outs = []
        for ci in range(n):
            outs.append(o_intra[ci * c:(ci + 1) * c] + _dot_nt(q_dec[ci], state_t.astype(BF16)))
            state_t = state_t * decay[ci] + incs[ci]
        o = _rmsnorm_rows(jnp.concatenate(outs, axis=0), gain, NORM_EPS)
        r = r_ref[0, rows_of(gi), :].astype(F32)
        o_ref[0, rows_of(gi), :] = (o * (r * jax.nn.sigmoid(r))).astype(o_ref.dtype)
        return state_t

    def factored_sweep():
        groups = seq // grp
        state_t = jnp.zeros((hv, hk), F32)
        cum, ops = {}, {}
        for i in range(-2, groups):
            if i >= 0:
                q_mid, k_mid, q_dec, k_end, decay = ops.pop(i)
                v = v_ref[0, rows_of(i), :]
                attn = jnp.where(mask, _dot_nt(q_mid, k_mid), 0.0).astype(BF16)
                incs = [_dot_tn(v[ci * c:(ci + 1) * c], k_end[ci]) for ci in range(n)]
            if i + 2 < groups:
                cum[i + 2] = cum_decay(i + 2)
            if i >= 0:
                o_intra = _dot(attn, v)
            if 0 <= i + 1 < groups:
                ops[i + 1] = scaled_operands(i + 1, cum.pop(i + 1))
            if i >= 0:
                state_t = state_part(i, o_intra, q_dec, decay, incs, state_t)

    def direct_sweep():
        causal = (lax.broadcasted_iota(jnp.int32, (c, c), 1)
                  <= lax.broadcasted_iota(jnp.int32, (c, c), 0))
        tri_c = jnp.where(causal, 1.0, 0.0)
        assert hk >= c
        col_id = lax.broadcasted_iota(jnp.int32, (c, hk), 1)

        def chunk_step(ci, state_t):
            rows = pl.ds(pl.multiple_of(ci * c, c), c)
            b = jnp.dot(tri_c, gk_ref[0, rows, :], preferred_element_type=F32,
                        precision=lax.Precision.HIGHEST)
            q = q_ref[0, rows, :].astype(F32) * q_scale
            k = k_ref[0, rows, :].astype(F32)
            v = v_ref[0, rows, :]
            attn = jnp.zeros((c, hk), F32)
            for j in range(c):
                weight = jnp.exp2(jnp.minimum(b - b[j:j + 1, :], 0.0))
                column = jnp.sum(q * weight * k[j:j + 1, :], axis=-1, keepdims=True)
                attn = jnp.where(col_id == j, column, attn)
            attn = jnp.where(causal, attn[:, 0:c], 0.0).astype(BF16)
            b_last = b[c - 1:c, :]
            o = _dot(attn, v) + _dot_nt((q * jnp.exp2(b)).astype(BF16), state_t.astype(BF16))
            k_end = (k * jnp.exp2(b_last - b)).astype(BF16)
            state_t = state_t * jnp.exp2(b_last) + _dot_tn(v, k_end)
            o = _rmsnorm_rows(o, gain, NORM_EPS)
            r = r_ref[0, rows, :].astype(F32)
            o_ref[0, rows, :] = (o * (r * jax.nn.sigmoid(r))).astype(o_ref.dtype)
            return state_t

        lax.fori_loop(0, seq // c, chunk_step, jnp.zeros((hv, hk), F32))

    decay_is_moderate = jnp.max(cd_ref[0]) <= GLA_MAX_FACTORED_DECAY

    @pl.when(decay_is_moderate)
    def _():
        factored_sweep()

    @pl.when(jnp.logical_not(decay_is_moderate))
    def _():
        direct_sweep()


def _gla_core(proj, gk, chunk_decay, g_norm, batch, seq):
    dk = gk.shape[1]
    hk = dk // GLA_HEADS
    hv = g_norm.shape[0]
    dv = hv * GLA_HEADS
    assert proj.shape[1] == 2 * dk + 2 * dv and seq % GLA_GROUP == 0 and GLA_GROUP % GLA_CHUNK == 0
    proj3 = proj.reshape(batch, seq, proj.shape[1])
    gk3 = gk.reshape(batch, seq, dk)
    cd3 = chunk_decay.reshape(batch, seq // GLA_CHUNK, dk)
    k_off, v_off, r_off = dk // hk, (2 * dk) // hv, (2 * dk + dv) // hv
    return pl.pallas_call(
        _gla_kernel,
        grid=(batch, GLA_HEADS),
        in_specs=[
            pl.BlockSpec((1, seq, hk), lambda b, h: (b, 0, h)),
            pl.BlockSpec((1, seq, hk), lambda b, h: (b, 0, k_off + h)),
            pl.BlockSpec((1, seq, hv), lambda b, h: (b, 0, v_off + h)),
            pl.BlockSpec((1, seq, hv), lambda b, h: (b, 0, r_off + h)),
            pl.BlockSpec((1, seq, hk), lambda b, h: (b, 0, h)),
            pl.BlockSpec((1, seq // GLA_CHUNK, hk), lambda b, h: (b, 0, h)),
            _resident((1, hv)),
        ],
        out_specs=pl.BlockSpec((1, seq, hv), lambda b, h: (b, 0, h)),
        out_shape=jax.ShapeDtypeStruct((batch, seq, dv), BF16),
        compiler_params=pltpu.CompilerParams(
            dimension_semantics=("parallel", "parallel"), vmem_limit_bytes=VMEM_LIMIT_BYTES),
        name="gla_core",
    )(proj3, proj3, proj3, proj3, gk3, cd3, g_norm.reshape(1, hv))


def _diff_attn_kernel(lq1_ref, lk1_ref, lq2_ref, lk2_ref, gn_ref, q_ref, k_ref, vt_ref, o_ref,
                      qq_ref, m_ref, acc_ref, *, lambda_init):
    tiles, tq = vt_ref.shape[1], vt_ref.shape[3]
    heads = ATTN_HEADS_PER_STEP
    d2 = vt_ref.shape[2] // heads
    d = d2 // 2
    ts = ATTN_SUB_TILE
    assert tq % ts == 0

    def head_lanes(hh):
        return slice(hh * d2, (hh + 1) * d2)

    lane = lax.broadcasted_iota(jnp.int32, (tq, d2), 1)
    for hh in range(heads):
        for t in range(tiles):
            q = q_ref[0, t * tq:(t + 1) * tq, head_lanes(hh)].astype(F32) * (d ** -0.5 * math.log2(math.e))
            qq_ref[hh * tiles + t, 0:tq, :] = jnp.where(lane < d, q, 0.0).astype(BF16)
            qq_ref[hh * tiles + t, tq:2 * tq, :] = jnp.where(lane >= d, q, 0.0).astype(BF16)
    lam = (jnp.exp(jnp.sum(lq1_ref[...] * lk1_ref[...], keepdims=True))
           - jnp.exp(jnp.sum(lq2_ref[...] * lk2_ref[...], keepdims=True)) + lambda_init)
    out_gain = gn_ref[...] * (1.0 - lambda_init)

    def query_ranges(t, kt, off):
        if kt < t or off == 0:
            return [(0, 2 * tq)]
        return [(off, tq), (tq + off, 2 * tq)]

    def scores(hh, t, kt):
        nk = ts if kt == t else tq
        pieces = []
        for off in range(0, tq, nk):
            k = k_ref[0, kt * tq + off:kt * tq + off + nk, head_lanes(hh)]
            for l0, l1 in query_ranges(t, kt, off):
                st = _dot_nt(k, qq_ref[hh * tiles + t, l0:l1, :])
                if kt == t:
                    k_pos = off + lax.broadcasted_iota(jnp.int32, st.shape, 0)
                    q_pos = (l0 + lax.broadcasted_iota(jnp.int32, st.shape, 1)) & (tq - 1)
                    st = jnp.where(k_pos <= q_pos, st, -jnp.inf)
                pieces.append((off, nk, l0, l1, st))
        return pieces

    def accumulate(pieces, hh, kt, shift, alpha):
        edges = sorted({e for _, _, l0, l1, _ in pieces for e in (l0, l1)})
        for a, b in zip(edges[:-1], edges[1:]):
            parts = [(off, nk, st[:, a - l0:b - l0]) for off, nk, l0, l1, st in pieces
                     if l0 <= a and b <= l1]
            p = jnp.concatenate([jnp.exp2(st - shift[:, a:b]).astype(BF16) for _, _, st in parts], axis=0)
            vt = jnp.concatenate([vt_ref[0, kt, head_lanes(hh), off:off + nk] for off, nk, _ in parts],
                                 axis=1)
            vt = jnp.concatenate([vt, jnp.ones((ATTN_SUM_ROWS, vt.shape[1]), BF16)], axis=0)
            prev = acc_ref[:, a:b] if alpha is None else alpha[:, a:b] * acc_ref[:, a:b]
            acc_ref[:, a:b] = prev + _dot(vt, p)

    def update_running_max(pieces, hh, t, kt):
        m_prev = m_ref[...]
        for _, _, l0, l1, st in pieces:
            m_ref[:, l0:l1] = jnp.maximum(m_ref[:, l0:l1], jnp.max(st, axis=0, keepdims=True))
        m_new = m_ref[...]
        accumulate(pieces, hh, kt, m_new, jnp.exp2(m_prev - m_new))

    def score_bounds(hh):
        ones = jnp.ones((8, d2), BF16)
        k = k_ref[0, :, head_lanes(hh)]
        k_norm2 = jnp.max(_dot_nt(ones, k * k)[0:1, :])
        bounds = []
        for t in range(tiles):
            qq = qq_ref[hh * tiles + t]
            q_norm2 = _dot_nt(ones, qq * qq)[0:1, :]
            bounds.append(jnp.sqrt(q_norm2 * k_norm2) * ATTN_BOUND_SLACK)
        return bounds

    bounds = {hh: score_bounds(hh) for hh in range(heads)}
    worst = jnp.max(jnp.concatenate([b for hh in range(heads) for b in bounds[hh]], axis=1))

    def update_fixed_shift(pieces, hh, t, kt):
        accumulate(pieces, hh, kt, bounds[hh][t], None)

    def finalize(hh, t):
        o_all = acc_ref[0:d2, :] / acc_ref[d2:d2 + 1, :]
        o_t = o_all[:, 0:tq] - lam * o_all[:, tq:2 * tq]
        inv_rms = lax.rsqrt(jnp.mean(o_t * o_t, axis=0, keepdims=True) + SUBLN_EPS)
        o_ref[0, t * tq:(t + 1) * tq, head_lanes(hh)] = (o_t * inv_rms * out_gain).T.astype(o_ref.dtype)

    units = [(hh, t, kt) for hh in range(heads) for t in range(tiles) for kt in range(t + 1)]

    def sweep(update):
        pending = {}
        for i in range(len(units) + ATTN_LOOKAHEAD):
            if i < len(units):
                pending[i] = scores(*units[i])
            j = i - ATTN_LOOKAHEAD
            if j >= 0:
                hh, t, kt = units[j]
                if kt == 0:
                    m_ref[...] = jnp.full_like(m_ref, -jnp.inf)
                    acc_ref[...] = jnp.zeros_like(acc_ref)
                update(pending.pop(j), hh, t, kt)
                if kt == t:
                    finalize(hh, t)

    bound_is_small = worst <= ATTN_MAX_FIXED_SHIFT

    @pl.when(bound_is_small)
    def _():
        sweep(update_fixed_shift)

    @pl.when(jnp.logical_not(bound_is_small))
    def _():
        sweep(update_running_max)


def _diff_attn(qk, v_t, lam_q1, lam_k1, lam_q2, lam_k2, g_norm, batch, seq, lambda_init):
    d_model = qk.shape[1] // 2
    d2 = d_model // DIFF_HEADS
    d = d2 // 2
    tq = ATTN_TILE
    tiles = seq // tq
    heads = ATTN_HEADS_PER_STEP
    assert seq % tq == 0 and g_norm.shape[0] == d2 and v_t.shape == (batch, tiles, d_model, tq)
    assert DIFF_HEADS % heads == 0
    qk3 = qk.reshape(batch, seq, 2 * d_model)
    k_off = DIFF_HEADS // heads
    return pl.pallas_call(
        functools.partial(_diff_attn_kernel, lambda_init=lambda_init),
        grid=(batch, DIFF_HEADS // heads),
        in_specs=[_resident((1, d))] * 4 + [
            _resident((d2, 1)),
            pl.BlockSpec((1, seq, heads * d2), lambda b, h: (b, 0, h)),
            pl.BlockSpec((1, seq, heads * d2), lambda b, h: (b, 0, k_off + h)),
            pl.BlockSpec((1, tiles, heads * d2, tq), lambda b, h: (b, 0, h, 0)),
        ],
        out_specs=pl.BlockSpec((1, seq, heads * d2), lambda b, h: (b, 0, h)),
        out_shape=jax.ShapeDtypeStruct((batch, seq, d_model), BF16),
        scratch_shapes=[
            pltpu.VMEM((heads * tiles, 2 * tq, d2), BF16),
            pltpu.VMEM((1, 2 * tq), F32),
            pltpu.VMEM((d2 + ATTN_SUM_ROWS, 2 * tq), F32),
        ],
        compiler_params=pltpu.CompilerParams(
            dimension_semantics=("parallel", "parallel"), vmem_limit_bytes=VMEM_LIMIT_BYTES),
        name="diff_attn",
    )(lam_q1.reshape(1, d), lam_k1.reshape(1, d), lam_q2.reshape(1, d), lam_k2.reshape(1, d),
      g_norm.reshape(d2, 1), qk3, qk3, v_t)


def _mix_out_ffn_kernel(x_ref, a_ref, wo_ref, gf_ref, wi_ref, w2_ref, gl_ref, o_ref, acc_ref, *, final_norm):
    d_ff = w2_ref.shape[0]
    x1 = x_ref[...] + _dot(a_ref[...], wo_ref[...])
    h = _rmsnorm_rows(x1, gf_ref[...], NORM_EPS).astype(BF16)
    acc_ref[...] = x1
    for f0 in range(0, d_ff, FFN_COL_TILE):
        gate = _dot(h, wi_ref[:, f0:f0 + FFN_COL_TILE])
        up = _dot(h, wi_ref[:, d_ff + f0:d_ff + f0 + FFN_COL_TILE])
        act = (gate * jax.nn.sigmoid(gate) * up).astype(BF16)
        acc_ref[...] += _dot(act, w2_ref[f0:f0 + FFN_COL_TILE, :])
    x2 = acc_ref[...]
    if final_norm:
        x2 = _rmsnorm_rows(x2, gl_ref[...], NORM_EPS)
    o_ref[...] = x2


def _mix_out_ffn(x2d, mix, w_out, layer, g_ffn_all, w_in_all, w2_all, g_last, final_norm):
    t, d = x2d.shape
    d_ff = w2_all.shape[1]
    assert t % FFN_ROW_TILE == 0 and d_ff % FFN_COL_TILE == 0
    row_spec = pl.BlockSpec((FFN_ROW_TILE, d), lambda i: (i, 0))
    return pl.pallas_call(
        functools.partial(_mix_out_ffn_kernel, final_norm=final_norm),
        grid=(t // FFN_ROW_TILE,),
        in_specs=[row_spec, row_spec, _resident((d, d)), _resident((1, d), layer),
                  _resident((d, 2 * d_ff), layer), _resident((d_ff, d), layer), _resident((1, d))],
        out_specs=row_spec,
        out_shape=jax.ShapeDtypeStruct((t, d), F32),
        scratch_shapes=[pltpu.VMEM((FFN_ROW_TILE, d), F32)],
        compiler_params=pltpu.CompilerParams(
            dimension_semantics=("parallel",), vmem_limit_bytes=VMEM_LIMIT_BYTES),
        name="mix_out_ffn",
    )(x2d, mix, w_out.astype(BF16), g_ffn_all.reshape(-1, 1, d), w_in_all, w2_all, g_last.reshape(1, d))


def kernel(x, gla_w_in, gla_w_gate_a, gla_w_gate_b, gla_b_gate, gla_norm, gla_w_out, diff_w_in, diff_lam_q1, diff_lam_k1, diff_lam_q2, diff_lam_k2, diff_norm, diff_w_out, norm_mixer, norm_ffn, ffn_w_in, ffn_w_out, norm_final):
    batch, seq, d = x.shape
    x2d = x.reshape(batch * seq, d)
    ffn_w_in_bf, ffn_w_out_bf = ffn_w_in.astype(BF16), ffn_w_out.astype(BF16)

    proj, gk, chunk_decay = _norm_proj_gate(x2d, norm_mixer[0], gla_w_in[0],
                                            gla_w_gate_a[0], gla_w_gate_b[0], gla_b_gate[0])
    mix = _gla_core(proj, gk, chunk_decay, gla_norm[0], batch, seq).reshape(batch * seq, -1)
    x2d = _mix_out_ffn(x2d, mix, gla_w_out[0], 0, norm_ffn, ffn_w_in_bf, ffn_w_out_bf, norm_final,
                       final_norm=False)

    lambda_init = 0.8 - 0.6 * math.exp(-0.3 * 1)
    qk, v_t = _norm_proj_vt(x2d, norm_mixer[1], diff_w_in[0], 2 * d, batch, seq)
    mix = _diff_attn(qk, v_t, diff_lam_q1[0], diff_lam_k1[0], diff_lam_q2[0], diff_lam_k2[0],
                     diff_norm[0], batch, seq, lambda_init).reshape(batch * seq, -1)
    x2d = _mix_out_ffn(x2d, mix, diff_w_out[0], 1, norm_ffn, ffn_w_in_bf, ffn_w_out_bf, norm_final,
                       final_norm=True)
    return x2d.reshape(batch, seq, d)
```

```python
import functools
import math

import jax
import jax.numpy as jnp
from jax import lax
from jax.experimental import pallas as pl
from jax.experimental.pallas import tpu as pltpu

F32 = jnp.float32
BF16 = jnp.bfloat16

NORM_EPS = 1e-6
SUBLN_EPS = 1e-5

GLA_HEADS = 4
GLA_TAU = 16.0
GLA_CHUNK = 64

DIFF_HEADS = 8

V7X_VMEM_BYTES = 64 * 1024 * 1024
VMEM_LIMIT_BYTES = V7X_VMEM_BYTES - 8 * 1024 * 1024

PROJ_ROW_TILE = 1024
FFN_ROW_TILE = 1024
PROJ_COL_TILE = 1024
FFN_COL_TILE = 512
GLA_GROUP = 256
GLA_MAX_FACTORED_DECAY = 48.0
ATTN_TILE = 512
ATTN_SUB_TILE = 256
ATTN_SUM_ROWS = 16
ATTN_LOOKAHEAD = 2
ATTN_HEADS_PER_STEP = 2
ATTN_MAX_FIXED_SHIFT = 40.0
ATTN_BOUND_SLACK = 1.02


def _resident(shape, layer=None):
    if layer is None:
        index, block = (0,) * len(shape), tuple(shape)
    else:
        index, block = (layer,) + (0,) * len(shape), (None,) + tuple(shape)
    return pl.BlockSpec(block, lambda *_: index, pipeline_mode=pl.Buffered(1))


def _rmsnorm_rows(x, gain, eps):
    return x * lax.rsqrt(jnp.mean(x * x, axis=-1, keepdims=True) + eps) * gain


def _dot(a, b):
    return jnp.dot(a, b, preferred_element_type=F32)


def _dot_nt(a, b):
    return lax.dot_general(a, b, (((1,), (1,)), ((), ())), preferred_element_type=F32)


def _dot_tn(a, b):
    return lax.dot_general(a, b, (((0,), (0,)), ((), ())), preferred_element_type=F32)


def _project(h, w_ref, o_ref):
    for c0 in range(0, o_ref.shape[1], PROJ_COL_TILE):
        cols = slice(c0, c0 + PROJ_COL_TILE)
        o_ref[:, cols] = _dot(h, w_ref[:, cols].astype(BF16)).astype(o_ref.dtype)


def _norm_proj_gate_kernel(x_ref, g_ref, w_ref, wa_ref, wb_ref, bg_ref, o_ref, gk_ref, cd_ref):
    h = _rmsnorm_rows(x_ref[...], g_ref[...], NORM_EPS).astype(BF16)
    low = _dot(h, wa_ref[...])
    logits = _dot(low.astype(BF16), wb_ref[...]) + bg_ref[...]
    log_sig = jnp.minimum(logits, 0.0) - jnp.log(1.0 + jnp.exp(-jnp.abs(logits)))
    log2_decay = log_sig * (math.log2(math.e) / GLA_TAU)
    gk_ref[...] = log2_decay
    chunks, dk = cd_ref.shape
    cd_ref[...] = -jnp.sum(log2_decay.reshape(chunks, GLA_CHUNK, dk), axis=1)
    _project(h, w_ref, o_ref)


def _norm_proj_vt_kernel(x_ref, g_ref, w_ref, wv_ref, o_ref, vt_ref, wvt_ref):
    @pl.when(pl.program_id(0) == 0)
    def _():
        wvt_ref[...] = wv_ref[...].T.astype(BF16)

    h = _rmsnorm_rows(x_ref[...], g_ref[...], NORM_EPS).astype(BF16)
    _project(h, w_ref, o_ref)
    tile = vt_ref.shape[3]
    for j in range(vt_ref.shape[1]):
        vt_ref[0, j] = _dot_nt(wvt_ref[...], h[j * tile:(j + 1) * tile]).astype(vt_ref.dtype)


def _norm_proj_call(kernel, name, x2d, gain, w, n_out, extra_in, extra_specs, extra_out_shapes,
                    extra_out_specs, scratch_shapes=(), semantics="parallel"):
    t, d = x2d.shape
    assert t % PROJ_ROW_TILE == 0 and n_out % PROJ_COL_TILE == 0 and w.shape[1] >= n_out
    return pl.pallas_call(
        kernel,
        grid=(t // PROJ_ROW_TILE,),
        in_specs=[pl.BlockSpec((PROJ_ROW_TILE, d), lambda i: (i, 0)), _resident((1, d)),
                  _resident((d, n_out))] + extra_specs,
        out_specs=[pl.BlockSpec((PROJ_ROW_TILE, n_out), lambda i: (i, 0))] + extra_out_specs,
        out_shape=[jax.ShapeDtypeStruct((t, n_out), BF16)] + extra_out_shapes,
        scratch_shapes=list(scratch_shapes),
        compiler_params=pltpu.CompilerParams(
            dimension_semantics=(semantics,), vmem_limit_bytes=VMEM_LIMIT_BYTES),
        name=name,
    )(x2d, gain.reshape(1, d), w, *extra_in)


def _norm_proj_gate(x2d, gain, w, w_a, w_b, b_g):
    t, d = x2d.shape
    rank, dk = w_b.shape
    chunks = PROJ_ROW_TILE // GLA_CHUNK
    return _norm_proj_call(
        _norm_proj_gate_kernel, "norm_proj_gate", x2d, gain, w, w.shape[1],
        [w_a.astype(BF16), w_b.astype(BF16), b_g.reshape(1, dk)],
        [_resident((d, rank)), _resident((rank, dk)), _resident((1, dk))],
        [jax.ShapeDtypeStruct((t, dk), F32), jax.ShapeDtypeStruct((t // GLA_CHUNK, dk), F32)],
        [pl.BlockSpec((PROJ_ROW_TILE, dk), lambda i: (i, 0)), pl.BlockSpec((chunks, dk), lambda i: (i, 0))])


def _norm_proj_vt(x2d, gain, w, n_qk, batch, seq):
    t, d = x2d.shape
    d_v = w.shape[1] - n_qk
    assert n_qk % d_v == 0 and seq % PROJ_ROW_TILE == 0 and PROJ_ROW_TILE % ATTN_TILE == 0
    steps = seq // PROJ_ROW_TILE
    per_step = PROJ_ROW_TILE // ATTN_TILE
    value_cols = pl.BlockSpec((d, d_v), lambda i: (0, n_qk // d_v), pipeline_mode=pl.Buffered(1))
    return _norm_proj_call(
        _norm_proj_vt_kernel, "norm_proj_vt", x2d, gain, w, n_qk, [w], [value_cols],
        [jax.ShapeDtypeStruct((batch, seq // ATTN_TILE, d_v, ATTN_TILE), BF16)],
        [pl.BlockSpec((1, per_step, d_v, ATTN_TILE), lambda i: (i // steps, i % steps, 0, 0))],
        scratch_shapes=[pltpu.VMEM((d_v, d), BF16)], semantics="arbitrary")


def _gla_kernel(q_ref, k_ref, v_ref, r_ref, gk_ref, cd_ref, gn_ref, o_ref):
    c = GLA_CHUNK
    grp = GLA_GROUP
    n = grp // c
    seq, hk = q_ref.shape[1], q_ref.shape[2]
    hv = v_ref.shape[2]
    shift = c.bit_length() - 1
    row = lax.broadcasted_iota(jnp.int32, (grp, grp), 0)
    col = lax.broadcasted_iota(jnp.int32, (grp, grp), 1)
    mask = (col <= row) & ((row >> shift) == (col >> shift))
    tri = jnp.where(mask, 1.0, 0.0).astype(BF16)
    gain = gn_ref[...]
    q_scale = hk ** -0.5

    def rows_of(gi):
        return slice(gi * grp, (gi + 1) * grp)

    def cum_decay(gi):
        g = gk_ref[0, rows_of(gi), :]
        g_hi = g.astype(BF16)
        g_lo = (g - g_hi.astype(F32)).astype(BF16)
        return _dot(tri, g_hi) + _dot(tri, g_lo)

    def scaled_operands(gi, b):
        b = b.reshape(n, c, hk)
        b_mid = b[:, c // 2:c // 2 + 1, :]
        b_last = b[:, c - 1:c, :]
        q = (q_ref[0, rows_of(gi), :].astype(F32) * q_scale).reshape(n, c, hk)
        k = k_ref[0, rows_of(gi), :].astype(F32).reshape(n, c, hk)
        q_mid = (q * jnp.exp2(b - b_mid)).reshape(grp, hk).astype(BF16)
        k_mid = (k * jnp.exp2(b_mid - b)).reshape(grp, hk).astype(BF16)
        q_dec = (q * jnp.exp2(b)).astype(BF16)
        k_end = (k * jnp.exp2(b_last - b)).astype(BF16)
        return q_mid, k_mid, q_dec, k_end, jnp.exp2(b_last)

    def state_part(gi, o_intra, q_dec, decay, incs, state_t):
        outs = []
        for ci in range(n):
            outs.append(o_intra[ci * c:(ci + 1) * c] + _dot_nt(q_dec[ci], state_t.astype(BF16)))
            state_t = state_t * decay[ci] + incs[ci]
        o = _rmsnorm_rows(jnp.concatenate(outs, axis=0), gain, NORM_EPS)
        r = r_ref[0, rows_of(gi), :].astype(F32)
        o_ref[0, rows_of(gi), :] = (o * (r * jax.nn.sigmoid(r))).astype(o_ref.dtype)
        return state_t

    def factored_sweep():
        groups = seq // grp
        state_t = jnp.zeros((hv, hk), F32)
        cum, ops = {}, {}
        for i in range(-2, groups):
            if i >= 0:
                q_mid, k_mid, q_dec, k_end, decay = ops.pop(i)
                v = v_ref[0, rows_of(i), :]
                attn = jnp.where(mask, _dot_nt(q_mid, k_mid), 0.0).astype(BF16)
                incs = [_dot_tn(v[ci * c:(ci + 1) * c], k_end[ci]) for ci in range(n)]
            if i + 2 < groups:
                cum[i + 2] = cum_decay(i + 2)
            if i >= 0:
                o_intra = _dot(attn, v)
            if 0 <= i + 1 < groups:
                ops[i + 1] = scaled_operands(i + 1, cum.pop(i + 1))
            if i >= 0:
                state_t = state_part(i, o_intra, q_dec, decay, incs, state_t)

    def direct_sweep():
        causal = (lax.broadcasted_iota(jnp.int32, (c, c), 1)
                  <= lax.broadcasted_iota(jnp.int32, (c, c), 0))
        tri_c = jnp.where(causal, 1.0, 0.0)
        assert hk >= c
        col_id = lax.broadcasted_iota(jnp.int32, (c, hk), 1)

        def chunk_step(ci, state_t):
            rows = pl.ds(pl.multiple_of(ci * c, c), c)
            b = jnp.dot(tri_c, gk_ref[0, rows, :], preferred_element_type=F32,
                        precision=lax.Precision.HIGHEST)
            q = q_ref[0, rows, :].astype(F32) * q_scale
            k = k_ref[0, rows, :].astype(F32)
            v = v_ref[0, rows, :]
            attn = jnp.zeros((c, hk), F32)
            for j in range(c):
                weight = jnp.exp2(jnp.minimum(b - b[j:j + 1, :], 0.0))
                column = jnp.sum(q * weight * k[j:j + 1, :], axis=-1, keepdims=True)
                attn = jnp.where(col_id == j, column, attn)
            attn = jnp.where(causal, attn[:, 0:c], 0.0).astype(BF16)
            b_last = b[c - 1:c, :]
            o = _dot(attn, v) + _dot_nt((q * jnp.exp2(b)).astype(BF16), state_t.astype(BF16))
            k_end = (k * jnp.exp2(b_last - b)).astype(BF16)
            state_t = state_t * jnp.exp2(b_last) + _dot_tn(v, k_end)
            o = _rmsnorm_rows(o, gain, NORM_EPS)
            r = r_ref[0, rows, :].astype(F32)
            o_ref[0, rows, :] = (o * (r * jax.nn.sigmoid(r))).astype(o_ref.dtype)
            return state_t

        lax.fori_loop(0, seq // c, chunk_step, jnp.zeros((hv, hk), F32))

    decay_is_moderate = jnp.max(cd_ref[0]) <= GLA_MAX_FACTORED_DECAY

    @pl.when(decay_is_moderate)
    def _():
        factored_sweep()

    @pl.when(jnp.logical_not(decay_is_moderate))
    def _():
        direct_sweep()


def _gla_core(proj, gk, chunk_decay, g_norm, batch, seq):
    dk = gk.shape[1]
    hk = dk // GLA_HEADS
    hv = g_norm.shape[0]
    dv = hv * GLA_HEADS
    assert proj.shape[1] == 2 * dk + 2 * dv and seq % GLA_GROUP == 0 and GLA_GROUP % GLA_CHUNK == 0
    proj3 = proj.reshape(batch, seq, proj.shape[1])
    gk3 = gk.reshape(batch, seq, dk)
    cd3 = chunk_decay.reshape(batch, seq // GLA_CHUNK, dk)
    k_off, v_off, r_off = dk // hk, (2 * dk) // hv, (2 * dk + dv) // hv
    return pl.pallas_call(
        _gla_kernel,
        grid=(batch, GLA_HEADS),
        in_specs=[
            pl.BlockSpec((1, seq, hk), lambda b, h: (b, 0, h)),
            pl.BlockSpec((1, seq, hk), lambda b, h: (b, 0, k_off + h)),
            pl.BlockSpec((1, seq, hv), lambda b, h: (b, 0, v_off + h)),
            pl.BlockSpec((1, seq, hv), lambda b, h: (b, 0, r_off + h)),
            pl.BlockSpec((1, seq, hk), lambda b, h: (b, 0, h)),
            pl.BlockSpec((1, seq // GLA_CHUNK, hk), lambda b, h: (b, 0, h)),
            _resident((1, hv)),
        ],
        out_specs=pl.BlockSpec((1, seq, hv), lambda b, h: (b, 0, h)),
        out_shape=jax.ShapeDtypeStruct((batch, seq, dv), BF16),
        compiler_params=pltpu.CompilerParams(
            dimension_semantics=("parallel", "parallel"), vmem_limit_bytes=VMEM_LIMIT_BYTES),
        name="gla_core",
    )(proj3, proj3, proj3, proj3, gk3, cd3, g_norm.reshape(1, hv))


def _diff_attn_kernel(lq1_ref, lk1_ref, lq2_ref, lk2_ref, gn_ref, q_ref, k_ref, vt_ref, o_ref,
                      qq_ref, m_ref, acc_ref, *, lambda_init):
    tiles, tq = vt_ref.shape[1], vt_ref.shape[3]
    heads = ATTN_HEADS_PER_STEP
    d2 = vt_ref.shape[2] // heads
    d = d2 // 2
    ts = ATTN_SUB_TILE
    assert tq % ts == 0

    def head_lanes(hh):
        return slice(hh * d2, (hh + 1) * d2)

    lane = lax.broadcasted_iota(jnp.int32, (tq, d2), 1)
    for hh in range(heads):
        for t in range(tiles):
            q = q_ref[0, t * tq:(t + 1) * tq, head_lanes(hh)].astype(F32) * (d ** -0.5 * math.log2(math.e))
            qq_ref[hh * tiles + t, 0:tq, :] = jnp.where(lane < d, q, 0.0).astype(BF16)
            qq_ref[hh * tiles + t, tq:2 * tq, :] = jnp.where(lane >= d, q, 0.0).astype(BF16)
    lam = (jnp.exp(jnp.sum(lq1_ref[...] * lk1_ref[...], keepdims=True))
           - jnp.exp(jnp.sum(lq2_ref[...] * lk2_ref[...], keepdims=True)) + lambda_init)
    out_gain = gn_ref[...] * (1.0 - lambda_init)

    def query_ranges(t, kt, off):
        if kt < t or off == 0:
            return [(0, 2 * tq)]
        return [(off, tq), (tq + off, 2 * tq)]

    def scores(hh, t, kt):
        nk = ts if kt == t else tq
        pieces = []
        for off in range(0, tq, nk):
            k = k_ref[0, kt * tq + off:kt * tq + off + nk, head_lanes(hh)]
            for l0, l1 in query_ranges(t, kt, off):
                st = _dot_nt(k, qq_ref[hh * tiles + t, l0:l1, :])
                if kt == t:
                    k_pos = off + lax.broadcasted_iota(jnp.int32, st.shape, 0)
                    q_pos = (l0 + lax.broadcasted_iota(jnp.int32, st.shape, 1)) & (tq - 1)
                    st = jnp.where(k_pos <= q_pos, st, -jnp.inf)
                pieces.append((off, nk, l0, l1, st))
        return pieces

    def accumulate(pieces, hh, kt, shift, alpha):
        edges = sorted({e for _, _, l0, l1, _ in pieces for e in (l0, l1)})
        for a, b in zip(edges[:-1], edges[1:]):
            parts = [(off, nk, st[:, a - l0:b - l0]) for off, nk, l0, l1, st in pieces
                     if l0 <= a and b <= l1]
            p = jnp.concatenate([jnp.exp2(st - shift[:, a:b]).astype(BF16) for _, _, st in parts], axis=0)
            vt = jnp.concatenate([vt_ref[0, kt, head_lanes(hh), off:off + nk] for off, nk, _ in parts],
                                 axis=1)
            vt = jnp.concatenate([vt, jnp.ones((ATTN_SUM_ROWS, vt.shape[1]), BF16)], axis=0)
            prev = acc_ref[:, a:b] if alpha is None else alpha[:, a:b] * acc_ref[:, a:b]
            acc_ref[:, a:b] = prev + _dot(vt, p)

    def update_running_max(pieces, hh, t, kt):
        m_prev = m_ref[...]
        for _, _, l0, l1, st in pieces:
            m_ref[:, l0:l1] = jnp.maximum(m_ref[:, l0:l1], jnp.max(st, axis=0, keepdims=True))
        m_new = m_ref[...]
        accumulate(pieces, hh, kt, m_new, jnp.exp2(m_prev - m_new))

    def score_bounds(hh):
        ones = jnp.ones((8, d2), BF16)
        k = k_ref[0, :, head_lanes(hh)]
        k_norm2 = jnp.max(_dot_nt(ones, k * k)[0:1, :])
        bounds = []
        for t in range(tiles):
            qq = qq_ref[hh * tiles + t]
            q_norm2 = _dot_nt(ones, qq * qq)[0:1, :]
            bounds.append(jnp.sqrt(q_norm2 * k_norm2) * ATTN_BOUND_SLACK)
        return bounds

    bounds = {hh: score_bounds(hh) for hh in range(heads)}
    worst = jnp.max(jnp.concatenate([b for hh in range(heads) for b in bounds[hh]], axis=1))

    def update_fixed_shift(pieces, hh, t, kt):
        accumulate(pieces, hh, kt, bounds[hh][t], None)

    def finalize(hh, t):
        o_all = acc_ref[0:d2, :] / acc_ref[d2:d2 + 1, :]
        o_t = o_all[:, 0:tq] - lam * o_all[:, tq:2 * tq]
        inv_rms = lax.rsqrt(jnp.mean(o_t * o_t, axis=0, keepdims=True) + SUBLN_EPS)
        o_ref[0, t * tq:(t + 1) * tq, head_lanes(hh)] = (o_t * inv_rms * out_gain).T.astype(o_ref.dtype)

    units = [(hh, t, kt) for hh in range(heads) for t in range(tiles) for kt in range(t + 1)]

    def sweep(update):
        pending = {}
        for i in range(len(units) + ATTN_LOOKAHEAD):
            if i < len(units):
                pending[i] = scores(*units[i])
            j = i - ATTN_LOOKAHEAD
            if j >= 0:
                hh, t, kt = units[j]
                if kt == 0:
                    m_ref[...] = jnp.full_like(m_ref, -jnp.inf)
                    acc_ref[...] = jnp.zeros_like(acc_ref)
                update(pending.pop(j), hh, t, kt)
                if kt == t:
                    finalize(hh, t)

    bound_is_small = worst <= ATTN_MAX_FIXED_SHIFT

    @pl.when(bound_is_small)
    def _():
        sweep(update_fixed_shift)

    @pl.when(jnp.logical_not(bound_is_small))
    def _():
        sweep(update_running_max)


def _diff_attn(qk, v_t, lam_q1, lam_k1, lam_q2, lam_k2, g_norm, batch, seq, lambda_init):
    d_model = qk.shape[1] // 2
    d2 = d_model // DIFF_HEADS
    d = d2 // 2
    tq = ATTN_TILE
    tiles = seq // tq
    heads = ATTN_HEADS_PER_STEP
    assert seq % tq == 0 and g_norm.shape[0] == d2 and v_t.shape == (batch, tiles, d_model, tq)
    assert DIFF_HEADS % heads == 0
    qk3 = qk.reshape(batch, seq, 2 * d_model)
    k_off = DIFF_HEADS // heads
    return pl.pallas_call(
        functools.partial(_diff_attn_kernel, lambda_init=lambda_init),
        grid=(batch, DIFF_HEADS // heads),
        in_specs=[_resident((1, d))] * 4 + [
            _resident((d2, 1)),
            pl.BlockSpec((1, seq, heads * d2), lambda b, h: (b, 0, h)),
            pl.BlockSpec((1, seq, heads * d2), lambda b, h: (b, 0, k_off + h)),
            pl.BlockSpec((1, tiles, heads * d2, tq), lambda b, h: (b, 0, h, 0)),
        ],
        out_specs=pl.BlockSpec((1, seq, heads * d2), lambda b, h: (b, 0, h)),
        out_shape=jax.ShapeDtypeStruct((batch, seq, d_model), BF16),
        scratch_shapes=[
            pltpu.VMEM((heads * tiles, 2 * tq, d2), BF16),
            pltpu.VMEM((1, 2 * tq), F32),
            pltpu.VMEM((d2 + ATTN_SUM_ROWS, 2 * tq), F32),
        ],
        compiler_params=pltpu.CompilerParams(
            dimension_semantics=("parallel", "parallel"), vmem_limit_bytes=VMEM_LIMIT_BYTES),
        name="diff_attn",
    )(lam_q1.reshape(1, d), lam_k1.reshape(1, d), lam_q2.reshape(1, d), lam_k2.reshape(1, d),
      g_norm.reshape(d2, 1), qk3, qk3, v_t)


def _mix_out_ffn_kernel(x_ref, a_ref, wo_ref, gf_ref, wi_ref, w2_ref, gl_ref, o_ref, acc_ref, *, final_norm):
    d_ff = w2_ref.shape[0]
    x1 = x_ref[...] + _dot(a_ref[...], wo_ref[...])
    h = _rmsnorm_rows(x1, gf_ref[...], NORM_EPS).astype(BF16)
    acc_ref[...] = x1
    for f0 in range(0, d_ff, FFN_COL_TILE):
        f1 = min(f0 + FFN_COL_TILE, d_ff)
        gate = _dot(h, wi_ref[:, f0:f1])
        up = _dot(h, wi_ref[:, d_ff + f0:d_ff + f1])
        act = (gate * jax.nn.sigmoid(gate) * up).astype(BF16)
        acc_ref[...] += _dot(act, w2_ref[f0:f1, :])
    x2 = acc_ref[...]
    if final_norm:
        x2 = _rmsnorm_rows(x2, gl_ref[...], NORM_EPS)
    o_ref[...] = x2


def _mix_out_ffn(x2d, mix, w_out, layer, g_ffn_all, w_in_all, w2_all, g_last, final_norm):
    t, d = x2d.shape
    d_ff = w2_all.shape[1]
    assert t % FFN_ROW_TILE == 0 and d_ff % 256 == 0 and FFN_COL_TILE % 256 == 0
    row_spec = pl.BlockSpec((FFN_ROW_TILE, d), lambda i: (i, 0))
    return pl.pallas_call(
        functools.partial(_mix_out_ffn_kernel, final_norm=final_norm),
        grid=(t // FFN_ROW_TILE,),
        in_specs=[row_spec, row_spec, _resident((d, d)), _resident((1, d), layer),
                  _resident((d, 2 * d_ff), layer), _resident((d_ff, d), layer), _resident((1, d))],
        out_specs=row_spec,
        out_shape=jax.ShapeDtypeStruct((t, d), F32),
        scratch_shapes=[pltpu.VMEM((FFN_ROW_TILE, d), F32)],
        compiler_params=pltpu.CompilerParams(
            dimension_semantics=("parallel",), vmem_limit_bytes=VMEM_LIMIT_BYTES),
        name="mix_out_ffn",
    )(x2d, mix, w_out.astype(BF16), g_ffn_all.reshape(-1, 1, d), w_in_all, w2_all, g_last.reshape(1, d))


def kernel(x, gla_w_in, gla_w_gate_a, gla_w_gate_b, gla_b_gate, gla_norm, gla_w_out, diff_w_in, diff_lam_q1, diff_lam_k1, diff_lam_q2, diff_lam_k2, diff_norm, diff_w_out, norm_mixer, norm_ffn, ffn_w_in, ffn_w_out, norm_final):
    batch, seq, d = x.shape
    x2d = x.reshape(batch * seq, d)
    ffn_w_in_bf, ffn_w_out_bf = ffn_w_in.astype(BF16), ffn_w_out.astype(BF16)

    proj, gk, chunk_decay = _norm_proj_gate(x2d, norm_mixer[0], gla_w_in[0],
                                            gla_w_gate_a[0], gla_w_gate_b[0], gla_b_gate[0])
    mix = _gla_core(proj, gk, chunk_decay, gla_norm[0], batch, seq).reshape(batch * seq, -1)
    x2d = _mix_out_ffn(x2d, mix, gla_w_out[0], 0, norm_ffn, ffn_w_in_bf, ffn_w_out_bf, norm_final,
                       final_norm=False)

    lambda_init = 0.8 - 0.6 * math.exp(-0.3 * 1)
    qk, v_t = _norm_proj_vt(x2d, norm_mixer[1], diff_w_in[0], 2 * d, batch, seq)
    mix = _diff_attn(qk, v_t, diff_lam_q1[0], diff_lam_k1[0], diff_lam_q2[0], diff_lam_k2[0],
                     diff_norm[0], batch, seq, lambda_init).reshape(batch * seq, -1)
    x2d = _mix_out_ffn(x2d, mix, diff_w_out[0], 1, norm_ffn, ffn_w_in_bf, ffn_w_out_bf, norm_final,
                       final_norm=True)
    return x2d.reshape(batch, seq, d)
```

```python
import functools
import math

import jax
import jax.numpy as jnp
from jax import lax
from jax.experimental import pallas as pl
from jax.experimental.pallas import tpu as pltpu

F32 = jnp.float32
BF16 = jnp.bfloat16

NORM_EPS = 1e-6
SUBLN_EPS = 1e-5

GLA_HEADS = 4
GLA_TAU = 16.0
GLA_CHUNK = 64

DIFF_HEADS = 8

V7X_VMEM_BYTES = 64 * 1024 * 1024
VMEM_LIMIT_BYTES = V7X_VMEM_BYTES - 8 * 1024 * 1024

PROJ_ROW_TILE = 1024
FFN_ROW_TILE = 1024
PROJ_COL_TILE = 1024
FFN_COL_TILE = 256
GLA_GROUP = 256
GLA_MAX_FACTORED_DECAY = 48.0
ATTN_TILE = 512
ATTN_SUB_TILE = 256
ATTN_SUM_ROWS = 16
ATTN_LOOKAHEAD = 1
ATTN_HEADS_PER_STEP = 2
ATTN_MAX_FIXED_SHIFT = 40.0
ATTN_BOUND_SLACK = 1.02


def _resident(shape, layer=None):
    if layer is None:
        index, block = (0,) * len(shape), tuple(shape)
    else:
        index, block = (layer,) + (0,) * len(shape), (None,) + tuple(shape)
    return pl.BlockSpec(block, lambda *_: index, pipeline_mode=pl.Buffered(1))


def _rmsnorm_rows(x, gain, eps):
    return x * lax.rsqrt(jnp.mean(x * x, axis=-1, keepdims=True) + eps) * gain


def _dot(a, b):
    return jnp.dot(a, b, preferred_element_type=F32)


def _dot_nt(a, b):
    return lax.dot_general(a, b, (((1,), (1,)), ((), ())), preferred_element_type=F32)


def _dot_tn(a, b):
    return lax.dot_general(a, b, (((0,), (0,)), ((), ())), preferred_element_type=F32)


def _project(h, w_ref, o_ref):
    for c0 in range(0, o_ref.shape[1], PROJ_COL_TILE):
        cols = slice(c0, c0 + PROJ_COL_TILE)
        o_ref[:, cols] = _dot(h, w_ref[:, cols].astype(BF16)).astype(o_ref.dtype)


def _norm_proj_gate_kernel(x_ref, g_ref, w_ref, wa_ref, wb_ref, bg_ref, o_ref, gk_ref, cd_ref):
    h = _rmsnorm_rows(x_ref[...], g_ref[...], NORM_EPS).astype(BF16)
    low = _dot(h, wa_ref[...])
    logits = _dot(low.astype(BF16), wb_ref[...]) + bg_ref[...]
    log_sig = jnp.minimum(logits, 0.0) - jnp.log(1.0 + jnp.exp(-jnp.abs(logits)))
    log2_decay = log_sig * (math.log2(math.e) / GLA_TAU)
    gk_ref[...] = log2_decay
    chunks, dk = cd_ref.shape
    cd_ref[...] = -jnp.sum(log2_decay.reshape(chunks, GLA_CHUNK, dk), axis=1)
    _project(h, w_ref, o_ref)


def _norm_proj_vt_kernel(x_ref, g_ref, w_ref, wv_ref, o_ref, vt_ref, wvt_ref):
    @pl.when(pl.program_id(0) == 0)
    def _():
        wvt_ref[...] = wv_ref[...].T.astype(BF16)

    h = _rmsnorm_rows(x_ref[...], g_ref[...], NORM_EPS).astype(BF16)
    _project(h, w_ref, o_ref)
    tile = vt_ref.shape[3]
    for j in range(vt_ref.shape[1]):
        vt_ref[0, j] = _dot_nt(wvt_ref[...], h[j * tile:(j + 1) * tile]).astype(vt_ref.dtype)


def _norm_proj_call(kernel, name, x2d, gain, w, n_out, extra_in, extra_specs, extra_out_shapes,
                    extra_out_specs, scratch_shapes=(), semantics="parallel"):
    t, d = x2d.shape
    assert t % PROJ_ROW_TILE == 0 and n_out % PROJ_COL_TILE == 0 and w.shape[1] >= n_out
    return pl.pallas_call(
        kernel,
        grid=(t // PROJ_ROW_TILE,),
        in_specs=[pl.BlockSpec((PROJ_ROW_TILE, d), lambda i: (i, 0)), _resident((1, d)),
                  _resident((d, n_out))] + extra_specs,
        out_specs=[pl.BlockSpec((PROJ_ROW_TILE, n_out), lambda i: (i, 0))] + extra_out_specs,
        out_shape=[jax.ShapeDtypeStruct((t, n_out), BF16)] + extra_out_shapes,
        scratch_shapes=list(scratch_shapes),
        compiler_params=pltpu.CompilerParams(
            dimension_semantics=(semantics,), vmem_limit_bytes=VMEM_LIMIT_BYTES),
        name=name,
    )(x2d, gain.reshape(1, d), w, *extra_in)


def _norm_proj_gate(x2d, gain, w, w_a, w_b, b_g):
    t, d = x2d.shape
    rank, dk = w_b.shape
    chunks = PROJ_ROW_TILE // GLA_CHUNK
    return _norm_proj_call(
        _norm_proj_gate_kernel, "norm_proj_gate", x2d, gain, w, w.shape[1],
        [w_a.astype(BF16), w_b.astype(BF16), b_g.reshape(1, dk)],
        [_resident((d, rank)), _resident((rank, dk)), _resident((1, dk))],
        [jax.ShapeDtypeStruct((t, dk), F32), jax.ShapeDtypeStruct((t // GLA_CHUNK, dk), F32)],
        [pl.BlockSpec((PROJ_ROW_TILE, dk), lambda i: (i, 0)), pl.BlockSpec((chunks, dk), lambda i: (i, 0))])


def _norm_proj_vt(x2d, gain, w, n_qk, batch, seq):
    t, d = x2d.shape
    d_v = w.shape[1] - n_qk
    assert n_qk % d_v == 0 and seq % PROJ_ROW_TILE == 0 and PROJ_ROW_TILE % ATTN_TILE == 0
    steps = seq // PROJ_ROW_TILE
    per_step = PROJ_ROW_TILE // ATTN_TILE
    value_cols = pl.BlockSpec((d, d_v), lambda i: (0, n_qk // d_v), pipeline_mode=pl.Buffered(1))
    return _norm_proj_call(
        _norm_proj_vt_kernel, "norm_proj_vt", x2d, gain, w, n_qk, [w], [value_cols],
        [jax.ShapeDtypeStruct((batch, seq // ATTN_TILE, d_v, ATTN_TILE), BF16)],
        [pl.BlockSpec((1, per_step, d_v, ATTN_TILE), lambda i: (i // steps, i % steps, 0, 0))],
        scratch_shapes=[pltpu.VMEM((d_v, d), BF16)], semantics="arbitrary")


def _gla_kernel(q_ref, k_ref, v_ref, r_ref, gk_ref, cd_ref, gn_ref, o_ref):
    c = GLA_CHUNK
    grp = GLA_GROUP
    n = grp // c
    seq, hk = q_ref.shape[1], q_ref.shape[2]
    hv = v_ref.shape[2]
    shift = c.bit_length() - 1
    row = lax.broadcasted_iota(jnp.int32, (grp, grp), 0)
    col = lax.broadcasted_iota(jnp.int32, (grp, grp), 1)
    mask = (col <= row) & ((row >> shift) == (col >> shift))
    tri = jnp.where(mask, 1.0, 0.0).astype(BF16)
    gain = gn_ref[...]
    q_scale = hk ** -0.5

    def rows_of(gi):
        return slice(gi * grp, (gi + 1) * grp)

    def cum_decay(gi):
        g = gk_ref[0, rows_of(gi), :]
        g_hi = g.astype(BF16)
        g_lo = (g - g_hi.astype(F32)).astype(BF16)
        return _dot(tri, g_hi) + _dot(tri, g_lo)

    def scaled_operands(gi, b):
        b = b.reshape(n, c, hk)
        b_mid = b[:, c // 2:c // 2 + 1, :]
        b_last = b[:, c - 1:c, :]
        q = (q_ref[0, rows_of(gi), :].astype(F32) * q_scale).reshape(n, c, hk)
        k = k_ref[0, rows_of(gi), :].astype(F32).reshape(n, c, hk)
        q_mid = (q * jnp.exp2(b - b_mid)).reshape(grp, hk).astype(BF16)
        k_mid = (k * jnp.exp2(b_mid - b)).reshape(grp, hk).astype(BF16)
        q_dec = (q * jnp.exp2(b)).astype(BF16)
        k_end = (k * jnp.exp2(b_last - b)).astype(BF16)
        return q_mid, k_mid, q_dec, k_end, jnp.exp2(b_last)

    def state_part(gi, o_intra, q_dec, decay, incs, state_t):
        outs = []
        for ci in range(n):
            outs.append(o_intra[ci * c:(ci + 1) * c] + _dot_nt(q_dec[ci], state_t.astype(BF16)))
            state_t = state_t * decay[ci] + incs[ci]
        o = _rmsnorm_rows(jnp.concatenate(outs, axis=0), gain, NORM_EPS)
        r = r_ref[0, rows_of(gi), :].astype(F32)
        o_ref[0, rows_of(gi), :] = (o * (r * jax.nn.sigmoid(r))).astype(o_ref.dtype)
        return state_t

    def factored_sweep():
        groups = seq // grp
        state_t = jnp.zeros((hv, hk), F32)
        cum, ops = {}, {}
        for i in range(-2, groups):
            if i >= 0:
                q_mid, k_mid, q_dec, k_end, decay = ops.pop(i)
                v = v_ref[0, rows_of(i), :]
                attn = jnp.where(mask, _dot_nt(q_mid, k_mid), 0.0).astype(BF16)
                incs = [_dot_tn(v[ci * c:(ci + 1) * c], k_end[ci]) for ci in range(n)]
            if i + 2 < groups:
                cum[i + 2] = cum_decay(i + 2)
            if i >= 0:
                o_intra = _dot(attn, v)
            if 0 <= i + 1 < groups:
                ops[i + 1] = scaled_operands(i + 1, cum.pop(i + 1))
            if i >= 0:
                state_t = state_part(i, o_intra, q_dec, decay, incs, state_t)

    def direct_sweep():
        causal = (lax.broadcasted_iota(jnp.int32, (c, c), 1)
                  <= lax.broadcasted_iota(jnp.int32, (c, c), 0))
        tri_c = jnp.where(causal, 1.0, 0.0)
        assert hk >= c
        col_id = lax.broadcasted_iota(jnp.int32, (c, hk), 1)

        def chunk_step(ci, state_t):
            rows = pl.ds(pl.multiple_of(ci * c, c), c)
            b = jnp.dot(tri_c, gk_ref[0, rows, :], preferred_element_type=F32,
                        precision=lax.Precision.HIGHEST)
            q = q_ref[0, rows, :].astype(F32) * q_scale
            k = k_ref[0, rows, :].astype(F32)
            v = v_ref[0, rows, :]
            attn = jnp.zeros((c, hk), F32)
            for j in range(c):
                weight = jnp.exp2(jnp.minimum(b - b[j:j + 1, :], 0.0))
                column = jnp.sum(q * weight * k[j:j + 1, :], axis=-1, keepdims=True)
                attn = jnp.where(col_id == j, column, attn)
            attn = jnp.where(causal, attn[:, 0:c], 0.0).astype(BF16)
            b_last = b[c - 1:c, :]
            o = _dot(attn, v) + _dot_nt((q * jnp.exp2(b)).astype(BF16), state_t.astype(BF16))
            k_end = (k * jnp.exp2(b_last - b)).astype(BF16)
            state_t = state_t * jnp.exp2(b_last) + _dot_tn(v, k_end)
            o = _rmsnorm_rows(o, gain, NORM_EPS)
            r = r_ref[0, rows, :].astype(F32)
            o_ref[0, rows, :] = (o * (r * jax.nn.sigmoid(r))).astype(o_ref.dtype)
            return state_t

        lax.fori_loop(0, seq // c, chunk_step, jnp.zeros((hv, hk), F32))

    decay_is_moderate = jnp.max(cd_ref[0]) <= GLA_MAX_FACTORED_DECAY

    @pl.when(decay_is_moderate)
    def _():
        factored_sweep()

    @pl.when(jnp.logical_not(decay_is_moderate))
    def _():
        direct_sweep()


def _gla_core(proj, gk, chunk_decay, g_norm, batch, seq):
    dk = gk.shape[1]
    hk = dk // GLA_HEADS
    hv = g_norm.shape[0]
    dv = hv * GLA_HEADS
    assert proj.shape[1] == 2 * dk + 2 * dv and seq % GLA_GROUP == 0 and GLA_GROUP % GLA_CHUNK == 0
    proj3 = proj.reshape(batch, seq, proj.shape[1])
    gk3 = gk.reshape(batch, seq, dk)
    cd3 = chunk_decay.reshape(batch, seq // GLA_CHUNK, dk)
    k_off, v_off, r_off = dk // hk, (2 * dk) // hv, (2 * dk + dv) // hv
    return pl.pallas_call(
        _gla_kernel,
        grid=(batch, GLA_HEADS),
        in_specs=[
            pl.BlockSpec((1, seq, hk), lambda b, h: (b, 0, h)),
            pl.BlockSpec((1, seq, hk), lambda b, h: (b, 0, k_off + h)),
            pl.BlockSpec((1, seq, hv), lambda b, h: (b, 0, v_off + h)),
            pl.BlockSpec((1, seq, hv), lambda b, h: (b, 0, r_off + h)),
            pl.BlockSpec((1, seq, hk), lambda b, h: (b, 0, h)),
            pl.BlockSpec((1, seq // GLA_CHUNK, hk), lambda b, h: (b, 0, h)),
            _resident((1, hv)),
        ],
        out_specs=pl.BlockSpec((1, seq, hv), lambda b, h: (b, 0, h)),
        out_shape=jax.ShapeDtypeStruct((batch, seq, dv), BF16),
        compiler_params=pltpu.CompilerParams(
            dimension_semantics=("parallel", "parallel"), vmem_limit_bytes=VMEM_LIMIT_BYTES),
        name="gla_core",
    )(proj3, proj3, proj3, proj3, gk3, cd3, g_norm.reshape(1, hv))


def _diff_attn_kernel(lq1_ref, lk1_ref, lq2_ref, lk2_ref, gn_ref, q_ref, k_ref, vt_ref, o_ref,
                      qq_ref, m_ref, acc_ref, *, lambda_init):
    tiles, tq = vt_ref.shape[1], vt_ref.shape[3]
    heads = ATTN_HEADS_PER_STEP
    d2 = vt_ref.shape[2] // heads
    d = d2 // 2
    ts = ATTN_SUB_TILE
    assert tq % ts == 0

    def head_lanes(hh):
        return slice(hh * d2, (hh + 1) * d2)

    lane = lax.broadcasted_iota(jnp.int32, (tq, d2), 1)
    for hh in range(heads):
        for t in range(tiles):
            q = q_ref[0, t * tq:(t + 1) * tq, head_lanes(hh)].astype(F32) * (d ** -0.5 * math.log2(math.e))
            qq_ref[hh * tiles + t, 0:tq, :] = jnp.where(lane < d, q, 0.0).astype(BF16)
            qq_ref[hh * tiles + t, tq:2 * tq, :] = jnp.where(lane >= d, q, 0.0).astype(BF16)
    lam = (jnp.exp(jnp.sum(lq1_ref[...] * lk1_ref[...], keepdims=True))
           - jnp.exp(jnp.sum(lq2_ref[...] * lk2_ref[...], keepdims=True)) + lambda_init)
    out_gain = gn_ref[...] * (1.0 - lambda_init)

    def query_ranges(t, kt, off):
        if kt < t or off == 0:
            return [(0, 2 * tq)]
        return [(off, tq), (tq + off, 2 * tq)]

    def scores(hh, t, kt):
        nk = ts if kt == t else tq
        pieces = []
        for off in range(0, tq, nk):
            k = k_ref[0, kt * tq + off:kt * tq + off + nk, head_lanes(hh)]
            for l0, l1 in query_ranges(t, kt, off):
                st = _dot_nt(k, qq_ref[hh * tiles + t, l0:l1, :])
                if kt == t:
                    k_pos = off + lax.broadcasted_iota(jnp.int32, st.shape, 0)
                    q_pos = (l0 + lax.broadcasted_iota(jnp.int32, st.shape, 1)) & (tq - 1)
                    st = jnp.where(k_pos <= q_pos, st, -jnp.inf)
                pieces.append((off, nk, l0, l1, st))
        return pieces

    def accumulate(pieces, hh, kt, shift, alpha):
        edges = sorted({e for _, _, l0, l1, _ in pieces for e in (l0, l1)})
        for a, b in zip(edges[:-1], edges[1:]):
            parts = [(off, nk, st[:, a - l0:b - l0]) for off, nk, l0, l1, st in pieces
                     if l0 <= a and b <= l1]
            p = jnp.concatenate([jnp.exp2(st - shift[:, a:b]).astype(BF16) for _, _, st in parts], axis=0)
            vt = jnp.concatenate([vt_ref[0, kt, head_lanes(hh), off:off + nk] for off, nk, _ in parts],
                                 axis=1)
            vt = jnp.concatenate([vt, jnp.ones((ATTN_SUM_ROWS, vt.shape[1]), BF16)], axis=0)
            prev = acc_ref[:, a:b] if alpha is None else alpha[:, a:b] * acc_ref[:, a:b]
            acc_ref[:, a:b] = prev + _dot(vt, p)

    def update_running_max(pieces, hh, t, kt):
        m_prev = m_ref[...]
        for _, _, l0, l1, st in pieces:
            m_ref[:, l0:l1] = jnp.maximum(m_ref[:, l0:l1], jnp.max(st, axis=0, keepdims=True))
        m_new = m_ref[...]
        accumulate(pieces, hh, kt, m_new, jnp.exp2(m_prev - m_new))

    def score_bounds(hh):
        ones = jnp.ones((8, d2), BF16)
        k = k_ref[0, :, head_lanes(hh)]
        k_norm2 = jnp.max(_dot_nt(ones, k * k)[0:1, :])
        bounds = []
        for t in range(tiles):
            qq = qq_ref[hh * tiles + t]
            q_norm2 = _dot_nt(ones, qq * qq)[0:1, :]
            bounds.append(jnp.sqrt(q_norm2 * k_norm2) * ATTN_BOUND_SLACK)
        return bounds

    bounds = {hh: score_bounds(hh) for hh in range(heads)}
    worst = jnp.max(jnp.concatenate([b for hh in range(heads) for b in bounds[hh]], axis=1))

    def update_fixed_shift(pieces, hh, t, kt):
        accumulate(pieces, hh, kt, bounds[hh][t], None)

    def finalize(hh, t):
        o_all = acc_ref[0:d2, :] / acc_ref[d2:d2 + 1, :]
        o_t = o_all[:, 0:tq] - lam * o_all[:, tq:2 * tq]
        inv_rms = lax.rsqrt(jnp.mean(o_t * o_t, axis=0, keepdims=True) + SUBLN_EPS)
        o_ref[0, t * tq:(t + 1) * tq, head_lanes(hh)] = (o_t * inv_rms * out_gain).T.astype(o_ref.dtype)

    units = [(hh, t, kt) for hh in range(heads) for t in range(tiles) for kt in range(t + 1)]

    def sweep(update):
        pending = {}
        for i in range(len(units) + ATTN_LOOKAHEAD):
            if i < len(units):
                pending[i] = scores(*units[i])
            j = i - ATTN_LOOKAHEAD
            if j >= 0:
                hh, t, kt = units[j]
                if kt == 0:
                    m_ref[...] = jnp.full_like(m_ref, -jnp.inf)
                    acc_ref[...] = jnp.zeros_like(acc_ref)
                update(pending.pop(j), hh, t, kt)
                if kt == t:
                    finalize(hh, t)

    bound_is_small = worst <= ATTN_MAX_FIXED_SHIFT

    @pl.when(bound_is_small)
    def _():
        sweep(update_fixed_shift)

    @pl.when(jnp.logical_not(bound_is_small))
    def _():
        sweep(update_running_max)


def _diff_attn(qk, v_t, lam_q1, lam_k1, lam_q2, lam_k2, g_norm, batch, seq, lambda_init):
    d_model = qk.shape[1] // 2
    d2 = d_model // DIFF_HEADS
    d = d2 // 2
    tq = ATTN_TILE
    tiles = seq // tq
    heads = ATTN_HEADS_PER_STEP
    assert seq % tq == 0 and g_norm.shape[0] == d2 and v_t.shape == (batch, tiles, d_model, tq)
    assert DIFF_HEADS % heads == 0
    qk3 = qk.reshape(batch, seq, 2 * d_model)
    k_off = DIFF_HEADS // heads
    return pl.pallas_call(
        functools.partial(_diff_attn_kernel, lambda_init=lambda_init),
        grid=(batch, DIFF_HEADS // heads),
        in_specs=[_resident((1, d))] * 4 + [
            _resident((d2, 1)),
            pl.BlockSpec((1, seq, heads * d2), lambda b, h: (b, 0, h)),
            pl.BlockSpec((1, seq, heads * d2), lambda b, h: (b, 0, k_off + h)),
            pl.BlockSpec((1, tiles, heads * d2, tq), lambda b, h: (b, 0, h, 0)),
        ],
        out_specs=pl.BlockSpec((1, seq, heads * d2), lambda b, h: (b, 0, h)),
        out_shape=jax.ShapeDtypeStruct((batch, seq, d_model), BF16),
        scratch_shapes=[
            pltpu.VMEM((heads * tiles, 2 * tq, d2), BF16),
            pltpu.VMEM((1, 2 * tq), F32),
            pltpu.VMEM((d2 + ATTN_SUM_ROWS, 2 * tq), F32),
        ],
        compiler_params=pltpu.CompilerParams(
            dimension_semantics=("parallel", "parallel"), vmem_limit_bytes=VMEM_LIMIT_BYTES),
        name="diff_attn",
    )(lam_q1.reshape(1, d), lam_k1.reshape(1, d), lam_q2.reshape(1, d), lam_k2.reshape(1, d),
      g_norm.reshape(d2, 1), qk3, qk3, v_t)


def _mix_out_ffn_kernel(x_ref, a_ref, wo_ref, gf_ref, wi_ref, w2_ref, gl_ref, o_ref, acc_ref, *, final_norm):
    d_ff = w2_ref.shape[0]
    x1 = x_ref[...] + _dot(a_ref[...], wo_ref[...])
    h = _rmsnorm_rows(x1, gf_ref[...], NORM_EPS).astype(BF16)
    acc_ref[...] = x1
    for f0 in range(0, d_ff, FFN_COL_TILE):
        gate = _dot(h, wi_ref[:, f0:f0 + FFN_COL_TILE])
        up = _dot(h, wi_ref[:, d_ff + f0:d_ff + f0 + FFN_COL_TILE])
        act = (gate * jax.nn.sigmoid(gate) * up).astype(BF16)
        acc_ref[...] += _dot(act, w2_ref[f0:f0 + FFN_COL_TILE, :])
    x2 = acc_ref[...]
    if final_norm:
        x2 = _rmsnorm_rows(x2, gl_ref[...], NORM_EPS)
    o_ref[...] = x2


def _mix_out_ffn(x2d, mix, w_out, layer, g_ffn_all, w_in_all, w2_all, g_last, final_norm):
    t, d = x2d.shape
    d_ff = w2_all.shape[1]
    assert t % FFN_ROW_TILE == 0 and d_ff % FFN_COL_TILE == 0
    row_spec = pl.BlockSpec((FFN_ROW_TILE, d), lambda i: (i, 0))
    return pl.pallas_call(
        functools.partial(_mix_out_ffn_kernel, final_norm=final_norm),
        grid=(t // FFN_ROW_TILE,),
        in_specs=[row_spec, row_spec, _resident((d, d)), _resident((1, d), layer),
                  _resident((d, 2 * d_ff), layer), _resident((d_ff, d), layer), _resident((1, d))],
        out_specs=row_spec,
        out_shape=jax.ShapeDtypeStruct((t, d), F32),
        scratch_shapes=[pltpu.VMEM((FFN_ROW_TILE, d), F32)],
        compiler_params=pltpu.CompilerParams(
            dimension_semantics=("parallel",), vmem_limit_bytes=VMEM_LIMIT_BYTES),
        name="mix_out_ffn",
    )(x2d, mix, w_out.astype(BF16), g_ffn_all.reshape(-1, 1, d), w_in_all, w2_all, g_last.reshape(1, d))


def kernel(x, gla_w_in, gla_w_gate_a, gla_w_gate_b, gla_b_gate, gla_norm, gla_w_out, diff_w_in, diff_lam_q1, diff_lam_k1, diff_lam_q2, diff_lam_k2, diff_norm, diff_w_out, norm_mixer, norm_ffn, ffn_w_in, ffn_w_out, norm_final):
    batch, seq, d = x.shape
    x2d = x.reshape(batch * seq, d)
    ffn_w_in_bf, ffn_w_out_bf = ffn_w_in.astype(BF16), ffn_w_out.astype(BF16)

    proj, gk, chunk_decay = _norm_proj_gate(x2d, norm_mixer[0], gla_w_in[0],
                                            gla_w_gate_a[0], gla_w_gate_b[0], gla_b_gate[0])
    mix = _gla_core(proj, gk, chunk_decay, gla_norm[0], batch, seq).reshape(batch * seq, -1)
    x2d = _mix_out_ffn(x2d, mix, gla_w_out[0], 0, norm_ffn, ffn_w_in_bf, ffn_w_out_bf, norm_final,
                       final_norm=False)

    lambda_init = 0.8 - 0.6 * math.exp(-0.3 * 1)
    qk, v_t = _norm_proj_vt(x2d, norm_mixer[1], diff_w_in[0], 2 * d, batch, seq)
    mix = _diff_attn(qk, v_t, diff_lam_q1[0], diff_lam_k1[0], diff_lam_q2[0], diff_lam_k2[0],
                     diff_norm[0], batch, seq, lambda_init).reshape(batch * seq, -1)
    x2d = _mix_out_ffn(x2d, mix, diff_w_out[0], 1, norm_ffn, ffn_w_in_bf, ffn_w_out_bf, norm_final,
                       final_norm=True)
    return x2d.reshape(batch, seq, d)
```

```python
import functools
import math

import jax
import jax.numpy as jnp
from jax import lax
from jax.experimental import pallas as pl
from jax.experimental.pallas import tpu as pltpu

F32 = jnp.float32
BF16 = jnp.bfloat16

NORM_EPS = 1e-6
SUBLN_EPS = 1e-5

GLA_HEADS = 4
GLA_TAU = 16.0
GLA_CHUNK = 64

DIFF_HEADS = 8

V7X_VMEM_BYTES = 64 * 1024 * 1024
VMEM_LIMIT_BYTES = V7X_VMEM_BYTES - 8 * 1024 * 1024

PROJ_ROW_TILE = 1024
FFN_ROW_TILE = 1024
PROJ_COL_TILE = 1024
FFN_COL_TILE = 256
GLA_GROUP = 256
GLA_MAX_FACTORED_DECAY = 48.0
ATTN_TILE = 512
ATTN_SUB_TILE = 256
ATTN_SUM_ROWS = 16
ATTN_LOOKAHEAD = 2
ATTN_HEADS_PER_STEP = 2
ATTN_MAX_FIXED_SHIFT = 40.0
ATTN_BOUND_SLACK = 1.02


def _resident(shape, layer=None):
    if layer is None:
        index, block = (0,) * len(shape), tuple(shape)
    else:
        index, block = (layer,) + (0,) * len(shape), (None,) + tuple(shape)
    return pl.BlockSpec(block, lambda *_: index, pipeline_mode=pl.Buffered(1))


def _rmsnorm_rows(x, gain, eps):
    return x * lax.rsqrt(jnp.mean(x * x, axis=-1, keepdims=True) + eps) * gain


def _dot(a, b):
    return jnp.dot(a, b, preferred_element_type=F32)


def _dot_nt(a, b):
    return lax.dot_general(a, b, (((1,), (1,)), ((), ())), preferred_element_type=F32)


def _dot_tn(a, b):
    return lax.dot_general(a, b, (((0,), (0,)), ((), ())), preferred_element_type=F32)


def _project(h, w_ref, o_ref):
    for c0 in range(0, o_ref.shape[1], PROJ_COL_TILE):
        cols = slice(c0, c0 + PROJ_COL_TILE)
        o_ref[:, cols] = _dot(h, w_ref[:, cols].astype(BF16)).astype(o_ref.dtype)


def _norm_proj_gate_kernel(x_ref, g_ref, w_ref, wa_ref, wb_ref, bg_ref, o_ref, gk_ref, cd_ref):
    h = _rmsnorm_rows(x_ref[...], g_ref[...], NORM_EPS).astype(BF16)
    low = _dot(h, wa_ref[...])
    logits = _dot(low.astype(BF16), wb_ref[...]) + bg_ref[...]
    log_sig = jnp.minimum(logits, 0.0) - jnp.log(1.0 + jnp.exp(-jnp.abs(logits)))
    log2_decay = log_sig * (math.log2(math.e) / GLA_TAU)
    gk_ref[...] = log2_decay
    chunks, dk = cd_ref.shape
    cd_ref[...] = -jnp.sum(log2_decay.reshape(chunks, GLA_CHUNK, dk), axis=1)
    _project(h, w_ref, o_ref)


def _norm_proj_vt_kernel(x_ref, g_ref, w_ref, wv_ref, o_ref, vt_ref, wvt_ref):
    @pl.when(pl.program_id(0) == 0)
    def _():
        wvt_ref[...] = wv_ref[...].T.astype(BF16)

    h = _rmsnorm_rows(x_ref[...], g_ref[...], NORM_EPS).astype(BF16)
    _project(h, w_ref, o_ref)
    tile = vt_ref.shape[3]
    for j in range(vt_ref.shape[1]):
        vt_ref[0, j] = _dot_nt(wvt_ref[...], h[j * tile:(j + 1) * tile]).astype(vt_ref.dtype)


def _norm_proj_call(kernel, name, x2d, gain, w, n_out, extra_in, extra_specs, extra_out_shapes,
                    extra_out_specs, scratch_shapes=(), semantics="parallel"):
    t, d = x2d.shape
    assert t % PROJ_ROW_TILE == 0 and n_out % PROJ_COL_TILE == 0 and w.shape[1] >= n_out
    return pl.pallas_call(
        kernel,
        grid=(t // PROJ_ROW_TILE,),
        in_specs=[pl.BlockSpec((PROJ_ROW_TILE, d), lambda i: (i, 0)), _resident((1, d)),
                  _resident((d, n_out))] + extra_specs,
        out_specs=[pl.BlockSpec((PROJ_ROW_TILE, n_out), lambda i: (i, 0))] + extra_out_specs,
        out_shape=[jax.ShapeDtypeStruct((t, n_out), BF16)] + extra_out_shapes,
        scratch_shapes=list(scratch_shapes),
        compiler_params=pltpu.CompilerParams(
            dimension_semantics=(semantics,), vmem_limit_bytes=VMEM_LIMIT_BYTES),
        name=name,
    )(x2d, gain.reshape(1, d), w, *extra_in)


def _norm_proj_gate(x2d, gain, w, w_a, w_b, b_g):
    t, d = x2d.shape
    rank, dk = w_b.shape
    chunks = PROJ_ROW_TILE // GLA_CHUNK
    return _norm_proj_call(
        _norm_proj_gate_kernel, "norm_proj_gate", x2d, gain, w, w.shape[1],
        [w_a.astype(BF16), w_b.astype(BF16), b_g.reshape(1, dk)],
        [_resident((d, rank)), _resident((rank, dk)), _resident((1, dk))],
        [jax.ShapeDtypeStruct((t, dk), F32), jax.ShapeDtypeStruct((t // GLA_CHUNK, dk), F32)],
        [pl.BlockSpec((PROJ_ROW_TILE, dk), lambda i: (i, 0)), pl.BlockSpec((chunks, dk), lambda i: (i, 0))])


def _norm_proj_vt(x2d, gain, w, n_qk, batch, seq):
    t, d = x2d.shape
    d_v = w.shape[1] - n_qk
    assert n_qk % d_v == 0 and seq % PROJ_ROW_TILE == 0 and PROJ_ROW_TILE % ATTN_TILE == 0
    steps = seq // PROJ_ROW_TILE
    per_step = PROJ_ROW_TILE // ATTN_TILE
    value_cols = pl.BlockSpec((d, d_v), lambda i: (0, n_qk // d_v), pipeline_mode=pl.Buffered(1))
    return _norm_proj_call(
        _norm_proj_vt_kernel, "norm_proj_vt", x2d, gain, w, n_qk, [w], [value_cols],
        [jax.ShapeDtypeStruct((batch, seq // ATTN_TILE, d_v, ATTN_TILE), BF16)],
        [pl.BlockSpec((1, per_step, d_v, ATTN_TILE), lambda i: (i // steps, i % steps, 0, 0))],
        scratch_shapes=[pltpu.VMEM((d_v, d), BF16)], semantics="arbitrary")


def _gla_kernel(q_ref, k_ref, v_ref, r_ref, gk_ref, cd_ref, gn_ref, o_ref):
    c = GLA_CHUNK
    grp = GLA_GROUP
    n = grp // c
    seq, hk = q_ref.shape[1], q_ref.shape[2]
    hv = v_ref.shape[2]
    shift = c.bit_length() - 1
    row = lax.broadcasted_iota(jnp.int32, (grp, grp), 0)
    col = lax.broadcasted_iota(jnp.int32, (grp, grp), 1)
    mask = (col <= row) & ((row >> shift) == (col >> shift))
    tri = jnp.where(mask, 1.0, 0.0).astype(BF16)
    gain = gn_ref[...]
    q_scale = hk ** -0.5

    def rows_of(gi):
        return slice(gi * grp, (gi + 1) * grp)

    def cum_decay(gi):
        g = gk_ref[0, rows_of(gi), :]
        g_hi = g.astype(BF16)
        g_lo = (g - g_hi.astype(F32)).astype(BF16)
        return _dot(tri, g_hi) + _dot(tri, g_lo)

    def scaled_operands(gi, b):
        b = b.reshape(n, c, hk)
        b_mid = b[:, c // 2:c // 2 + 1, :]
        b_last = b[:, c - 1:c, :]
        q = (q_ref[0, rows_of(gi), :].astype(F32) * q_scale).reshape(n, c, hk)
        k = k_ref[0, rows_of(gi), :].astype(F32).reshape(n, c, hk)
        q_mid = (q * jnp.exp2(b - b_mid)).reshape(grp, hk).astype(BF16)
        k_mid = (k * jnp.exp2(b_mid - b)).reshape(grp, hk).astype(BF16)
        q_dec = (q * jnp.exp2(b)).astype(BF16)
        k_end = (k * jnp.exp2(b_last - b)).astype(BF16)
        return q_mid, k_mid, q_dec, k_end, jnp.exp2(b_last)

    def state_part(gi, o_intra, q_dec, decay, incs, state_t):
        outs = []
        for ci in range(n):
            outs.append(o_intra[ci * c:(ci + 1) * c] + _dot_nt(q_dec[ci], state_t.astype(BF16)))
            state_t = state_t * decay[ci] + incs[ci]
        o = _rmsnorm_rows(jnp.concatenate(outs, axis=0), gain, NORM_EPS)
        r = r_ref[0, rows_of(gi), :].astype(F32)
        o_ref[0, rows_of(gi), :] = (o * (r * jax.nn.sigmoid(r))).astype(o_ref.dtype)
        return state_t

    def factored_sweep():
        groups = seq // grp
        state_t = jnp.zeros((hv, hk), F32)
        cum, ops = {}, {}
        for i in range(-2, groups):
            if i >= 0:
                q_mid, k_mid, q_dec, k_end, decay = ops.pop(i)
                v = v_ref[0, rows_of(i), :]
                attn = jnp.where(mask, _dot_nt(q_mid, k_mid), 0.0).astype(BF16)
                incs = [_dot_tn(v[ci * c:(ci + 1) * c], k_end[ci]) for ci in range(n)]
            if i + 2 < groups:
                cum[i + 2] = cum_decay(i + 2)
            if i >= 0:
                o_intra = _dot(attn, v)
            if 0 <= i + 1 < groups:
                ops[i + 1] = scaled_operands(i + 1, cum.pop(i + 1))
            if i >= 0:
                state_t = state_part(i, o_intra, q_dec, decay, incs, state_t)

    def direct_sweep():
        causal = (lax.broadcasted_iota(jnp.int32, (c, c), 1)
                  <= lax.broadcasted_iota(jnp.int32, (c, c), 0))
        tri_c = jnp.where(causal, 1.0, 0.0)
        assert hk >= c
        col_id = lax.broadcasted_iota(jnp.int32, (c, hk), 1)

        def chunk_step(ci, state_t):
            rows = pl.ds(pl.multiple_of(ci * c, c), c)
            b = jnp.dot(tri_c, gk_ref[0, rows, :], preferred_element_type=F32,
                        precision=lax.Precision.HIGHEST)
            q = q_ref[0, rows, :].astype(F32) * q_scale
            k = k_ref[0, rows, :].astype(F32)
            v = v_ref[0, rows, :]
            attn = jnp.zeros((c, hk), F32)
            for j in range(c):
                weight = jnp.exp2(jnp.minimum(b - b[j:j + 1, :], 0.0))
                column = jnp.sum(q * weight * k[j:j + 1, :], axis=-1, keepdims=True)
                attn = jnp.where(col_id == j, column, attn)
            attn = jnp.where(causal, attn[:, 0:c], 0.0).astype(BF16)
            b_last = b[c - 1:c, :]
            o = _dot(attn, v) + _dot_nt((q * jnp.exp2(b)).astype(BF16), state_t.astype(BF16))
            k_end = (k * jnp.exp2(b_last - b)).astype(BF16)
            state_t = state_t * jnp.exp2(b_last) + _dot_tn(v, k_end)
            o = _rmsnorm_rows(o, gain, NORM_EPS)
            r = r_ref[0, rows, :].astype(F32)
            o_ref[0, rows, :] = (o * (r * jax.nn.sigmoid(r))).astype(o_ref.dtype)
            return state_t

        lax.fori_loop(0, seq // c, chunk_step, jnp.zeros((hv, hk), F32))

    decay_is_moderate = jnp.max(cd_ref[0]) <= GLA_MAX_FACTORED_DECAY

    @pl.when(decay_is_moderate)
    def _():
        factored_sweep()

    @pl.when(jnp.logical_not(decay_is_moderate))
    def _():
        direct_sweep()


def _gla_core(proj, gk, chunk_decay, g_norm, batch, seq):
    dk = gk.shape[1]
    hk = dk // GLA_HEADS
    hv = g_norm.shape[0]
    dv = hv * GLA_HEADS
    assert proj.shape[1] == 2 * dk + 2 * dv and seq % GLA_GROUP == 0 and GLA_GROUP % GLA_CHUNK == 0
    proj3 = proj.reshape(batch, seq, proj.shape[1])
    gk3 = gk.reshape(batch, seq, dk)
    cd3 = chunk_decay.reshape(batch, seq // GLA_CHUNK, dk)
    k_off, v_off, r_off = dk // hk, (2 * dk) // hv, (2 * dk + dv) // hv
    return pl.pallas_call(
        _gla_kernel,
        grid=(batch, GLA_HEADS),
        in_specs=[
            pl.BlockSpec((1, seq, hk), lambda b, h: (b, 0, h)),
            pl.BlockSpec((1, seq, hk), lambda b, h: (b, 0, k_off + h)),
            pl.BlockSpec((1, seq, hv), lambda b, h: (b, 0, v_off + h)),
            pl.BlockSpec((1, seq, hv), lambda b, h: (b, 0, r_off + h)),
            pl.BlockSpec((1, seq, hk), lambda b, h: (b, 0, h)),
            pl.BlockSpec((1, seq // GLA_CHUNK, hk), lambda b, h: (b, 0, h)),
            _resident((1, hv)),
        ],
        out_specs=pl.BlockSpec((1, seq, hv), lambda b, h: (b, 0, h)),
        out_shape=jax.ShapeDtypeStruct((batch, seq, dv), BF16),
        compiler_params=pltpu.CompilerParams(
            dimension_semantics=("parallel", "parallel"), vmem_limit_bytes=VMEM_LIMIT_BYTES),
        name="gla_core",
    )(proj3, proj3, proj3, proj3, gk3, cd3, g_norm.reshape(1, hv))


def _diff_attn_kernel(lq1_ref, lk1_ref, lq2_ref, lk2_ref, gn_ref, q_ref, k_ref, vt_ref, o_ref,
                      qq_ref, m_ref, acc_ref, *, lambda_init):
    tiles, tq = vt_ref.shape[1], vt_ref.shape[3]
    heads = ATTN_HEADS_PER_STEP
    d2 = vt_ref.shape[2] // heads
    d = d2 // 2
    ts = ATTN_SUB_TILE
    assert tq % ts == 0

    def head_lanes(hh):
        return slice(hh * d2, (hh + 1) * d2)

    seq = tiles * tq
    lane = lax.broadcasted_iota(jnp.int32, (tq, d2), 1)
    q_scaled = []
    for hh in range(heads):
        qs = (q_ref[0, :, head_lanes(hh)].astype(F32) * (d ** -0.5 * math.log2(math.e))).astype(BF16)
        q_scaled.append(qs)
        for t in range(tiles):
            q = qs[t * tq:(t + 1) * tq]
            qq_ref[hh * tiles + t, 0:tq, :] = jnp.where(lane < d, q, jnp.zeros_like(q))
            qq_ref[hh * tiles + t, tq:2 * tq, :] = jnp.where(lane >= d, q, jnp.zeros_like(q))
    lam = (jnp.exp(jnp.sum(lq1_ref[...] * lk1_ref[...], keepdims=True))
           - jnp.exp(jnp.sum(lq2_ref[...] * lk2_ref[...], keepdims=True)) + lambda_init)
    out_gain = gn_ref[...] * (1.0 - lambda_init)

    def query_ranges(t, kt, off):
        if kt < t or off == 0:
            return [(0, 2 * tq)]
        return [(off, tq), (tq + off, 2 * tq)]

    def scores(hh, t, kt):
        nk = ts if kt == t else tq
        pieces = []
        for off in range(0, tq, nk):
            k = k_ref[0, kt * tq + off:kt * tq + off + nk, head_lanes(hh)]
            for l0, l1 in query_ranges(t, kt, off):
                st = _dot_nt(k, qq_ref[hh * tiles + t, l0:l1, :])
                if kt == t:
                    k_pos = off + lax.broadcasted_iota(jnp.int32, st.shape, 0)
                    q_pos = (l0 + lax.broadcasted_iota(jnp.int32, st.shape, 1)) & (tq - 1)
                    st = jnp.where(k_pos <= q_pos, st, -jnp.inf)
                pieces.append((off, nk, l0, l1, st))
        return pieces

    def accumulate(pieces, hh, kt, shift, alpha):
        edges = sorted({e for _, _, l0, l1, _ in pieces for e in (l0, l1)})
        for a, b in zip(edges[:-1], edges[1:]):
            parts = [(off, nk, st[:, a - l0:b - l0]) for off, nk, l0, l1, st in pieces
                     if l0 <= a and b <= l1]
            p = jnp.concatenate([jnp.exp2(st - shift[:, a:b]).astype(BF16) for _, _, st in parts], axis=0)
            vt = jnp.concatenate([vt_ref[0, kt, head_lanes(hh), off:off + nk] for off, nk, _ in parts],
                                 axis=1)
            vt = jnp.concatenate([vt, jnp.ones((ATTN_SUM_ROWS, vt.shape[1]), BF16)], axis=0)
            prev = acc_ref[:, a:b] if alpha is None else alpha[:, a:b] * acc_ref[:, a:b]
            acc_ref[:, a:b] = prev + _dot(vt, p)

    def update_running_max(pieces, hh, t, kt):
        m_prev = m_ref[...]
        for _, _, l0, l1, st in pieces:
            m_ref[:, l0:l1] = jnp.maximum(m_ref[:, l0:l1], jnp.max(st, axis=0, keepdims=True))
        m_new = m_ref[...]
        accumulate(pieces, hh, kt, m_new, jnp.exp2(m_prev - m_new))

    sel_row = lax.broadcasted_iota(jnp.int32, (16, d2), 0)
    sel_lane = lax.broadcasted_iota(jnp.int32, (16, d2), 1)
    norm_rows = jnp.where((sel_row == 0) | ((sel_row == 1) & (sel_lane < d))
                          | ((sel_row == 2) & (sel_lane >= d)), 1.0, 0.0).astype(BF16)

    def score_bound(hh):
        k = k_ref[0, :, head_lanes(hh)]
        qs = q_scaled[hh]
        norm2 = _dot_nt(norm_rows, jnp.concatenate([k * k, qs * qs], axis=0))
        k_norm2 = jnp.max(norm2[0:1, 0:seq], keepdims=True)
        q_norm2 = jnp.max(norm2[1:3, seq:2 * seq], keepdims=True)
        return jnp.sqrt(q_norm2 * k_norm2) * ATTN_BOUND_SLACK

    bound = [score_bound(hh) for hh in range(heads)]
    worst = jnp.max(functools.reduce(jnp.maximum, bound))
    shift_rows = [jnp.broadcast_to(b, (1, 2 * tq)) for b in bound]

    def update_fixed_shift(pieces, hh, t, kt):
        accumulate(pieces, hh, kt, shift_rows[hh], None)

    def finalize(hh, t):
        o_all = acc_ref[0:d2, :] / acc_ref[d2:d2 + 1, :]
        o_t = o_all[:, 0:tq] - lam * o_all[:, tq:2 * tq]
        inv_rms = lax.rsqrt(jnp.mean(o_t * o_t, axis=0, keepdims=True) + SUBLN_EPS)
        o_ref[0, t * tq:(t + 1) * tq, head_lanes(hh)] = (o_t * inv_rms * out_gain).T.astype(o_ref.dtype)

    units = [(hh, t, kt) for hh in range(heads) for t in range(tiles) for kt in range(t + 1)]

    def sweep(update):
        pending = {}
        for i in range(len(units) + ATTN_LOOKAHEAD):
            if i < len(units):
                pending[i] = scores(*units[i])
            j = i - ATTN_LOOKAHEAD
            if j >= 0:
                hh, t, kt = units[j]
                if kt == 0:
                    m_ref[...] = jnp.full_like(m_ref, -jnp.inf)
                    acc_ref[...] = jnp.zeros_like(acc_ref)
                update(pending.pop(j), hh, t, kt)
                if kt == t:
                    finalize(hh, t)

    bound_is_small = worst <= ATTN_MAX_FIXED_SHIFT

    @pl.when(bound_is_small)
    def _():
        sweep(update_fixed_shift)

    @pl.when(jnp.logical_not(bound_is_small))
    def _():
        sweep(update_running_max)


def _diff_attn(qk, v_t, lam_q1, lam_k1, lam_q2, lam_k2, g_norm, batch, seq, lambda_init):
    d_model = qk.shape[1] // 2
    d2 = d_model // DIFF_HEADS
    d = d2 // 2
    tq = ATTN_TILE
    tiles = seq // tq
    heads = ATTN_HEADS_PER_STEP
    assert seq % tq == 0 and g_norm.shape[0] == d2 and v_t.shape == (batch, tiles, d_model, tq)
    assert DIFF_HEADS % heads == 0
    qk3 = qk.reshape(batch, seq, 2 * d_model)
    k_off = DIFF_HEADS // heads
    return pl.pallas_call(
        functools.partial(_diff_attn_kernel, lambda_init=lambda_init),
        grid=(batch, DIFF_HEADS // heads),
        in_specs=[_resident((1, d))] * 4 + [
            _resident((d2, 1)),
            pl.BlockSpec((1, seq, heads * d2), lambda b, h: (b, 0, h)),
            pl.BlockSpec((1, seq, heads * d2), lambda b, h: (b, 0, k_off + h)),
            pl.BlockSpec((1, tiles, heads * d2, tq), lambda b, h: (b, 0, h, 0)),
        ],
        out_specs=pl.BlockSpec((1, seq, heads * d2), lambda b, h: (b, 0, h)),
        out_shape=jax.ShapeDtypeStruct((batch, seq, d_model), BF16),
        scratch_shapes=[
            pltpu.VMEM((heads * tiles, 2 * tq, d2), BF16),
            pltpu.VMEM((1, 2 * tq), F32),
            pltpu.VMEM((d2 + ATTN_SUM_ROWS, 2 * tq), F32),
        ],
        compiler_params=pltpu.CompilerParams(
            dimension_semantics=("parallel", "parallel"), vmem_limit_bytes=VMEM_LIMIT_BYTES),
        name="diff_attn",
    )(lam_q1.reshape(1, d), lam_k1.reshape(1, d), lam_q2.reshape(1, d), lam_k2.reshape(1, d),
      g_norm.reshape(d2, 1), qk3, qk3, v_t)


def _mix_out_ffn_kernel(x_ref, a_ref, wo_ref, gf_ref, wi_ref, w2_ref, gl_ref, o_ref, acc_ref, *, final_norm):
    d_ff = w2_ref.shape[0]
    x1 = x_ref[...] + _dot(a_ref[...], wo_ref[...])
    h = _rmsnorm_rows(x1, gf_ref[...], NORM_EPS).astype(BF16)
    acc_ref[...] = x1
    for f0 in range(0, d_ff, FFN_COL_TILE):
        gate = _dot(h, wi_ref[:, f0:f0 + FFN_COL_TILE])
        up = _dot(h, wi_ref[:, d_ff + f0:d_ff + f0 + FFN_COL_TILE])
        act = (gate * jax.nn.sigmoid(gate) * up).astype(BF16)
        acc_ref[...] += _dot(act, w2_ref[f0:f0 + FFN_COL_TILE, :])
    x2 = acc_ref[...]
    if final_norm:
        x2 = _rmsnorm_rows(x2, gl_ref[...], NORM_EPS)
    o_ref[...] = x2


def _mix_out_ffn(x2d, mix, w_out, layer, g_ffn_all, w_in_all, w2_all, g_last, final_norm):
    t, d = x2d.shape
    d_ff = w2_all.shape[1]
    assert t % FFN_ROW_TILE == 0 and d_ff % FFN_COL_TILE == 0
    row_spec = pl.BlockSpec((FFN_ROW_TILE, d), lambda i: (i, 0))
    return pl.pallas_call(
        functools.partial(_mix_out_ffn_kernel, final_norm=final_norm),
        grid=(t // FFN_ROW_TILE,),
        in_specs=[row_spec, row_spec, _resident((d, d)), _resident((1, d), layer),
                  _resident((d, 2 * d_ff), layer), _resident((d_ff, d), layer), _resident((1, d))],
        out_specs=row_spec,
        out_shape=jax.ShapeDtypeStruct((t, d), F32),
        scratch_shapes=[pltpu.VMEM((FFN_ROW_TILE, d), F32)],
        compiler_params=pltpu.CompilerParams(
            dimension_semantics=("parallel",), vmem_limit_bytes=VMEM_LIMIT_BYTES),
        name="mix_out_ffn",
    )(x2d, mix, w_out.astype(BF16), g_ffn_all.reshape(-1, 1, d), w_in_all, w2_all, g_last.reshape(1, d))


def kernel(x, gla_w_in, gla_w_gate_a, gla_w_gate_b, gla_b_gate, gla_norm, gla_w_out, diff_w_in, diff_lam_q1, diff_lam_k1, diff_lam_q2, diff_lam_k2, diff_norm, diff_w_out, norm_mixer, norm_ffn, ffn_w_in, ffn_w_out, norm_final):
    batch, seq, d = x.shape
    x2d = x.reshape(batch * seq, d)
    ffn_w_in_bf, ffn_w_out_bf = ffn_w_in.astype(BF16), ffn_w_out.astype(BF16)

    proj, gk, chunk_decay = _norm_proj_gate(x2d, norm_mixer[0], gla_w_in[0],
                                            gla_w_gate_a[0], gla_w_gate_b[0], gla_b_gate[0])
    mix = _gla_core(proj, gk, chunk_decay, gla_norm[0], batch, seq).reshape(batch * seq, -1)
    x2d = _mix_out_ffn(x2d, mix, gla_w_out[0], 0, norm_ffn, ffn_w_in_bf, ffn_w_out_bf, norm_final,
                       final_norm=False)

    lambda_init = 0.8 - 0.6 * math.exp(-0.3 * 1)
    qk, v_t = _norm_proj_vt(x2d, norm_mixer[1], diff_w_in[0], 2 * d, batch, seq)
    mix = _diff_attn(qk, v_t, diff_lam_q1[0], diff_lam_k1[0], diff_lam_q2[0], diff_lam_k2[0],
                     diff_norm[0], batch, seq, lambda_init).reshape(batch * seq, -1)
    x2d = _mix_out_ffn(x2d, mix, diff_w_out[0], 1, norm_ffn, ffn_w_in_bf, ffn_w_out_bf, norm_final,
                       final_norm=True)
    return x2d.reshape(batch, seq, d)
```

```python
import functools
import math

import jax
import jax.numpy as jnp
from jax import lax
from jax.experimental import pallas as pl
from jax.experimental.pallas import tpu as pltpu

F32 = jnp.float32
BF16 = jnp.bfloat16

NORM_EPS = 1e-6
SUBLN_EPS = 1e-5

GLA_HEADS = 4
GLA_TAU = 16.0
GLA_CHUNK = 64

DIFF_HEADS = 8

V7X_VMEM_BYTES = 64 * 1024 * 1024
VMEM_LIMIT_BYTES = V7X_VMEM_BYTES - 8 * 1024 * 1024

PROJ_ROW_TILE = 1024
FFN_ROW_TILE = 1024
PROJ_COL_TILE = 1024
FFN_COL_TILE = 256
GLA_HEADS_PER_STEP = 2
GLA_GROUP = 256
GLA_MAX_FACTORED_DECAY = 48.0
ATTN_TILE = 512
ATTN_SUB_TILE = 256
ATTN_SUM_ROWS = 16
ATTN_LOOKAHEAD = 2
ATTN_HEADS_PER_STEP = 2
ATTN_MAX_FIXED_SHIFT = 40.0
ATTN_BOUND_SLACK = 1.02


def _resident(shape, layer=None):
    if layer is None:
        index, block = (0,) * len(shape), tuple(shape)
    else:
        index, block = (layer,) + (0,) * len(shape), (None,) + tuple(shape)
    return pl.BlockSpec(block, lambda *_: index, pipeline_mode=pl.Buffered(1))


def _rmsnorm_rows(x, gain, eps):
    return x * lax.rsqrt(jnp.mean(x * x, axis=-1, keepdims=True) + eps) * gain


def _dot(a, b):
    return jnp.dot(a, b, preferred_element_type=F32)


def _dot_nt(a, b):
    return lax.dot_general(a, b, (((1,), (1,)), ((), ())), preferred_element_type=F32)


def _dot_tn(a, b):
    return lax.dot_general(a, b, (((0,), (0,)), ((), ())), preferred_element_type=F32)


def _project(h, w_ref, o_ref):
    for c0 in range(0, o_ref.shape[1], PROJ_COL_TILE):
        cols = slice(c0, c0 + PROJ_COL_TILE)
        o_ref[:, cols] = _dot(h, w_ref[:, cols].astype(BF16)).astype(o_ref.dtype)


def _norm_proj_gate_kernel(x_ref, g_ref, w_ref, wa_ref, wb_ref, bg_ref, o_ref, gk_ref, cd_ref):
    h = _rmsnorm_rows(x_ref[...], g_ref[...], NORM_EPS).astype(BF16)
    low = _dot(h, wa_ref[...])
    logits = _dot(low.astype(BF16), wb_ref[...]) + bg_ref[...]
    log_sig = jnp.minimum(logits, 0.0) - jnp.log(1.0 + jnp.exp(-jnp.abs(logits)))
    log2_decay = log_sig * (math.log2(math.e) / GLA_TAU)
    gk_ref[...] = log2_decay
    chunks, dk = cd_ref.shape
    cd_ref[...] = -jnp.sum(log2_decay.reshape(chunks, GLA_CHUNK, dk), axis=1)
    _project(h, w_ref, o_ref)


def _norm_proj_vt_kernel(x_ref, g_ref, w_ref, wv_ref, o_ref, vt_ref, wvt_ref):
    @pl.when(pl.program_id(0) == 0)
    def _():
        wvt_ref[...] = wv_ref[...].T.astype(BF16)

    h = _rmsnorm_rows(x_ref[...], g_ref[...], NORM_EPS).astype(BF16)
    _project(h, w_ref, o_ref)
    tile = vt_ref.shape[3]
    for j in range(vt_ref.shape[1]):
        vt_ref[0, j] = _dot_nt(wvt_ref[...], h[j * tile:(j + 1) * tile]).astype(vt_ref.dtype)


def _norm_proj_call(kernel, name, x2d, gain, w, n_out, extra_in, extra_specs, extra_out_shapes,
                    extra_out_specs, scratch_shapes=(), semantics="parallel"):
    t, d = x2d.shape
    assert t % PROJ_ROW_TILE == 0 and n_out % PROJ_COL_TILE == 0 and w.shape[1] >= n_out
    return pl.pallas_call(
        kernel,
        grid=(t // PROJ_ROW_TILE,),
        in_specs=[pl.BlockSpec((PROJ_ROW_TILE, d), lambda i: (i, 0)), _resident((1, d)),
                  _resident((d, n_out))] + extra_specs,
        out_specs=[pl.BlockSpec((PROJ_ROW_TILE, n_out), lambda i: (i, 0))] + extra_out_specs,
        out_shape=[jax.ShapeDtypeStruct((t, n_out), BF16)] + extra_out_shapes,
        scratch_shapes=list(scratch_shapes),
        compiler_params=pltpu.CompilerParams(
            dimension_semantics=(semantics,), vmem_limit_bytes=VMEM_LIMIT_BYTES),
        name=name,
    )(x2d, gain.reshape(1, d), w, *extra_in)


def _norm_proj_gate(x2d, gain, w, w_a, w_b, b_g):
    t, d = x2d.shape
    rank, dk = w_b.shape
    chunks = PROJ_ROW_TILE // GLA_CHUNK
    return _norm_proj_call(
        _norm_proj_gate_kernel, "norm_proj_gate", x2d, gain, w, w.shape[1],
        [w_a.astype(BF16), w_b.astype(BF16), b_g.reshape(1, dk)],
        [_resident((d, rank)), _resident((rank, dk)), _resident((1, dk))],
        [jax.ShapeDtypeStruct((t, dk), F32), jax.ShapeDtypeStruct((t // GLA_CHUNK, dk), F32)],
        [pl.BlockSpec((PROJ_ROW_TILE, dk), lambda i: (i, 0)), pl.BlockSpec((chunks, dk), lambda i: (i, 0))])


def _norm_proj_vt(x2d, gain, w, n_qk, batch, seq):
    t, d = x2d.shape
    d_v = w.shape[1] - n_qk
    assert n_qk % d_v == 0 and seq % PROJ_ROW_TILE == 0 and PROJ_ROW_TILE % ATTN_TILE == 0
    steps = seq // PROJ_ROW_TILE
    per_step = PROJ_ROW_TILE // ATTN_TILE
    value_cols = pl.BlockSpec((d, d_v), lambda i: (0, n_qk // d_v), pipeline_mode=pl.Buffered(1))
    return _norm_proj_call(
        _norm_proj_vt_kernel, "norm_proj_vt", x2d, gain, w, n_qk, [w], [value_cols],
        [jax.ShapeDtypeStruct((batch, seq // ATTN_TILE, d_v, ATTN_TILE), BF16)],
        [pl.BlockSpec((1, per_step, d_v, ATTN_TILE), lambda i: (i // steps, i % steps, 0, 0))],
        scratch_shapes=[pltpu.VMEM((d_v, d), BF16)], semantics="arbitrary")


def _gla_kernel(q_ref, k_ref, v_ref, r_ref, gk_ref, cd_ref, gn_ref, o_ref):
    c = GLA_CHUNK
    grp = GLA_GROUP
    n = grp // c
    heads = GLA_HEADS_PER_STEP
    seq, hk = q_ref.shape[1], q_ref.shape[2] // heads
    hv = v_ref.shape[2] // heads
    shift = c.bit_length() - 1
    row = lax.broadcasted_iota(jnp.int32, (grp, grp), 0)
    col = lax.broadcasted_iota(jnp.int32, (grp, grp), 1)
    mask = (col <= row) & ((row >> shift) == (col >> shift))
    tri = jnp.where(mask, 1.0, 0.0).astype(BF16)
    gain = gn_ref[...]
    q_scale = hk ** -0.5

    def rows_of(gi):
        return slice(gi * grp, (gi + 1) * grp)

    def key_lanes(hh):
        return slice(hh * hk, (hh + 1) * hk)

    def value_lanes(hh):
        return slice(hh * hv, (hh + 1) * hv)

    def cum_decay(hh, gi):
        g = gk_ref[0, rows_of(gi), key_lanes(hh)]
        g_hi = g.astype(BF16)
        g_lo = (g - g_hi.astype(F32)).astype(BF16)
        return _dot(tri, g_hi) + _dot(tri, g_lo)

    def scaled_operands(hh, gi, b):
        b = b.reshape(n, c, hk)
        b_mid = b[:, c // 2:c // 2 + 1, :]
        b_last = b[:, c - 1:c, :]
        q = (q_ref[0, rows_of(gi), key_lanes(hh)].astype(F32) * q_scale).reshape(n, c, hk)
        k = k_ref[0, rows_of(gi), key_lanes(hh)].astype(F32).reshape(n, c, hk)
        q_mid = (q * jnp.exp2(b - b_mid)).reshape(grp, hk).astype(BF16)
        k_mid = (k * jnp.exp2(b_mid - b)).reshape(grp, hk).astype(BF16)
        q_dec = (q * jnp.exp2(b)).astype(BF16)
        k_end = (k * jnp.exp2(b_last - b)).astype(BF16)
        return q_mid, k_mid, q_dec, k_end, jnp.exp2(b_last)

    def state_part(hh, gi, o_intra, q_dec, decay, incs, state_t):
        outs = []
        for ci in range(n):
            outs.append(o_intra[ci * c:(ci + 1) * c] + _dot_nt(q_dec[ci], state_t.astype(BF16)))
            state_t = state_t * decay[ci] + incs[ci]
        o = _rmsnorm_rows(jnp.concatenate(outs, axis=0), gain, NORM_EPS)
        r = r_ref[0, rows_of(gi), value_lanes(hh)].astype(F32)
        o_ref[0, rows_of(gi), value_lanes(hh)] = (o * (r * jax.nn.sigmoid(r))).astype(o_ref.dtype)
        return state_t

    def factored_sweep(hh):
        groups = seq // grp
        state_t = jnp.zeros((hv, hk), F32)
        cum, ops = {}, {}
        for i in range(-2, groups):
            if i >= 0:
                q_mid, k_mid, q_dec, k_end, decay = ops.pop(i)
                v = v_ref[0, rows_of(i), value_lanes(hh)]
                attn = jnp.where(mask, _dot_nt(q_mid, k_mid), 0.0).astype(BF16)
                incs = [_dot_tn(v[ci * c:(ci + 1) * c], k_end[ci]) for ci in range(n)]
            if i + 2 < groups:
                cum[i + 2] = cum_decay(hh, i + 2)
            if i >= 0:
                o_intra = _dot(attn, v)
            if 0 <= i + 1 < groups:
                ops[i + 1] = scaled_operands(hh, i + 1, cum.pop(i + 1))
            if i >= 0:
                state_t = state_part(hh, i, o_intra, q_dec, decay, incs, state_t)

    def direct_sweep(hh):
        causal = (lax.broadcasted_iota(jnp.int32, (c, c), 1)
                  <= lax.broadcasted_iota(jnp.int32, (c, c), 0))
        tri_c = jnp.where(causal, 1.0, 0.0)
        assert hk >= c
        col_id = lax.broadcasted_iota(jnp.int32, (c, hk), 1)

        def chunk_step(ci, state_t):
            rows = pl.ds(pl.multiple_of(ci * c, c), c)
            b = jnp.dot(tri_c, gk_ref[0, rows, key_lanes(hh)], preferred_element_type=F32,
                        precision=lax.Precision.HIGHEST)
            q = q_ref[0, rows, key_lanes(hh)].astype(F32) * q_scale
            k = k_ref[0, rows, key_lanes(hh)].astype(F32)
            v = v_ref[0, rows, value_lanes(hh)]
            attn = jnp.zeros((c, hk), F32)
            for j in range(c):
                weight = jnp.exp2(jnp.minimum(b - b[j:j + 1, :], 0.0))
                column = jnp.sum(q * weight * k[j:j + 1, :], axis=-1, keepdims=True)
                attn = jnp.where(col_id == j, column, attn)
            attn = jnp.where(causal, attn[:, 0:c], 0.0).astype(BF16)
            b_last = b[c - 1:c, :]
            o = _dot(attn, v) + _dot_nt((q * jnp.exp2(b)).astype(BF16), state_t.astype(BF16))
            k_end = (k * jnp.exp2(b_last - b)).astype(BF16)
            state_t = state_t * jnp.exp2(b_last) + _dot_tn(v, k_end)
            o = _rmsnorm_rows(o, gain, NORM_EPS)
            r = r_ref[0, rows, value_lanes(hh)].astype(F32)
            o_ref[0, rows, value_lanes(hh)] = (o * (r * jax.nn.sigmoid(r))).astype(o_ref.dtype)
            return state_t

        lax.fori_loop(0, seq // c, chunk_step, jnp.zeros((hv, hk), F32))

    decay_is_moderate = jnp.max(cd_ref[0]) <= GLA_MAX_FACTORED_DECAY

    @pl.when(decay_is_moderate)
    def _():
        for hh in range(heads):
            factored_sweep(hh)

    @pl.when(jnp.logical_not(decay_is_moderate))
    def _():
        for hh in range(heads):
            direct_sweep(hh)


def _gla_core(proj, gk, chunk_decay, g_norm, batch, seq):
    dk = gk.shape[1]
    hk = dk // GLA_HEADS
    hv = g_norm.shape[0]
    dv = hv * GLA_HEADS
    assert proj.shape[1] == 2 * dk + 2 * dv and seq % GLA_GROUP == 0 and GLA_GROUP % GLA_CHUNK == 0
    proj3 = proj.reshape(batch, seq, proj.shape[1])
    gk3 = gk.reshape(batch, seq, dk)
    cd3 = chunk_decay.reshape(batch, seq // GLA_CHUNK, dk)
    heads = GLA_HEADS_PER_STEP
    assert GLA_HEADS % heads == 0
    wk, wv = heads * hk, heads * hv
    k_off, v_off, r_off = dk // wk, (2 * dk) // wv, (2 * dk + dv) // wv
    return pl.pallas_call(
        _gla_kernel,
        grid=(batch, GLA_HEADS // heads),
        in_specs=[
            pl.BlockSpec((1, seq, wk), lambda b, h: (b, 0, h)),
            pl.BlockSpec((1, seq, wk), lambda b, h: (b, 0, k_off + h)),
            pl.BlockSpec((1, seq, wv), lambda b, h: (b, 0, v_off + h)),
            pl.BlockSpec((1, seq, wv), lambda b, h: (b, 0, r_off + h)),
            pl.BlockSpec((1, seq, wk), lambda b, h: (b, 0, h)),
            pl.BlockSpec((1, seq // GLA_CHUNK, wk), lambda b, h: (b, 0, h)),
            _resident((1, hv)),
        ],
        out_specs=pl.BlockSpec((1, seq, wv), lambda b, h: (b, 0, h)),
        out_shape=jax.ShapeDtypeStruct((batch, seq, dv), BF16),
        compiler_params=pltpu.CompilerParams(
            dimension_semantics=("parallel", "parallel"), vmem_limit_bytes=VMEM_LIMIT_BYTES),
        name="gla_core",
    )(proj3, proj3, proj3, proj3, gk3, cd3, g_norm.reshape(1, hv))


def _diff_attn_kernel(lq1_ref, lk1_ref, lq2_ref, lk2_ref, gn_ref, q_ref, k_ref, vt_ref, o_ref,
                      qq_ref, m_ref, acc_ref, *, lambda_init):
    tiles, tq = vt_ref.shape[1], vt_ref.shape[3]
    heads = ATTN_HEADS_PER_STEP
    d2 = vt_ref.shape[2] // heads
    d = d2 // 2
    ts = ATTN_SUB_TILE
    assert tq % ts == 0

    def head_lanes(hh):
        return slice(hh * d2, (hh + 1) * d2)

    seq = tiles * tq
    lane = lax.broadcasted_iota(jnp.int32, (tq, d2), 1)
    q_scaled = []
    for hh in range(heads):
        qs = (q_ref[0, :, head_lanes(hh)].astype(F32) * (d ** -0.5 * math.log2(math.e))).astype(BF16)
        q_scaled.append(qs)
        for t in range(tiles):
            q = qs[t * tq:(t + 1) * tq]
            qq_ref[hh * tiles + t, 0:tq, :] = jnp.where(lane < d, q, jnp.zeros_like(q))
            qq_ref[hh * tiles + t, tq:2 * tq, :] = jnp.where(lane >= d, q, jnp.zeros_like(q))
    lam = (jnp.exp(jnp.sum(lq1_ref[...] * lk1_ref[...], keepdims=True))
           - jnp.exp(jnp.sum(lq2_ref[...] * lk2_ref[...], keepdims=True)) + lambda_init)
    out_gain = gn_ref[...] * (1.0 - lambda_init)

    def query_ranges(t, kt, off):
        if kt < t or off == 0:
            return [(0, 2 * tq)]
        return [(off, tq), (tq + off, 2 * tq)]

    def scores(hh, t, kt):
        nk = ts if kt == t else tq
        pieces = []
        for off in range(0, tq, nk):
            k = k_ref[0, kt * tq + off:kt * tq + off + nk, head_lanes(hh)]
            for l0, l1 in query_ranges(t, kt, off):
                st = _dot_nt(k, qq_ref[hh * tiles + t, l0:l1, :])
                if kt == t:
                    k_pos = off + lax.broadcasted_iota(jnp.int32, st.shape, 0)
                    q_pos = (l0 + lax.broadcasted_iota(jnp.int32, st.shape, 1)) & (tq - 1)
                    st = jnp.where(k_pos <= q_pos, st, -jnp.inf)
                pieces.append((off, nk, l0, l1, st))
        return pieces

    def accumulate(pieces, hh, kt, shift, alpha):
        edges = sorted({e for _, _, l0, l1, _ in pieces for e in (l0, l1)})
        for a, b in zip(edges[:-1], edges[1:]):
            parts = [(off, nk, st[:, a - l0:b - l0]) for off, nk, l0, l1, st in pieces
                     if l0 <= a and b <= l1]
            p = jnp.concatenate([jnp.exp2(st - shift[:, a:b]).astype(BF16) for _, _, st in parts], axis=0)
            vt = jnp.concatenate([vt_ref[0, kt, head_lanes(hh), off:off + nk] for off, nk, _ in parts],
                                 axis=1)
            vt = jnp.concatenate([vt, jnp.ones((ATTN_SUM_ROWS, vt.shape[1]), BF16)], axis=0)
            prev = acc_ref[:, a:b] if alpha is None else alpha[:, a:b] * acc_ref[:, a:b]
            acc_ref[:, a:b] = prev + _dot(vt, p)

    def update_running_max(pieces, hh, t, kt):
        m_prev = m_ref[...]
        for _, _, l0, l1, st in pieces:
            m_ref[:, l0:l1] = jnp.maximum(m_ref[:, l0:l1], jnp.max(st, axis=0, keepdims=True))
        m_new = m_ref[...]
        accumulate(pieces, hh, kt, m_new, jnp.exp2(m_prev - m_new))

    sel_row = lax.broadcasted_iota(jnp.int32, (16, d2), 0)
    sel_lane = lax.broadcasted_iota(jnp.int32, (16, d2), 1)
    norm_rows = jnp.where((sel_row == 0) | ((sel_row == 1) & (sel_lane < d))
                          | ((sel_row == 2) & (sel_lane >= d)), 1.0, 0.0).astype(BF16)

    def score_bound(hh):
        k = k_ref[0, :, head_lanes(hh)]
        qs = q_scaled[hh]
        norm2 = _dot_nt(norm_rows, jnp.concatenate([k * k, qs * qs], axis=0))
        k_norm2 = jnp.max(norm2[0:1, 0:seq], keepdims=True)
        q_norm2 = jnp.max(norm2[1:3, seq:2 * seq], keepdims=True)
        return jnp.sqrt(q_norm2 * k_norm2) * ATTN_BOUND_SLACK

    bound = [score_bound(hh) for hh in range(heads)]
    worst = jnp.max(functools.reduce(jnp.maximum, bound))
    shift_rows = [jnp.broadcast_to(b, (1, 2 * tq)) for b in bound]

    def update_fixed_shift(pieces, hh, t, kt):
        accumulate(pieces, hh, kt, shift_rows[hh], None)

    def finalize(hh, t):
        o_all = acc_ref[0:d2, :] / acc_ref[d2:d2 + 1, :]
        o_t = o_all[:, 0:tq] - lam * o_all[:, tq:2 * tq]
        inv_rms = lax.rsqrt(jnp.mean(o_t * o_t, axis=0, keepdims=True) + SUBLN_EPS)
        o_ref[0, t * tq:(t + 1) * tq, head_lanes(hh)] = (o_t * inv_rms * out_gain).T.astype(o_ref.dtype)

    units = [(hh, t, kt) for hh in range(heads) for t in range(tiles) for kt in range(t + 1)]

    def sweep(update):
        pending = {}
        for i in range(len(units) + ATTN_LOOKAHEAD):
            if i < len(units):
                pending[i] = scores(*units[i])
            j = i - ATTN_LOOKAHEAD
            if j >= 0:
                hh, t, kt = units[j]
                if kt == 0:
                    m_ref[...] = jnp.full_like(m_ref, -jnp.inf)
                    acc_ref[...] = jnp.zeros_like(acc_ref)
                update(pending.pop(j), hh, t, kt)
                if kt == t:
                    finalize(hh, t)

    bound_is_small = worst <= ATTN_MAX_FIXED_SHIFT

    @pl.when(bound_is_small)
    def _():
        sweep(update_fixed_shift)

    @pl.when(jnp.logical_not(bound_is_small))
    def _():
        sweep(update_running_max)


def _diff_attn(qk, v_t, lam_q1, lam_k1, lam_q2, lam_k2, g_norm, batch, seq, lambda_init):
    d_model = qk.shape[1] // 2
    d2 = d_model // DIFF_HEADS
    d = d2 // 2
    tq = ATTN_TILE
    tiles = seq // tq
    heads = ATTN_HEADS_PER_STEP
    assert seq % tq == 0 and g_norm.shape[0] == d2 and v_t.shape == (batch, tiles, d_model, tq)
    assert DIFF_HEADS % heads == 0
    qk3 = qk.reshape(batch, seq, 2 * d_model)
    k_off = DIFF_HEADS // heads
    return pl.pallas_call(
        functools.partial(_diff_attn_kernel, lambda_init=lambda_init),
        grid=(batch, DIFF_HEADS // heads),
        in_specs=[_resident((1, d))] * 4 + [
            _resident((d2, 1)),
            pl.BlockSpec((1, seq, heads * d2), lambda b, h: (b, 0, h)),
            pl.BlockSpec((1, seq, heads * d2), lambda b, h: (b, 0, k_off + h)),
            pl.BlockSpec((1, tiles, heads * d2, tq), lambda b, h: (b, 0, h, 0)),
        ],
        out_specs=pl.BlockSpec((1, seq, heads * d2), lambda b, h: (b, 0, h)),
        out_shape=jax.ShapeDtypeStruct((batch, seq, d_model), BF16),
        scratch_shapes=[
            pltpu.VMEM((heads * tiles, 2 * tq, d2), BF16),
            pltpu.VMEM((1, 2 * tq), F32),
            pltpu.VMEM((d2 + ATTN_SUM_ROWS, 2 * tq), F32),
        ],
        compiler_params=pltpu.CompilerParams(
            dimension_semantics=("parallel", "parallel"), vmem_limit_bytes=VMEM_LIMIT_BYTES),
        name="diff_attn",
    )(lam_q1.reshape(1, d), lam_k1.reshape(1, d), lam_q2.reshape(1, d), lam_k2.reshape(1, d),
      g_norm.reshape(d2, 1), qk3, qk3, v_t)


def _mix_out_ffn_kernel(x_ref, a_ref, wo_ref, gf_ref, wi_ref, w2_ref, gl_ref, o_ref, acc_ref, *, final_norm):
    d_ff = w2_ref.shape[0]
    x1 = x_ref[...] + _dot(a_ref[...], wo_ref[...])
    h = _rmsnorm_rows(x1, gf_ref[...], NORM_EPS).astype(BF16)
    acc_ref[...] = x1
    for f0 in range(0, d_ff, FFN_COL_TILE):
        gate = _dot(h, wi_ref[:, f0:f0 + FFN_COL_TILE])
        up = _dot(h, wi_ref[:, d_ff + f0:d_ff + f0 + FFN_COL_TILE])
        act = (gate * jax.nn.sigmoid(gate) * up).astype(BF16)
        acc_ref[...] += _dot(act, w2_ref[f0:f0 + FFN_COL_TILE, :])
    x2 = acc_ref[...]
    if final_norm:
        x2 = _rmsnorm_rows(x2, gl_ref[...], NORM_EPS)
    o_ref[...] = x2


def _mix_out_ffn(x2d, mix, w_out, layer, g_ffn_all, w_in_all, w2_all, g_last, final_norm):
    t, d = x2d.shape
    d_ff = w2_all.shape[1]
    assert t % FFN_ROW_TILE == 0 and d_ff % FFN_COL_TILE == 0
    row_spec = pl.BlockSpec((FFN_ROW_TILE, d), lambda i: (i, 0))
    return pl.pallas_call(
        functools.partial(_mix_out_ffn_kernel, final_norm=final_norm),
        grid=(t // FFN_ROW_TILE,),
        in_specs=[row_spec, row_spec, _resident((d, d)), _resident((1, d), layer),
                  _resident((d, 2 * d_ff), layer), _resident((d_ff, d), layer), _resident((1, d))],
        out_specs=row_spec,
        out_shape=jax.ShapeDtypeStruct((t, d), F32),
        scratch_shapes=[pltpu.VMEM((FFN_ROW_TILE, d), F32)],
        compiler_params=pltpu.CompilerParams(
            dimension_semantics=("parallel",), vmem_limit_bytes=VMEM_LIMIT_BYTES),
        name="mix_out_ffn",
    )(x2d, mix, w_out.astype(BF16), g_ffn_all.reshape(-1, 1, d), w_in_all, w2_all, g_last.reshape(1, d))


def kernel(x, gla_w_in, gla_w_gate_a, gla_w_gate_b, gla_b_gate, gla_norm, gla_w_out, diff_w_in, diff_lam_q1, diff_lam_k1, diff_lam_q2, diff_lam_k2, diff_norm, diff_w_out, norm_mixer, norm_ffn, ffn_w_in, ffn_w_out, norm_final):
    batch, seq, d = x.shape
    x2d = x.reshape(batch * seq, d)
    ffn_w_in_bf, ffn_w_out_bf = ffn_w_in.astype(BF16), ffn_w_out.astype(BF16)

    proj, gk, chunk_decay = _norm_proj_gate(x2d, norm_mixer[0], gla_w_in[0],
                                            gla_w_gate_a[0], gla_w_gate_b[0], gla_b_gate[0])
    mix = _gla_core(proj, gk, chunk_decay, gla_norm[0], batch, seq).reshape(batch * seq, -1)
    x2d = _mix_out_ffn(x2d, mix, gla_w_out[0], 0, norm_ffn, ffn_w_in_bf, ffn_w_out_bf, norm_final,
                       final_norm=False)

    lambda_init = 0.8 - 0.6 * math.exp(-0.3 * 1)
    qk, v_t = _norm_proj_vt(x2d, norm_mixer[1], diff_w_in[0], 2 * d, batch, seq)
    mix = _diff_attn(qk, v_t, diff_lam_q1[0], diff_lam_k1[0], diff_lam_q2[0], diff_lam_k2[0],
                     diff_norm[0], batch, seq, lambda_init).reshape(batch * seq, -1)
    x2d = _mix_out_ffn(x2d, mix, diff_w_out[0], 1, norm_ffn, ffn_w_in_bf, ffn_w_out_bf, norm_final,
                       final_norm=True)
    return x2d.reshape(batch, seq, d)
```
